```python
import math
import jax, jax.numpy as jnp
from jax import lax
import numpy as np

D_MODEL = 2048
BATCH = 8
SEQ = 2048
DEPTH = 1

D_HYENA = D_MODEL // 2
HYENA_GROUPS = 8
HYENA_ORDER = 2
HYENA_SHORT = 3
FILTER_EMB = 33
FILTER_ORDER = 64
FAST_DECAY = 0.3
SLOW_DECAY = 1.5
DECAY_TARGET = 1e-2
MOD_SHIFT = 0.0
N_DIR = 2
D_SGU = D_MODEL // 2
SGU_HEADS = 8
SGU_HEAD_DIM = D_SGU // SGU_HEADS
CHUNK = 128
N_BRANCH = 2
D_FF = -(-8 * D_MODEL // (3 * 256)) * 256
D_IN = 3 * D_HYENA + 2 * D_SGU + N_BRANCH * D_MODEL
EPS = 1e-6

kernel_name = "hyena_sgu_gated_hybrid_encoder"

F32 = jnp.float32


def _rmsnorm(x, g):
    x32 = x.astype(F32)
    y = x32 * lax.rsqrt(jnp.mean(x32 * x32, axis=-1, keepdims=True) + EPS)
    return (y * g.astype(F32)).astype(x.dtype)


def _layernorm(x, g, b):
    x32 = x.astype(F32)
    mu = jnp.mean(x32, axis=-1, keepdims=True)
    var = jnp.mean(jnp.square(x32 - mu), axis=-1, keepdims=True)
    y = (x32 - mu) * lax.rsqrt(var + EPS)
    return (y * g.astype(F32) + b.astype(F32)).astype(x.dtype)


def _hyena_filters(L, w1, b1, w2, b2, w3, b3, freq, w4):
    t = jnp.linspace(0.0, 1.0, L, dtype=F32)[:, None]
    bands = (FILTER_EMB - 1) // 2
    w = 2.0 * math.pi * jnp.arange(L, dtype=F32)[:, None] / L
    f = jnp.linspace(1e-4, bands - 1, bands, dtype=F32)[None, :]
    z = jnp.concatenate([t, jnp.cos(f * w), -jnp.sin(f * w)], axis=-1)
    fr = freq.astype(F32)
    h = jnp.sin(fr * (z @ w1.astype(F32) + b1.astype(F32)))
    h = jnp.sin(fr * (h @ w2.astype(F32) + b2.astype(F32)))
    h = jnp.sin(fr * (h @ w3.astype(F32) + b3.astype(F32)))
    h = (h @ w4.astype(F32)).reshape(L, N_DIR, HYENA_ORDER, D_HYENA)
    max_decay = math.log(DECAY_TARGET) / FAST_DECAY
    min_decay = math.log(DECAY_TARGET) / SLOW_DECAY
    deltas = jnp.linspace(min_decay, max_decay, D_HYENA, dtype=F32)
    decay = jnp.exp(-t * jnp.abs(deltas))
    h = h * (decay + MOD_SHIFT)[:, None, None, :]
    h_fwd, h_bwd = h[:, 0], h[:, 1]
    k = jnp.concatenate([h_fwd, jnp.zeros((1, HYENA_ORDER, D_HYENA), F32), h_bwd[:0:-1]], axis=0)
    k = k / jnp.sum(jnp.abs(k), axis=0, keepdims=True)
    return jnp.fft.rfft(k, axis=0)


def _long_conv(z, k_f, bias):
    L = z.shape[1]
    zf = jnp.fft.rfft(z, n=2 * L, axis=1)
    y = jnp.fft.irfft(zf * k_f[None], n=2 * L, axis=1)[:, :L]
    return y + z * bias.astype(F32)


def _hyena_mixer(p, conv_w, conv_b, k_f, bias):
    pp = jnp.pad(p, ((0, 0), (1, 1), (0, 0)))
    p = pp[:, :-2] * conv_w[0] + pp[:, 1:-1] * conv_w[1] + pp[:, 2:] * conv_w[2] + conv_b
    v, x1, x2 = jnp.split(p, 3, axis=-1)
    z = v.astype(F32)
    z = x1.astype(F32) * _long_conv(z, k_f[:, 0], bias[0])
    z = x2.astype(F32) * _long_conv(z, k_f[:, 1], bias[1])
    return z.astype(p.dtype)


def _sgu_mixer(p, ln_g, ln_b, w_s, b_s):
    B, L, _ = p.shape
    u, v = jnp.split(jax.nn.gelu(p, approximate=False), 2, axis=-1)
    v = _layernorm(v, ln_g, ln_b)
    v = v.reshape(B, L // CHUNK, CHUNK, SGU_HEADS, SGU_HEAD_DIM)
    s = jnp.einsum('hpq,bnqhc->bnphc', w_s, v) + b_s.T[None, None, :, :, None]
    return u * s.reshape(B, L, D_SGU)


def setup_inputs(seed: int = 0) -> dict:
    key = jax.random.key(seed)
    ks = jax.random.split(key, 32)

    def nrm(k, shape, scale):
        return jax.random.normal(k, shape, F32) * scale

    return {
        "x": nrm(ks[0], (BATCH, SEQ, D_MODEL), 1.0),
        "norm_mix_g": 1.0 + nrm(ks[1], (DEPTH, D_MODEL), 0.01),
        "w_in": nrm(ks[2], (DEPTH, D_MODEL, D_IN), D_MODEL ** -0.5),
        "short_conv_w": nrm(ks[3], (DEPTH, HYENA_SHORT, 3 * D_HYENA), HYENA_SHORT ** -0.5),
        "short_conv_b": nrm(ks[4], (DEPTH, 3 * D_HYENA), 0.02),
        "filt_w1": nrm(ks[5], (DEPTH, FILTER_EMB, FILTER_ORDER), FILTER_EMB ** -0.5),
        "filt_b1": nrm(ks[6], (DEPTH, FILTER_ORDER), 0.1),
        "filt_w2": nrm(ks[7], (DEPTH, FILTER_ORDER, FILTER_ORDER), FILTER_ORDER ** -0.5),
        "filt_b2": nrm(ks[8], (DEPTH, FILTER_ORDER), 0.1),
        "filt_w3": nrm(ks[9], (DEPTH, FILTER_ORDER, FILTER_ORDER), FILTER_ORDER ** -0.5),
        "filt_b3": nrm(ks[10], (DEPTH, FILTER_ORDER), 0.1),
        "filt_freq": 1.0 + nrm(ks[11], (DEPTH, FILTER_ORDER), 0.01),
        "filt_w4": nrm(ks[12], (DEPTH, FILTER_ORDER, N_DIR * HYENA_ORDER * D_HYENA), FILTER_ORDER ** -0.5),
        "hyena_bias": nrm(ks[13], (DEPTH, HYENA_ORDER, D_HYENA), 0.5),
        "sgu_ln_g": 1.0 + nrm(ks[14], (DEPTH, D_SGU), 0.01),
        "sgu_ln_b": nrm(ks[15], (DEPTH, D_SGU), 0.02),
        "sgu_w_s": nrm(ks[16], (DEPTH, SGU_HEADS, CHUNK, CHUNK), CHUNK ** -0.5),
        "sgu_b_s": 1.0 + nrm(ks[17], (DEPTH, SGU_HEADS, CHUNK), 0.01),
        "w_branch_hyena": nrm(ks[18], (DEPTH, D_HYENA, D_MODEL), D_HYENA ** -0.5),
        "w_branch_sgu": nrm(ks[19], (DEPTH, D_SGU, D_MODEL), D_SGU ** -0.5),
        "w_out": nrm(ks[20], (DEPTH, D_MODEL, D_MODEL), D_MODEL ** -0.5),
        "norm_ffn_g": 1.0 + nrm(ks[21], (DEPTH, D_MODEL), 0.01),
        "w_ffn_in": nrm(ks[22], (DEPTH, D_MODEL, 2 * D_FF), D_MODEL ** -0.5),
        "w_ffn_out": nrm(ks[23], (DEPTH, D_FF, D_MODEL), D_FF ** -0.5),
        "norm_final_g": 1.0 + nrm(ks[24], (D_MODEL,), 0.01),
    }


def reference(x, norm_mix_g, w_in, short_conv_w, short_conv_b, filt_w1, filt_b1, filt_w2, filt_b2,
              filt_w3, filt_b3, filt_freq, filt_w4, hyena_bias, sgu_ln_g, sgu_ln_b, sgu_w_s, sgu_b_s,
              w_branch_hyena, w_branch_sgu, w_out, norm_ffn_g, w_ffn_in, w_ffn_out, norm_final_g):
    L = x.shape[1]
    split_pts = [3 * D_HYENA, 3 * D_HYENA + 2 * D_SGU, 3 * D_HYENA + 2 * D_SGU + D_MODEL]
    for l in range(DEPTH):
        h = _rmsnorm(x, norm_mix_g[l])
        proj = jnp.einsum('bsd,de->bse', h, w_in[l])
        p_hy, p_sgu, g_hy, g_sgu = jnp.split(proj, split_pts, axis=-1)
        k_f = _hyena_filters(L, filt_w1[l], filt_b1[l], filt_w2[l], filt_b2[l],
                             filt_w3[l], filt_b3[l], filt_freq[l], filt_w4[l])
        y_hy = _hyena_mixer(p_hy, short_conv_w[l], short_conv_b[l], k_f, hyena_bias[l])
        y_sgu = _sgu_mixer(p_sgu, sgu_ln_g[l], sgu_ln_b[l], sgu_w_s[l], sgu_b_s[l])
        merged = (jax.nn.sigmoid(g_hy) * jnp.einsum('bsc,cd->bsd', y_hy, w_branch_hyena[l])
                  + jax.nn.sigmoid(g_sgu) * jnp.einsum('bsc,cd->bsd', y_sgu, w_branch_sgu[l]))
        x = x + jnp.einsum('bsd,de->bse', merged, w_out[l])
        h = _rmsnorm(x, norm_ffn_g[l])
        gate, up = jnp.split(jnp.einsum('bsd,df->bsf', h, w_ffn_in[l]), 2, axis=-1)
        x = x + jnp.einsum('bsf,fd->bsd', jax.nn.silu(gate) * up, w_ffn_out[l])
    return _rmsnorm(x, norm_final_g)
```

```python
import functools
import math

import jax
import jax.numpy as jnp
from jax import lax
from jax.experimental import pallas as pl
from jax.experimental.pallas import tpu as pltpu

F32 = jnp.float32
BF16 = jnp.bfloat16

EPS = 1e-6
FAST_DECAY = 0.3
SLOW_DECAY = 1.5
DECAY_TARGET = 1e-2
MOD_SHIFT = 0.0
N_DIR = 2
HYENA_ORDER = 2
LANES = 128
MIB = 1024 * 1024


def _params(vmem_mib, n_grid):
    return pltpu.CompilerParams(
        dimension_semantics=("arbitrary",) * n_grid,
        vmem_limit_bytes=vmem_mib * MIB)


def _resident(block_shape, index_map):
    return pl.BlockSpec(block_shape, index_map, pipeline_mode=pl.Buffered(1))


def _rms(x, g):
    ms = jnp.mean(x * x, axis=-1, keepdims=True)
    return x * lax.rsqrt(ms + EPS) * g


def _filter_feat_kernel(z_ref, w1_ref, b1_ref, w2_ref, b2_ref, w3_ref, b3_ref, fr_ref, o_ref):
    hp = lax.Precision.HIGHEST
    fr = fr_ref[...]
    h = jnp.sin(fr * (jnp.dot(z_ref[...], w1_ref[...], precision=hp, preferred_element_type=F32) + b1_ref[...]))
    h = jnp.sin(fr * (jnp.dot(h, w2_ref[...], precision=hp, preferred_element_type=F32) + b2_ref[...]))
    h = jnp.sin(fr * (jnp.dot(h, w3_ref[...], precision=hp, preferred_element_type=F32) + b3_ref[...]))
    o_ref[...] = h


def _filter_spec_kernel(h_ref, w4f_ref, w4b_ref, t_ref, ad_ref, sign_ref, c_ref, s_ref,
                        kr_ref, ki_ref, kn_ref):
    hp = lax.Precision.HIGHEST
    L = h_ref.shape[0]
    inv_n = 1.0 / (2 * L)
    decay = jnp.exp(-t_ref[...] * ad_ref[...]) + MOD_SHIFT
    hf = jnp.dot(h_ref[...], w4f_ref[...], precision=hp, preferred_element_type=F32) * decay
    hb = jnp.dot(h_ref[...], w4b_ref[...], precision=hp, preferred_element_type=F32) * decay
    row = lax.broadcasted_iota(jnp.int32, hf.shape, 0)
    hb = jnp.where(row == 0, 0.0, hb)
    norm = jnp.sum(jnp.abs(hf), axis=0, keepdims=True) + jnp.sum(jnp.abs(hb), axis=0, keepdims=True)
    inv = 1.0 / norm
    even = (hf + hb) * inv
    odd = (hf - hb) * inv
    wrow = jnp.where(row[:, :1] == 0, inv_n, 2.0 * inv_n)
    kr_ref[...] = jnp.dot(c_ref[...], even.astype(BF16), preferred_element_type=F32) * wrow
    ki_ref[...] = jnp.dot(s_ref[...], odd.astype(BF16), preferred_element_type=F32) * wrow
    kn_ref[...] = jnp.sum(even * sign_ref[...], axis=0, keepdims=True) * inv_n


def _hyena_filters(zfeat, t_col, absdelta, sign, cmat, smat, w1, b1, w2, b2, w3, b3, freq, w4, tc):
    L = zfeat.shape[0]
    c2 = w4.shape[1] // N_DIR
    n_ch = c2 // HYENA_ORDER

    def pad2(a, r, c):
        return jnp.pad(a.astype(F32), ((0, r - a.shape[0]), (0, c - a.shape[1])))

    feat = pl.pallas_call(
        _filter_feat_kernel,
        out_shape=jax.ShapeDtypeStruct((L, LANES), F32),
        name="filter_feat",
        compiler_params=_params(32, 0),
    )(pad2(zfeat, L, LANES), pad2(w1, LANES, LANES), pad2(b1[None], 1, LANES),
      pad2(w2, LANES, LANES), pad2(b2[None], 1, LANES),
      pad2(w3, LANES, LANES), pad2(b3[None], 1, LANES), pad2(freq[None], 1, LANES))

    w4p = pad2(w4, LANES, w4.shape[1])
    nt = c2 // tc
    nct = n_ch // tc
    full = lambda j: (0, 0)
    kr, ki, kn = pl.pallas_call(
        _filter_spec_kernel,
        grid=(nt,),
        in_specs=[
            _resident((L, LANES), full),
            pl.BlockSpec((LANES, tc), lambda j: (0, j)),
            pl.BlockSpec((LANES, tc), lambda j: (0, nt + j)),
            _resident((L, 1), full),
            pl.BlockSpec((1, tc), lambda j: (0, j % nct)),
            _resident((L, 1), full),
            _resident((L, L), full),
            _resident((L, L), full),
        ],
        out_specs=[
            pl.BlockSpec((L, tc), lambda j: (0, j)),
            pl.BlockSpec((L, tc), lambda j: (0, j)),
            pl.BlockSpec((1, tc), lambda j: (0, j)),
        ],
        out_shape=[jax.ShapeDtypeStruct((L, c2), F32),
                   jax.ShapeDtypeStruct((L, c2), F32),
                   jax.ShapeDtypeStruct((1, c2), F32)],
        name="filter_spec",
        compiler_params=_params(48, 1),
    )(feat, w4p, w4p, t_col, absdelta, sign, cmat, smat)
    return kr, ki, kn


def _proj_kernel(x_ref, g_ref, w_ref, o1_ref, o2_ref, h_ref, *, n_plain):
    j = pl.program_id(1)

    @pl.when(j == 0)
    def _():
        h_ref[...] = _rms(x_ref[...], g_ref[...]).astype(BF16)

    acc = jnp.dot(h_ref[...], w_ref[...], preferred_element_type=F32)

    @pl.when(j < n_plain)
    def _():
        o1_ref[...] = acc.astype(BF16)

    @pl.when(j >= n_plain)
    def _():
        o2_ref[...] = jax.nn.sigmoid(acc).astype(BF16)


def _proj(x2, g, w, n_mix, tm, tn):
    M, D = x2.shape
    E = w.shape[1]
    n_plain = n_mix // tn
    nj = E // tn
    return pl.pallas_call(
        functools.partial(_proj_kernel, n_plain=n_plain),
        grid=(M // tm, nj),
        in_specs=[
            pl.BlockSpec((tm, D), lambda i, j: (i, 0)),
            _resident((1, D), lambda i, j: (0, 0)),
            pl.BlockSpec((D, tn), lambda i, j: (0, j)),
        ],
        out_specs=[
            pl.BlockSpec((tm, tn), lambda i, j: (i, jnp.minimum(j, n_plain - 1))),
            pl.BlockSpec((tm, tn), lambda i, j: (i, jnp.maximum(j - n_plain, 0))),
        ],
        out_shape=[jax.ShapeDtypeStruct((M, n_mix), BF16),
                   jax.ShapeDtypeStruct((M, E - n_mix), BF16)],
        scratch_shapes=[pltpu.VMEM((tm, D), BF16)],
        name="proj",
        compiler_params=_params(48, 2),
    )(x2, g, w)


BF16_ROWS = 16


def _short_conv_rows(p_ref, w_ref, b_ref, r0, rc):
    L = p_ref.shape[1]
    p = p_ref[0, r0:r0 + rc, :].astype(F32)
    row = lax.broadcasted_iota(jnp.int32, p.shape, 0)
    if r0 == 0:
        before = 0.0
    else:
        before = p_ref[0, r0 - BF16_ROWS:r0, :].astype(F32)[BF16_ROWS - 1:BF16_ROWS, :]
    if r0 + rc == L:
        after = 0.0
    else:
        after = p_ref[0, r0 + rc:r0 + rc + BF16_ROWS, :].astype(F32)[0:1, :]
    prev = jnp.where(row == 0, before, pltpu.roll(p, 1, axis=0))
    nxt = jnp.where(row == rc - 1, after, pltpu.roll(p, rc - 1, axis=0))
    return prev * w_ref[0:1, :] + p * w_ref[1:2, :] + nxt * w_ref[2:3, :] + b_ref[...]


def _hyena_kernel(v_ref, x1_ref, x2_ref, wv_ref, w1_ref, w2_ref, bv_ref, b1_ref, b2_ref,
                  kr0_ref, ki0_ref, kn0_ref, kr1_ref, ki1_ref, kn1_ref, hb0_ref, hb1_ref,
                  sign_ref, c_ref, s_ref, o_ref, zf_ref, zb_ref, p_ref, *, rc):
    L = c_ref.shape[0]
    chunks = [(r0, slice(r0, r0 + rc)) for r0 in range(0, L, rc)]

    def put_z(z, rows):
        zf_ref[rows, :] = z
        zb_ref[rows, :] = z.astype(BF16)
        return jnp.sum(z * sign_ref[rows, :], axis=0, keepdims=True)

    def spectrum_product(kr_ref, ki_ref):
        zb = zb_ref[...]
        for _, rows in chunks:
            xr = jnp.dot(c_ref[rows, :], zb, preferred_element_type=F32)
            xi = jnp.dot(s_ref[rows, :], zb, preferred_element_type=F32)
            kr = kr_ref[rows, :]
            ki = ki_ref[rows, :]
            p_ref[rows, :] = (xr * kr - xi * ki).astype(BF16)
            p_ref[L + rows.start:L + rows.stop, :] = (xr * ki + xi * kr).astype(BF16)

    def conv_rows(rows, xn, kn_ref, hb_ref):
        y = (jnp.dot(c_ref[rows, :], p_ref[0:L, :], preferred_element_type=F32)
             + jnp.dot(s_ref[rows, :], p_ref[L:2 * L, :], preferred_element_type=F32))
        return y + sign_ref[rows, :] * (xn * kn_ref[...]) + zf_ref[rows, :] * hb_ref[...]

    xn = 0.0
    for r0, rows in chunks:
        xn = xn + put_z(_short_conv_rows(v_ref, wv_ref, bv_ref, r0, rc), rows)
    spectrum_product(kr0_ref, ki0_ref)
    xn1 = 0.0
    for r0, rows in chunks:
        y = conv_rows(rows, xn, kn0_ref, hb0_ref)
        xn1 = xn1 + put_z(_short_conv_rows(x1_ref, w1_ref, b1_ref, r0, rc) * y, rows)
    spectrum_product(kr1_ref, ki1_ref)
    for r0, rows in chunks:
        y = conv_rows(rows, xn1, kn1_ref, hb1_ref)
        o_ref[0, rows, :] = (_short_conv_rows(x2_ref, w2_ref, b2_ref, r0, rc) * y).astype(o_ref.dtype)


def _hyena(p3, conv_w, conv_b, kr, ki, kn, hbias, sign, cmat, smat, n_ch, ct):
    B, L, _ = p3.shape
    nct = n_ch // ct
    hb = hbias.reshape(1, HYENA_ORDER * n_ch).astype(F32)
    cb = conv_b.reshape(1, -1).astype(F32)
    cw = conv_w.astype(F32)

    def pspec(g):
        return pl.BlockSpec((1, L, ct), lambda j, b, g=g: (b, 0, g * nct + j))

    def rowspec(rows, g):
        return pl.BlockSpec((rows, ct), lambda j, b, g=g: (0, g * nct + j))

    def specspec(g):
        return pl.BlockSpec((L, ct), lambda j, b, g=g: (0, g * nct + j), pipeline_mode=pl.Buffered(1))

    full = lambda j, b: (0, 0)
    return pl.pallas_call(
        functools.partial(_hyena_kernel, rc=_pick(L, 512)),
        grid=(nct, B),
        in_specs=[
            pspec(0), pspec(1), pspec(2),
            rowspec(3, 0), rowspec(3, 1), rowspec(3, 2),
            rowspec(1, 0), rowspec(1, 1), rowspec(1, 2),
            specspec(0), specspec(0), rowspec(1, 0),
            specspec(1), specspec(1), rowspec(1, 1),
            rowspec(1, 0), rowspec(1, 1),
            _resident((L, 1), full),
            _resident((L, L), full),
            _resident((L, L), full),
        ],
        out_specs=pl.BlockSpec((1, L, ct), lambda j, b: (b, 0, j)),
        out_shape=jax.ShapeDtypeStruct((B, L, n_ch), BF16),
        scratch_shapes=[pltpu.VMEM((L, ct), F32), pltpu.VMEM((L, ct), BF16),
                        pltpu.VMEM((2 * L, ct), BF16)],
        name="hyena",
        compiler_params=_params(56, 2),
    )(p3, p3, p3, cw, cw, cw, cb, cb, cb, kr, ki, kn, kr, ki, kn, hb, hb, sign, cmat, smat)


def _gelu(x):
    return 0.5 * x * (1.0 + lax.erf(x * math.sqrt(0.5)))


def _sgu_kernel(u_ref, v_ref, g_ref, b_ref, ws_ref, bs_ref, o_ref, *, heads, chunk):
    u = _gelu(u_ref[...].astype(F32))
    v = _gelu(v_ref[...].astype(F32))
    mu = jnp.mean(v, axis=-1, keepdims=True)
    vc = v - mu
    var = jnp.mean(vc * vc, axis=-1, keepdims=True)
    vn = (vc * lax.rsqrt(var + EPS) * g_ref[...] + b_ref[...]).astype(BF16)
    tm, ds = vn.shape
    hd = ds // heads
    for n in range(tm // chunk):
        rows = slice(n * chunk, (n + 1) * chunk)
        for h in range(heads):
            cols = slice(h * hd, (h + 1) * hd)
            s = jnp.dot(ws_ref[h], vn[rows, cols], preferred_element_type=F32) + bs_ref[:, h:h + 1]
            o_ref[rows, cols] = (u[rows, cols] * s).astype(o_ref.dtype)


def _sgu(p2, ln_g, ln_b, w_s, b_s, col0, ds, tm):
    M = p2.shape[0]
    heads, chunk, _ = w_s.shape
    cb = col0 // ds
    full2 = lambda i: (0, 0)
    return pl.pallas_call(
        functools.partial(_sgu_kernel, heads=heads, chunk=chunk),
        grid=(M // tm,),
        in_specs=[
            pl.BlockSpec((tm, ds), lambda i: (i, cb)),
            pl.BlockSpec((tm, ds), lambda i: (i, cb + 1)),
            _resident((1, ds), full2),
            _resident((1, ds), full2),
            _resident((heads, chunk, chunk), lambda i: (0, 0, 0)),
            _resident((chunk, heads), full2),
        ],
        out_specs=pl.BlockSpec((tm, ds), lambda i: (i, 0)),
        out_shape=jax.ShapeDtypeStruct((M, ds), BF16),
        name="sgu",
        compiler_params=_params(48, 1),
    )(p2, p2, ln_g.reshape(1, ds).astype(F32), ln_b.reshape(1, ds).astype(F32),
      w_s.astype(BF16), b_s.T.astype(F32))


def _merge_kernel(x_ref, yh_ref, ys_ref, gh_ref, gs_ref, wbh_ref, wbs_ref, wo_ref, o_ref):
    a = (jnp.dot(yh_ref[...], wbh_ref[...], preferred_element_type=F32) * gh_ref[...].astype(F32)
         + jnp.dot(ys_ref[...], wbs_ref[...], preferred_element_type=F32) * gs_ref[...].astype(F32))
    o_ref[...] = x_ref[...] + jnp.dot(a.astype(BF16), wo_ref[...], preferred_element_type=F32)


def _merge(x2, yh, ys, gates, wbh, wbs, wo, tm):
    M, D = x2.shape
    full = lambda i: (0, 0)
    return pl.pallas_call(
        _merge_kernel,
        grid=(M // tm,),
        in_specs=[
            pl.BlockSpec((tm, D), lambda i: (i, 0)),
            pl.BlockSpec((tm, yh.shape[1]), lambda i: (i, 0)),
            pl.BlockSpec((tm, ys.shape[1]), lambda i: (i, 0)),
            pl.BlockSpec((tm, D), lambda i: (i, 0)),
            pl.BlockSpec((tm, D), lambda i: (i, 1)),
            _resident(wbh.shape, full),
            _resident(wbs.shape, full),
            _resident(wo.shape, full),
        ],
        out_specs=pl.BlockSpec((tm, D), lambda i: (i, 0)),
        out_shape=jax.ShapeDtypeStruct((M, D), F32),
        name="merge",
        compiler_params=_params(56, 1),
    )(x2, yh, ys, gates, gates, wbh, wbs, wo)


def _ffn_kernel(x_ref, g_ref, wg_ref, wu_ref, wo_ref, gf_ref, o_ref, h_ref, *, final_norm):
    f = pl.program_id(1)

    @pl.when(f == 0)
    def _():
        x = x_ref[...]
        h_ref[...] = _rms(x, g_ref[...]).astype(BF16)
        o_ref[...] = x

    h = h_ref[...]
    gate = jnp.dot(h, wg_ref[...], preferred_element_type=F32)
    up = jnp.dot(h, wu_ref[...], preferred_element_type=F32)
    a = (gate * jax.nn.sigmoid(gate) * up).astype(BF16)
    o_ref[...] += jnp.dot(a, wo_ref[...], preferred_element_type=F32)

    if final_norm:
        @pl.when(f == pl.num_programs(1) - 1)
        def _():
            o_ref[...] = _rms(o_ref[...], gf_ref[...])


def _ffn(x2, g, w_in, w_out, g_final, final_norm, tm, tf):
    M, D = x2.shape
    FF = w_out.shape[0]
    nf = FF // tf
    full = lambda i, f: (0, 0)
    return pl.pallas_call(
        functools.partial(_ffn_kernel, final_norm=final_norm),
        grid=(M // tm, nf),
        in_specs=[
            pl.BlockSpec((tm, D), lambda i, f: (i, 0)),
            _resident((1, D), full),
            pl.BlockSpec((D, tf), lambda i, f: (0, f)),
            pl.BlockSpec((D, tf), lambda i, f: (0, nf + f)),
            pl.BlockSpec((tf, D), lambda i, f: (f, 0)),
            _resident((1, D), full),
        ],
        out_specs=pl.BlockSpec((tm, D), lambda i, f: (i, 0)),
        out_shape=jax.ShapeDtypeStruct((M, D), F32),
        scratch_shapes=[pltpu.VMEM((tm, D), BF16)],
        name="ffn",
        compiler_params=_params(56, 2),
    )(x2, g, w_in, w_in, w_out, g_final)


def _transform_tables(L):
    k = jnp.arange(L, dtype=jnp.int32)
    idx = (k[:, None] * k[None, :]) % (2 * L)
    ang = idx.astype(F32) * (math.pi / L)
    sign = (1 - 2 * (k % 2)).astype(F32)[:, None]
    return jnp.cos(ang).astype(BF16), jnp.sin(ang).astype(BF16), sign


def _position_features(L, emb):
    t = jnp.linspace(0.0, 1.0, L, dtype=F32)[:, None]
    bands = (emb - 1) // 2
    w = 2.0 * math.pi * jnp.arange(L, dtype=F32)[:, None] / L
    f = jnp.linspace(1e-4, bands - 1, bands, dtype=F32)[None, :]
    return jnp.concatenate([t, jnp.cos(f * w), -jnp.sin(f * w)], axis=-1), t


def _pick(n, pref):
    t = min(n, pref)
    while n % t:
        t -= 1
    return t


def kernel(x, norm_mix_g, w_in, short_conv_w, short_conv_b, filt_w1, filt_b1, filt_w2, filt_b2, filt_w3, filt_b3, filt_freq, filt_w4, hyena_bias, sgu_ln_g, sgu_ln_b, sgu_w_s, sgu_b_s, w_branch_hyena, w_branch_sgu, w_out, norm_ffn_g, w_ffn_in, w_ffn_out, norm_final_g):
    B, L, D = x.shape
    depth = w_in.shape[0]
    n_ch = w_branch_hyena.shape[1]
    ds = w_branch_sgu.shape[1]
    assert n_ch == ds and 2 * n_ch == D
    n_mix = 3 * n_ch + 2 * ds
    M = B * L

    cmat, smat, sign = _transform_tables(L)
    zfeat, t_col = _position_features(L, filt_w1.shape[1])
    max_decay = math.log(DECAY_TARGET) / FAST_DECAY
    min_decay = math.log(DECAY_TARGET) / SLOW_DECAY
    absdelta = jnp.abs(jnp.linspace(min_decay, max_decay, n_ch, dtype=F32))[None, :]

    ct = _pick(n_ch, 256)
    tm = _pick(M, 1024)
    tn = _pick(n_ch, 1024)

    x2 = x.reshape(M, D)
    for l in range(depth):
        kr, ki, kn = _hyena_filters(zfeat, t_col, absdelta, sign, cmat, smat,
                                    filt_w1[l], filt_b1[l], filt_w2[l], filt_b2[l],
                                    filt_w3[l], filt_b3[l], filt_freq[l], filt_w4[l], ct)
        mix, gates = _proj(x2, norm_mix_g[l].reshape(1, D), w_in[l].astype(BF16), n_mix, tm, tn)
        y_hy = _hyena(mix.reshape(B, L, n_mix), short_conv_w[l], short_conv_b[l], kr, ki, kn,
                      hyena_bias[l], sign, cmat, smat, n_ch, ct)
        y_sgu = _sgu(mix, sgu_ln_g[l], sgu_ln_b[l], sgu_w_s[l], sgu_b_s[l], 3 * n_ch, ds, tm)
        x2 = _merge(x2, y_hy.reshape(M, n_ch), y_sgu, gates,
                    w_branch_hyena[l].astype(BF16), w_branch_sgu[l].astype(BF16),
                    w_out[l].astype(BF16), _pick(M, 512))
        last = l == depth - 1
        x2 = _ffn(x2, norm_ffn_g[l].reshape(1, D), w_ffn_in[l].astype(BF16),
                  w_ffn_out[l].astype(BF16), norm_final_g.reshape(1, D), last,
                  _pick(M, 512), _pick(w_ffn_out.shape[1], 512))
    if depth == 0:
        raise NotImplementedError("depth 0")
    return x2.reshape(B, L, D)
```

```python
import functools
import math

import numpy as np
import jax
import jax.numpy as jnp
from jax import lax
from jax.experimental import pallas as pl
from jax.experimental.pallas import tpu as pltpu

F32 = jnp.float32
BF16 = jnp.bfloat16

EPS = 1e-6
FAST_DECAY = 0.3
SLOW_DECAY = 1.5
DECAY_TARGET = 1e-2
MOD_SHIFT = 0.0
N_DIR = 2
HYENA_ORDER = 2
LANES = 128
BF16_ROWS = 16
MIB = 1024 * 1024


def _params(vmem_mib, n_grid):
    return pltpu.CompilerParams(
        dimension_semantics=("arbitrary",) * n_grid,
        vmem_limit_bytes=vmem_mib * MIB)


def _resident(block_shape, index_map):
    return pl.BlockSpec(block_shape, index_map, pipeline_mode=pl.Buffered(1))


def _rms(x, g):
    ms = jnp.mean(x * x, axis=-1, keepdims=True)
    return x * lax.rsqrt(ms + EPS) * g


def _dot(a, b):
    return jnp.dot(a, b, preferred_element_type=F32)


def _filter_feat_kernel(z_ref, w1_ref, b1_ref, w2_ref, b2_ref, w3_ref, b3_ref, fr_ref, o_ref):
    hp = lax.Precision.HIGHEST
    fr = fr_ref[...]
    h = jnp.sin(fr * (jnp.dot(z_ref[...], w1_ref[...], precision=hp, preferred_element_type=F32) + b1_ref[...]))
    h = jnp.sin(fr * (jnp.dot(h, w2_ref[...], precision=hp, preferred_element_type=F32) + b2_ref[...]))
    h = jnp.sin(fr * (jnp.dot(h, w3_ref[...], precision=hp, preferred_element_type=F32) + b3_ref[...]))
    o_ref[...] = h


def _filter_spec_kernel(f_ref, w4f_ref, w4b_ref, te_ref, to_ref, ad_ref, alt_ref,
                        ce_ref, se_ref, co_ref, so_ref,
                        kra_ref, kia_ref, krb_ref, kib_ref, kh_ref):
    hp = lax.Precision.HIGHEST
    H = f_ref.shape[0]
    inv_n = 1.0 / (4 * H)
    ad = ad_ref[...]

    def taps(cols, t_ref):
        f = f_ref[:, cols]
        decay = jnp.exp(-t_ref[...] * ad) + MOD_SHIFT
        return (jnp.dot(f, w4f_ref[...], precision=hp, preferred_element_type=F32) * decay,
                jnp.dot(f, w4b_ref[...], precision=hp, preferred_element_type=F32) * decay)

    hf_e, hb_e = taps(slice(0, LANES), te_ref)
    hf_o, hb_o = taps(slice(LANES, 2 * LANES), to_ref)
    row = lax.broadcasted_iota(jnp.int32, hf_e.shape, 0)
    hb_e = jnp.where(row == 0, 0.0, hb_e)

    def colsum(a):
        return jnp.sum(a, axis=0, keepdims=True)

    norm = colsum(jnp.abs(hf_e)) + colsum(jnp.abs(hf_o)) + colsum(jnp.abs(hb_e)) + colsum(jnp.abs(hb_o))
    inv = 1.0 / norm
    ev_e, ev_o = (hf_e + hb_e) * inv, (hf_o + hb_o) * inv
    od_e, od_o = (hf_e - hb_e) * inv, (hf_o - hb_o) * inv
    ur = _dot(ce_ref[...], ev_e.astype(BF16))
    vr = _dot(co_ref[...], ev_o.astype(BF16))
    ui = _dot(se_ref[...], od_e.astype(BF16))
    vi = _dot(so_ref[...], od_o.astype(BF16))
    w = jnp.where(row[:, :1] == 0, inv_n, 2.0 * inv_n)
    kra_ref[...] = (ur + vr) * w
    krb_ref[...] = (ur - vr) * w
    kia_ref[...] = (ui + vi) * w
    kib_ref[...] = (vi - ui) * w
    alt = alt_ref[...]
    kh_ref[0:1, :] = colsum(ev_e * alt) * (2.0 * inv_n)
    kh_ref[1:2, :] = colsum(od_o * alt) * (2.0 * inv_n)


def _hyena_filters(zfeat, t_col, absdelta, alt, tabs, w1, b1, w2, b2, w3, b3, freq, w4, tc):
    L = zfeat.shape[0]
    H = L // 2
    c2 = w4.shape[1] // N_DIR
    n_ch = c2 // HYENA_ORDER
    ce, se, co, so = tabs[:4]

    def pad2(a, r, c):
        return jnp.pad(a.astype(F32), ((0, r - a.shape[0]), (0, c - a.shape[1])))

    feat = pl.pallas_call(
        _filter_feat_kernel,
        out_shape=jax.ShapeDtypeStruct((L, LANES), F32),
        name="filter_feat",
        compiler_params=_params(32, 0),
    )(pad2(zfeat, L, LANES), pad2(w1, LANES, LANES), pad2(b1[None], 1, LANES),
      pad2(w2, LANES, LANES), pad2(b2[None], 1, LANES),
      pad2(w3, LANES, LANES), pad2(b3[None], 1, LANES), pad2(freq[None], 1, LANES))

    w4p = pad2(w4, LANES, w4.shape[1])
    t_eo = t_col.reshape(H, 2)
    nt = c2 // tc
    nct = n_ch // tc
    full = lambda j: (0, 0)
    spec = jax.ShapeDtypeStruct((H, c2), F32)
    tile = pl.BlockSpec((H, tc), lambda j: (0, j))
    return pl.pallas_call(
        _filter_spec_kernel,
        grid=(nt,),
        in_specs=[
            _resident((H, 2 * LANES), full),
            pl.BlockSpec((LANES, tc), lambda j: (0, j)),
            pl.BlockSpec((LANES, tc), lambda j: (0, nt + j)),
            _resident((H, 1), full),
            _resident((H, 1), full),
            pl.BlockSpec((1, tc), lambda j: (0, j % nct)),
            _resident((H, 1), full),
            _resident((H, H), full), _resident((H, H), full),
            _resident((H, H), full), _resident((H, H), full),
        ],
        out_specs=[tile, tile, tile, tile, pl.BlockSpec((2, tc), lambda j: (0, j))],
        out_shape=[spec, spec, spec, spec, jax.ShapeDtypeStruct((2, c2), F32)],
        name="filter_spec",
        compiler_params=_params(48, 1),
    )(feat.reshape(H, 2 * LANES), w4p, w4p, t_eo[:, 0:1], t_eo[:, 1:2], absdelta, alt, ce, se, co, so)


def _proj_kernel(x_ref, g_ref, w_ref, o1_ref, o2_ref, h_ref, *, n_plain):
    j = pl.program_id(1)

    @pl.when(j == 0)
    def _():
        h_ref[...] = _rms(x_ref[...], g_ref[...]).astype(BF16)

    acc = _dot(h_ref[...], w_ref[...])

    @pl.when(j < n_plain)
    def _():
        o1_ref[...] = acc.astype(BF16)

    @pl.when(j >= n_plain)
    def _():
        o2_ref[...] = jax.nn.sigmoid(acc).astype(BF16)


def _proj(x2, g, w, n_mix, tm, tn):
    M, D = x2.shape
    E = w.shape[1]
    n_plain = n_mix // tn
    nj = E // tn
    return pl.pallas_call(
        functools.partial(_proj_kernel, n_plain=n_plain),
        grid=(M // tm, nj),
        in_specs=[
            pl.BlockSpec((tm, D), lambda i, j: (i, 0)),
            _resident((1, D), lambda i, j: (0, 0)),
            pl.BlockSpec((D, tn), lambda i, j: (0, j)),
        ],
        out_specs=[
            pl.BlockSpec((tm, tn), lambda i, j: (i, jnp.minimum(j, n_plain - 1))),
            pl.BlockSpec((tm, tn), lambda i, j: (i, jnp.maximum(j - n_plain, 0))),
        ],
        out_shape=[jax.ShapeDtypeStruct((M, n_mix), BF16),
                   jax.ShapeDtypeStruct((M, E - n_mix), BF16)],
        scratch_shapes=[pltpu.VMEM((tm, D), BF16)],
        name="proj",
        compiler_params=_params(48, 2),
    )(x2, g, w)


def _short_conv_rows(pe_ref, po_ref, w_ref, b_ref, r0, rc):
    H = pe_ref.shape[1]
    pe = pe_ref[0, r0:r0 + rc, :].astype(F32)
    po = po_ref[0, r0:r0 + rc, :].astype(F32)
    row = lax.broadcasted_iota(jnp.int32, pe.shape, 0)
    if r0 == 0:
        before = 0.0
    else:
        before = po_ref[0, r0 - BF16_ROWS:r0, :].astype(F32)[BF16_ROWS - 1:BF16_ROWS, :]
    if r0 + rc == H:
        after = 0.0
    else:
        after = pe_ref[0, r0 + rc:r0 + rc + BF16_ROWS, :].astype(F32)[0:1, :]
    po_prev = jnp.where(row == 0, before, pltpu.roll(po, 1, axis=0))
    pe_next = jnp.where(row == rc - 1, after, pltpu.roll(pe, rc - 1, axis=0))
    w0, w1, w2, b = w_ref[0:1, :], w_ref[1:2, :], w_ref[2:3, :], b_ref[...]
    return (po_prev * w0 + pe * w1 + po * w2 + b,
            pe * w0 + po * w1 + pe_next * w2 + b)


def _hyena_kernel(ve_ref, vo_ref, x1e_ref, x1o_ref, x2e_ref, x2o_ref,
                  wv_ref, w1_ref, w2_ref, bv_ref, b1_ref, b2_ref,
                  kra0_ref, kia0_ref, krb0_ref, kib0_ref, kh0_ref,
                  kra1_ref, kia1_ref, krb1_ref, kib1_ref, kh1_ref, hb0_ref, hb1_ref,
                  alt_ref, ce_ref, se_ref, co_ref, so_ref, cot_ref, sot_ref,
                  oe_ref, oo_ref, zf_ref, zb_ref, p_ref, *, rc):
    H = ce_ref.shape[0]
    chunks = [(r0, slice(r0, r0 + rc)) for r0 in range(0, H, rc)]

    def put_z(ze, zo, rows):
        zf_ref[0, rows, :] = ze
        zf_ref[1, rows, :] = zo
        zb_ref[0, rows, :] = ze.astype(BF16)
        zb_ref[1, rows, :] = zo.astype(BF16)
        alt = alt_ref[rows, :]
        return (jnp.sum(ze * alt, axis=0, keepdims=True), jnp.sum(zo * alt, axis=0, keepdims=True))

    def spectrum_product(kra_ref, kia_ref, krb_ref, kib_ref):
        zbe = zb_ref[0]
        zbo = zb_ref[1]
        for _, rows in chunks:
            ur = _dot(ce_ref[rows, :], zbe)
            ui = _dot(se_ref[rows, :], zbe)
            vr = _dot(co_ref[rows, :], zbo)
            vi = _dot(so_ref[rows, :], zbo)
            xra, xia, xrb, xib = ur + vr, ui + vi, ur - vr, vi - ui
            kra, kia, krb, kib = kra_ref[rows, :], kia_ref[rows, :], krb_ref[rows, :], kib_ref[rows, :]
            pra = xra * kra - xia * kia
            pia = xra * kia + xia * kra
            prb = xrb * krb - xib * kib
            pib = xrb * kib + xib * krb
            p_ref[0, rows, :] = (pra + prb).astype(BF16)
            p_ref[1, rows, :] = (pia - pib).astype(BF16)
            p_ref[2, rows, :] = (pra - prb).astype(BF16)
            p_ref[3, rows, :] = (pia + pib).astype(BF16)

    def conv_rows(rows, xh, kh_ref, hb_ref):
        xrh, xih = xh
        krh, kih = kh_ref[0:1, :], kh_ref[1:2, :]
        prh = xrh * krh - xih * kih
        pih = xrh * kih + xih * krh
        alt = alt_ref[rows, :]
        bias = hb_ref[...]
        ye = (_dot(ce_ref[rows, :], p_ref[0]) + _dot(se_ref[rows, :], p_ref[1])
              + alt * prh + zf_ref[0, rows, :] * bias)
        yo = (_dot(cot_ref[rows, :], p_ref[2]) + _dot(sot_ref[rows, :], p_ref[3])
              + alt * pih + zf_ref[1, rows, :] * bias)
        return ye, yo

    def add2(a, b):
        return (a[0] + b[0], a[1] + b[1])

    xh = (0.0, 0.0)
    for r0, rows in chunks:
        ze, zo = _short_conv_rows(ve_ref, vo_ref, wv_ref, bv_ref, r0, rc)
        xh = add2(xh, put_z(ze, zo, rows))
    spectrum_product(kra0_ref, kia0_ref, krb0_ref, kib0_ref)
    xh1 = (0.0, 0.0)
    for r0, rows in chunks:
        ye, yo = conv_rows(rows, xh, kh0_ref, hb0_ref)
        ge, go = _short_conv_rows(x1e_ref, x1o_ref, w1_ref, b1_ref, r0, rc)
        xh1 = add2(xh1, put_z(ge * ye, go * yo, rows))
    spectrum_product(kra1_ref, kia1_ref, krb1_ref, kib1_ref)
    for r0, rows in chunks:
        ye, yo = conv_rows(rows, xh1, kh1_ref, hb1_ref)
        ge, go = _short_conv_rows(x2e_ref, x2o_ref, w2_ref, b2_ref, r0, rc)
        oe_ref[0, rows, :] = (ge * ye).astype(oe_ref.dtype)
        oo_ref[0, rows, :] = (go * yo).astype(oo_ref.dtype)


def _hyena(p3, conv_w, conv_b, spectra, hbias, alt, tabs, n_ch, ct):
    B, H, w2 = p3.shape
    nct = n_ch // ct
    nw = (w2 // 2) // ct
    kra, kia, krb, kib, kh = spectra
    hb = hbias.reshape(1, HYENA_ORDER * n_ch).astype(F32)
    cb = conv_b.reshape(1, -1).astype(F32)
    cw = conv_w.astype(F32)

    def pspec(g, odd):
        return pl.BlockSpec((1, H, ct), lambda j, b, g=g, odd=odd: (b, 0, odd * nw + g * nct + j))

    def rowspec(rows, g):
        return pl.BlockSpec((rows, ct), lambda j, b, g=g: (0, g * nct + j))

    def specspec(g):
        return _resident((H, ct), lambda j, b, g=g: (0, g * nct + j))

    full = lambda j, b: (0, 0)
    table = _resident((H, H), full)
    out = pl.BlockSpec((1, H, ct), lambda j, b: (b, 0, j))
    return pl.pallas_call(
        functools.partial(_hyena_kernel, rc=_pick(H, 512)),
        grid=(nct, B),
        in_specs=[
            pspec(0, 0), pspec(0, 1), pspec(1, 0), pspec(1, 1), pspec(2, 0), pspec(2, 1),
            rowspec(3, 0), rowspec(3, 1), rowspec(3, 2),
            rowspec(1, 0), rowspec(1, 1), rowspec(1, 2),
            specspec(0), specspec(0), specspec(0), specspec(0), rowspec(2, 0),
            specspec(1), specspec(1), specspec(1), specspec(1), rowspec(2, 1),
            rowspec(1, 0), rowspec(1, 1),
            _resident((H, 1), full),
            table, table, table, table, table, table,
        ],
        out_specs=[out, out],
        out_shape=[jax.ShapeDtypeStruct((B, H, n_ch), BF16)] * 2,
        scratch_shapes=[pltpu.VMEM((2, H, ct), F32), pltpu.VMEM((2, H, ct), BF16),
                        pltpu.VMEM((4, H, ct), BF16)],
        name="hyena",
        compiler_params=_params(56, 2),
    )(p3, p3, p3, p3, p3, p3, cw, cw, cw, cb, cb, cb,
      kra, kia, krb, kib, kh, kra, kia, krb, kib, kh, hb, hb, alt, *tabs)


def _gelu(x):
    return 0.5 * x * (1.0 + lax.erf(x * math.sqrt(0.5)))


def _sgu_kernel(u_ref, v_ref, g_ref, b_ref, ws_ref, bs_ref, o_ref, *, heads, chunk):
    u = _gelu(u_ref[...].astype(F32))
    v = _gelu(v_ref[...].astype(F32))
    mu = jnp.mean(v, axis=-1, keepdims=True)
    vc = v - mu
    var = jnp.mean(vc * vc, axis=-1, keepdims=True)
    vn = (vc * lax.rsqrt(var + EPS) * g_ref[...] + b_ref[...]).astype(BF16)
    tm, ds = vn.shape
    hd = ds // heads
    for n in range(tm // chunk):
        rows = slice(n * chunk, (n + 1) * chunk)
        for h in range(heads):
            cols = slice(h * hd, (h + 1) * hd)
            s = _dot(ws_ref[h], vn[rows, cols]) + bs_ref[:, h:h + 1]
            o_ref[rows, cols] = (u[rows, cols] * s).astype(o_ref.dtype)


def _sgu(p2, ln_g, ln_b, w_s, b_s, col0, ds, tm):
    M = p2.shape[0]
    heads, chunk, _ = w_s.shape
    cb = col0 // ds
    full2 = lambda i: (0, 0)
    return pl.pallas_call(
        functools.partial(_sgu_kernel, heads=heads, chunk=chunk),
        grid=(M // tm,),
        in_specs=[
            pl.BlockSpec((tm, ds), lambda i: (i, cb)),
            pl.BlockSpec((tm, ds), lambda i: (i, cb + 1)),
            _resident((1, ds), full2),
            _resident((1, ds), full2),
            _resident((heads, chunk, chunk), lambda i: (0, 0, 0)),
            _resident((chunk, heads), full2),
        ],
        out_specs=pl.BlockSpec((tm, ds), lambda i: (i, 0)),
        out_shape=jax.ShapeDtypeStruct((M, ds), BF16),
        name="sgu",
        compiler_params=_params(48, 1),
    )(p2, p2, ln_g.reshape(1, ds).astype(F32), ln_b.reshape(1, ds).astype(F32),
      w_s.astype(BF16), b_s.T.astype(F32))


def _merge_kernel(xe_ref, xo_ref, yhe_ref, yho_ref, yse_ref, yso_ref,
                  ghe_ref, gse_ref, gho_ref, gso_ref, wbh_ref, wbs_ref, wo_ref, o_ref):
    D = xe_ref.shape[1]

    def tokens(x_ref, yh_ref, ys_ref, gh_ref, gs_ref):
        a = (_dot(yh_ref[...], wbh_ref[...]) * gh_ref[...].astype(F32)
             + _dot(ys_ref[...], wbs_ref[...]) * gs_ref[...].astype(F32))
        return x_ref[...] + _dot(a.astype(BF16), wo_ref[...])

    o_ref[:, 0:D] = tokens(xe_ref, yhe_ref, yse_ref, ghe_ref, gse_ref)
    o_ref[:, D:2 * D] = tokens(xo_ref, yho_ref, yso_ref, gho_ref, gso_ref)


def _merge(xp, yhe, yho, ysp, gp, wbh, wbs, wo, tp):
    M2, D2 = xp.shape
    D = D2 // 2
    n_ch = yhe.shape[1]
    ds = ysp.shape[1] // 2
    full = lambda i: (0, 0)

    def cols(width, c):
        return pl.BlockSpec((tp, width), lambda i, c=c: (i, c))

    return pl.pallas_call(
        _merge_kernel,
        grid=(M2 // tp,),
        in_specs=[
            cols(D, 0), cols(D, 1),
            cols(n_ch, 0), cols(n_ch, 0),
            cols(ds, 0), cols(ds, 1),
            cols(D, 0), cols(D, 1), cols(D, 2), cols(D, 3),
            _resident(wbh.shape, full),
            _resident(wbs.shape, full),
            _resident(wo.shape, full),
        ],
        out_specs=pl.BlockSpec((tp, D2), lambda i: (i, 0)),
        out_shape=jax.ShapeDtypeStruct((M2, D2), F32),
        name="merge",
        compiler_params=_params(56, 1),
    )(xp, xp, yhe, yho, ysp, ysp, gp, gp, gp, gp, wbh, wbs, wo)


def _ffn_kernel(x_ref, g_ref, wg_ref, wu_ref, wo_ref, gf_ref, o_ref, h_ref, *, final_norm):
    f = pl.program_id(1)

    @pl.when(f == 0)
    def _():
        x = x_ref[...]
        h_ref[...] = _rms(x, g_ref[...]).astype(BF16)
        o_ref[...] = x

    h = h_ref[...]
    gate = _dot(h, wg_ref[...])
    up = _dot(h, wu_ref[...])
    a = (gate * jax.nn.sigmoid(gate) * up).astype(BF16)
    o_ref[...] += _dot(a, wo_ref[...])

    if final_norm:
        @pl.when(f == pl.num_programs(1) - 1)
        def _():
            o_ref[...] = _rms(o_ref[...], gf_ref[...])


def _ffn(x2, g, w_in, w_out, g_final, final_norm, tm, tf):
    M, D = x2.shape
    FF = w_out.shape[0]
    nf = FF // tf
    full = lambda i, f: (0, 0)
    return pl.pallas_call(
        functools.partial(_ffn_kernel, final_norm=final_norm),
        grid=(M // tm, nf),
        in_specs=[
            pl.BlockSpec((tm, D), lambda i, f: (i, 0)),
            _resident((1, D), full),
            pl.BlockSpec((D, tf), lambda i, f: (0, f)),
            pl.BlockSpec((D, tf), lambda i, f: (0, nf + f)),
            pl.BlockSpec((tf, D), lambda i, f: (f, 0)),
            _resident((1, D), full),
        ],
        out_specs=pl.BlockSpec((tm, D), lambda i, f: (i, 0)),
        out_shape=jax.ShapeDtypeStruct((M, D), F32),
        scratch_shapes=[pltpu.VMEM((tm, D), BF16)],
        name="ffn",
        compiler_params=_params(56, 2),
    )(x2, g, w_in, w_in, w_out, g_final)


def _transform_tables(L):
    H = L // 2
    k = np.arange(H, dtype=np.int64)[:, None]
    m = np.arange(H, dtype=np.int64)[None, :]

    def tab(fn, idx):
        return (fn((idx % (2 * L)) * (np.pi / L)).astype(np.float32)).astype(BF16)

    ce, se = tab(np.cos, k * (2 * m)), tab(np.sin, k * (2 * m))
    co, so = tab(np.cos, k * (2 * m + 1)), tab(np.sin, k * (2 * m + 1))
    tabs = tuple(jnp.asarray(t) for t in (ce, se, co, so, np.ascontiguousarray(co.T), np.ascontiguousarray(so.T)))
    alt = jnp.asarray((1.0 - 2.0 * (np.arange(H) % 2)).astype(np.float32)[:, None])
    return tabs, alt


def _position_features(L, emb):
    t = jnp.linspace(0.0, 1.0, L, dtype=F32)[:, None]
    bands = (emb - 1) // 2
    w = 2.0 * math.pi * jnp.arange(L, dtype=F32)[:, None] / L
    f = jnp.linspace(1e-4, bands - 1, bands, dtype=F32)[None, :]
    return jnp.concatenate([t, jnp.cos(f * w), -jnp.sin(f * w)], axis=-1), t


def _pick(n, pref):
    t = min(n, pref)
    while n % t:
        t -= 1
    return t


def kernel(x, norm_mix_g, w_in, short_conv_w, short_conv_b, filt_w1, filt_b1, filt_w2, filt_b2, filt_w3, filt_b3, filt_freq, filt_w4, hyena_bias, sgu_ln_g, sgu_ln_b, sgu_w_s, sgu_b_s, w_branch_hyena, w_branch_sgu, w_out, norm_ffn_g, w_ffn_in, w_ffn_out, norm_final_g):
    B, L, D = x.shape
    depth = w_in.shape[0]
    n_ch = w_branch_hyena.shape[1]
    ds = w_branch_sgu.shape[1]
    assert n_ch == ds and 2 * n_ch == D and L % 2 == 0 and depth >= 1
    n_mix = 3 * n_ch + 2 * ds
    M = B * L
    H = L // 2

    tabs, alt = _transform_tables(L)
    zfeat, t_col = _position_features(L, filt_w1.shape[1])
    max_decay = math.log(DECAY_TARGET) / FAST_DECAY
    min_decay = math.log(DECAY_TARGET) / SLOW_DECAY
    absdelta = jnp.abs(jnp.linspace(min_decay, max_decay, n_ch, dtype=F32))[None, :]

    ct = _pick(n_ch, 256)
    tm = _pick(M, 1024)
    tn = _pick(n_ch, 1024)

    x2 = x.reshape(M, D)
    for l in range(depth):
        spectra = _hyena_filters(zfeat, t_col, absdelta, alt, tabs,
                                 filt_w1[l], filt_b1[l], filt_w2[l], filt_b2[l],
                                 filt_w3[l], filt_b3[l], filt_freq[l], filt_w4[l], ct)
        mix, gates = _proj(x2, norm_mix_g[l].reshape(1, D), w_in[l].astype(BF16), n_mix, tm, tn)
        yhe, yho = _hyena(mix.reshape(B, H, 2 * n_mix), short_conv_w[l], short_conv_b[l], spectra,
                          hyena_bias[l], alt, tabs, n_ch, ct)
        y_sgu = _sgu(mix, sgu_ln_g[l], sgu_ln_b[l], sgu_w_s[l], sgu_b_s[l], 3 * n_ch, ds, tm)
        x2 = _merge(x2.reshape(M // 2, 2 * D), yhe.reshape(M // 2, n_ch), yho.reshape(M // 2, n_ch),
                    y_sgu.reshape(M // 2, 2 * ds), gates.reshape(M // 2, 4 * D),
                    w_branch_hyena[l].astype(BF16), w_branch_sgu[l].astype(BF16),
                    w_out[l].astype(BF16), _pick(M // 2, 256)).reshape(M, D)
        last = l == depth - 1
        x2 = _ffn(x2, norm_ffn_g[l].reshape(1, D), w_ffn_in[l].astype(BF16),
                  w_ffn_out[l].astype(BF16), norm_final_g.reshape(1, D), last,
                  _pick(M, 512), _pick(w_ffn_out.shape[1], 512))
    return x2.reshape(B, L, D)
```

```python
import functools
import math

import numpy as np
import jax
import jax.numpy as jnp
from jax import lax
from jax.experimental import pallas as pl
from jax.experimental.pallas import tpu as pltpu

F32 = jnp.float32
BF16 = jnp.bfloat16

EPS = 1e-6
FAST_DECAY = 0.3
SLOW_DECAY = 1.5
DECAY_TARGET = 1e-2
MOD_SHIFT = 0.0
N_DIR = 2
HYENA_ORDER = 2
LANES = 128
BF16_ROWS = 16
MIB = 1024 * 1024


def _params(vmem_mib, n_grid):
    return pltpu.CompilerParams(
        dimension_semantics=("arbitrary",) * n_grid,
        vmem_limit_bytes=vmem_mib * MIB)


def _resident(block_shape, index_map):
    return pl.BlockSpec(block_shape, index_map, pipeline_mode=pl.Buffered(1))


def _rms(x, g):
    ms = jnp.mean(x * x, axis=-1, keepdims=True)
    return x * lax.rsqrt(ms + EPS) * g


def _dot(a, b):
    return jnp.dot(a, b, preferred_element_type=F32)


def _filter_feat_kernel(z_ref, w1_ref, b1_ref, w2_ref, b2_ref, w3_ref, b3_ref, fr_ref, o_ref):
    hp = lax.Precision.HIGHEST
    fr = fr_ref[...]
    h = jnp.sin(fr * (jnp.dot(z_ref[...], w1_ref[...], precision=hp, preferred_element_type=F32) + b1_ref[...]))
    h = jnp.sin(fr * (jnp.dot(h, w2_ref[...], precision=hp, preferred_element_type=F32) + b2_ref[...]))
    h = jnp.sin(fr * (jnp.dot(h, w3_ref[...], precision=hp, preferred_element_type=F32) + b3_ref[...]))
    o_ref[...] = h


def _filter_spec_kernel(fe_ref, fo_ref, w4f_ref, w4b_ref, te_ref, to_ref, ad_ref, alt_ref,
                        ce_ref, se_ref, co_ref, so_ref,
                        kra_ref, kia_ref, krb_ref, kib_ref, kh_ref):
    hp = lax.Precision.HIGHEST
    H = fe_ref.shape[0]
    inv_n = 1.0 / (4 * H)
    ad = ad_ref[...]

    def taps(f_ref, t_ref):
        f = f_ref[...]
        decay = jnp.exp(-t_ref[...] * ad) + MOD_SHIFT
        return (jnp.dot(f, w4f_ref[...], precision=hp, preferred_element_type=F32) * decay,
                jnp.dot(f, w4b_ref[...], precision=hp, preferred_element_type=F32) * decay)

    hf_e, hb_e = taps(fe_ref, te_ref)
    hf_o, hb_o = taps(fo_ref, to_ref)
    row = lax.broadcasted_iota(jnp.int32, hf_e.shape, 0)
    hb_e = jnp.where(row == 0, 0.0, hb_e)

    def colsum(a):
        return jnp.sum(a, axis=0, keepdims=True)

    norm = colsum(jnp.abs(hf_e)) + colsum(jnp.abs(hf_o)) + colsum(jnp.abs(hb_e)) + colsum(jnp.abs(hb_o))
    inv = 1.0 / norm
    ev_e, ev_o = (hf_e + hb_e) * inv, (hf_o + hb_o) * inv
    od_e, od_o = (hf_e - hb_e) * inv, (hf_o - hb_o) * inv
    ur = _dot(ce_ref[...], ev_e.astype(BF16))
    vr = _dot(co_ref[...], ev_o.astype(BF16))
    ui = _dot(se_ref[...], od_e.astype(BF16))
    vi = _dot(so_ref[...], od_o.astype(BF16))
    w = jnp.where(row[:, :1] == 0, inv_n, 2.0 * inv_n)
    kra_ref[...] = (ur + vr) * w
    krb_ref[...] = (ur - vr) * w
    kia_ref[...] = (ui + vi) * w
    kib_ref[...] = (vi - ui) * w
    alt = alt_ref[...]
    kh_ref[0:1, :] = colsum(ev_e * alt) * (2.0 * inv_n)
    kh_ref[1:2, :] = colsum(od_o * alt) * (2.0 * inv_n)


def _hyena_filters(zfeat, t_col, absdelta, alt, tabs, w1, b1, w2, b2, w3, b3, freq, w4, tc):
    L = zfeat.shape[0]
    H = L // 2
    c2 = w4.shape[1] // N_DIR
    n_ch = c2 // HYENA_ORDER
    ce, se, co, so = tabs[:4]

    def pad2(a, r, c):
        return jnp.pad(a.astype(F32), ((0, r - a.shape[0]), (0, c - a.shape[1])))

    feat = pl.pallas_call(
        _filter_feat_kernel,
        out_shape=jax.ShapeDtypeStruct((L, LANES), F32),
        name="filter_feat",
        compiler_params=_params(32, 0),
    )(pad2(jnp.concatenate([zfeat[0::2], zfeat[1::2]], axis=0), L, LANES),
      pad2(w1, LANES, LANES), pad2(b1[None], 1, LANES),
      pad2(w2, LANES, LANES), pad2(b2[None], 1, LANES),
      pad2(w3, LANES, LANES), pad2(b3[None], 1, LANES), pad2(freq[None], 1, LANES))

    w4p = pad2(w4, LANES, w4.shape[1])
    nt = c2 // tc
    nct = n_ch // tc
    full = lambda j: (0, 0)
    spec = jax.ShapeDtypeStruct((H, c2), F32)
    tile = pl.BlockSpec((H, tc), lambda j: (0, j))
    return pl.pallas_call(
        _filter_spec_kernel,
        grid=(nt,),
        in_specs=[
            _resident((H, LANES), full),
            _resident((H, LANES), lambda j: (1, 0)),
            pl.BlockSpec((LANES, tc), lambda j: (0, j)),
            pl.BlockSpec((LANES, tc), lambda j: (0, nt + j)),
            _resident((H, 1), full),
            _resident((H, 1), full),
            pl.BlockSpec((1, tc), lambda j: (0, j % nct)),
            _resident((H, 1), full),
            _resident((H, H), full), _resident((H, H), full),
            _resident((H, H), full), _resident((H, H), full),
        ],
        out_specs=[tile, tile, tile, tile, pl.BlockSpec((2, tc), lambda j: (0, j))],
        out_shape=[spec, spec, spec, spec, jax.ShapeDtypeStruct((2, c2), F32)],
        name="filter_spec",
        compiler_params=_params(48, 1),
    )(feat, feat, w4p, w4p, t_col[0::2], t_col[1::2], absdelta, alt, ce, se, co, so)


def _proj_kernel(x_ref, g_ref, w_ref, o1_ref, o2_ref, h_ref, *, n_plain):
    j = pl.program_id(1)

    @pl.when(j == 0)
    def _():
        h_ref[...] = _rms(x_ref[...], g_ref[...]).astype(BF16)

    acc = _dot(h_ref[...], w_ref[...])

    @pl.when(j < n_plain)
    def _():
        o1_ref[...] = acc.astype(BF16)

    @pl.when(j >= n_plain)
    def _():
        o2_ref[...] = jax.nn.sigmoid(acc).astype(BF16)


def _proj(x2, g, w, n_mix, tm, tn):
    M, D = x2.shape
    E = w.shape[1]
    n_plain = n_mix // tn
    nj = E // tn
    return pl.pallas_call(
        functools.partial(_proj_kernel, n_plain=n_plain),
        grid=(M // tm, nj),
        in_specs=[
            pl.BlockSpec((tm, D), lambda i, j: (i, 0)),
            _resident((1, D), lambda i, j: (0, 0)),
            pl.BlockSpec((D, tn), lambda i, j: (0, j)),
        ],
        out_specs=[
            pl.BlockSpec((tm, tn), lambda i, j: (i, jnp.minimum(j, n_plain - 1))),
            pl.BlockSpec((tm, tn), lambda i, j: (i, jnp.maximum(j - n_plain, 0))),
        ],
        out_shape=[jax.ShapeDtypeStruct((M, n_mix), BF16),
                   jax.ShapeDtypeStruct((M, E - n_mix), BF16)],
        scratch_shapes=[pltpu.VMEM((tm, D), BF16)],
        name="proj",
        compiler_params=_params(48, 2),
    )(x2, g, w)


def _short_conv_rows(p_ref, il_ref, w_ref, b_ref, r0, rc):
    L = p_ref.shape[1]
    t0, t1 = 2 * r0, 2 * (r0 + rc)
    pe, po = [], []
    for c in range(il_ref.shape[0]):
        il_ref[c] = p_ref[0, t0:t1, c * LANES:(c + 1) * LANES].astype(F32)
        pe.append(il_ref[c, pl.ds(0, rc, stride=2), :])
        po.append(il_ref[c, pl.ds(1, rc, stride=2), :])
    pe = jnp.concatenate(pe, axis=1)
    po = jnp.concatenate(po, axis=1)
    row = lax.broadcasted_iota(jnp.int32, pe.shape, 0)
    if t0 == 0:
        before = 0.0
    else:
        before = p_ref[0, t0 - BF16_ROWS:t0, :].astype(F32)[BF16_ROWS - 1:BF16_ROWS, :]
    if t1 == L:
        after = 0.0
    else:
        after = p_ref[0, t1:t1 + BF16_ROWS, :].astype(F32)[0:1, :]
    po_prev = jnp.where(row == 0, before, pltpu.roll(po, 1, axis=0))
    pe_next = jnp.where(row == rc - 1, after, pltpu.roll(pe, rc - 1, axis=0))
    w0, w1, w2, b = w_ref[0:1, :], w_ref[1:2, :], w_ref[2:3, :], b_ref[...]
    return (po_prev * w0 + pe * w1 + po * w2 + b,
            pe * w0 + po * w1 + pe_next * w2 + b)


def _hyena_kernel(v_ref, x1_ref, x2_ref,
                  wv_ref, w1_ref, w2_ref, bv_ref, b1_ref, b2_ref,
                  kra0_ref, kia0_ref, krb0_ref, kib0_ref, kh0_ref,
                  kra1_ref, kia1_ref, krb1_ref, kib1_ref, kh1_ref, hb0_ref, hb1_ref,
                  alt_ref, ce_ref, se_ref, co_ref, so_ref, cot_ref, sot_ref,
                  o_ref, zf_ref, zb_ref, p_ref, il_ref, *, rc):
    H = ce_ref.shape[0]
    chunks = [(r0, slice(r0, r0 + rc)) for r0 in range(0, H, rc)]

    def put_z(ze, zo, rows):
        zf_ref[0, rows, :] = ze
        zf_ref[1, rows, :] = zo
        zb_ref[0, rows, :] = ze.astype(BF16)
        zb_ref[1, rows, :] = zo.astype(BF16)
        alt = alt_ref[rows, :]
        return (jnp.sum(ze * alt, axis=0, keepdims=True), jnp.sum(zo * alt, axis=0, keepdims=True))

    def spectrum_product(kra_ref, kia_ref, krb_ref, kib_ref):
        zbe = zb_ref[0]
        zbo = zb_ref[1]
        for _, rows in chunks:
            ur = _dot(ce_ref[rows, :], zbe)
            ui = _dot(se_ref[rows, :], zbe)
            vr = _dot(co_ref[rows, :], zbo)
            vi = _dot(so_ref[rows, :], zbo)
            xra, xia, xrb, xib = ur + vr, ui + vi, ur - vr, vi - ui
            kra, kia, krb, kib = kra_ref[rows, :], kia_ref[rows, :], krb_ref[rows, :], kib_ref[rows, :]
            pra = xra * kra - xia * kia
            pia = xra * kia + xia * kra
            prb = xrb * krb - xib * kib
            pib = xrb * kib + xib * krb
            p_ref[0, rows, :] = (pra + prb).astype(BF16)
            p_ref[1, rows, :] = (pia - pib).astype(BF16)
            p_ref[2, rows, :] = (pra - prb).astype(BF16)
            p_ref[3, rows, :] = (pia + pib).astype(BF16)

    def conv_rows(rows, xh, kh_ref, hb_ref):
        xrh, xih = xh
        krh, kih = kh_ref[0:1, :], kh_ref[1:2, :]
        prh = xrh * krh - xih * kih
        pih = xrh * kih + xih * krh
        alt = alt_ref[rows, :]
        bias = hb_ref[...]
        ye = (_dot(ce_ref[rows, :], p_ref[0]) + _dot(se_ref[rows, :], p_ref[1])
              + alt * prh + zf_ref[0, rows, :] * bias)
        yo = (_dot(cot_ref[rows, :], p_ref[2]) + _dot(sot_ref[rows, :], p_ref[3])
              + alt * pih + zf_ref[1, rows, :] * bias)
        return ye, yo

    def add2(a, b):
        return (a[0] + b[0], a[1] + b[1])

    xh = (0.0, 0.0)
    for r0, rows in chunks:
        ze, zo = _short_conv_rows(v_ref, il_ref, wv_ref, bv_ref, r0, rc)
        xh = add2(xh, put_z(ze, zo, rows))
    spectrum_product(kra0_ref, kia0_ref, krb0_ref, kib0_ref)
    xh1 = (0.0, 0.0)
    for r0, rows in chunks:
        ye, yo = conv_rows(rows, xh, kh0_ref, hb0_ref)
        ge, go = _short_conv_rows(x1_ref, il_ref, w1_ref, b1_ref, r0, rc)
        xh1 = add2(xh1, put_z(ge * ye, go * yo, rows))
    spectrum_product(kra1_ref, kia1_ref, krb1_ref, kib1_ref)
    for r0, rows in chunks:
        ye, yo = conv_rows(rows, xh1, kh1_ref, hb1_ref)
        ge, go = _short_conv_rows(x2_ref, il_ref, w2_ref, b2_ref, r0, rc)
        oe, oo = ge * ye, go * yo
        for c in range(il_ref.shape[0]):
            lanes = slice(c * LANES, (c + 1) * LANES)
            il_ref[c, pl.ds(0, rc, stride=2), :] = oe[:, lanes]
            il_ref[c, pl.ds(1, rc, stride=2), :] = oo[:, lanes]
            o_ref[0, 2 * r0:2 * (r0 + rc), lanes] = il_ref[c].astype(o_ref.dtype)


def _hyena(p3, conv_w, conv_b, spectra, hbias, alt, tabs, n_ch, ct):
    B, L, _ = p3.shape
    H = L // 2
    nct = n_ch // ct
    rc = _pick(H, 512)
    kra, kia, krb, kib, kh = spectra
    hb = hbias.reshape(1, HYENA_ORDER * n_ch).astype(F32)
    cb = conv_b.reshape(1, -1).astype(F32)
    cw = conv_w.astype(F32)

    def pspec(g):
        return pl.BlockSpec((1, L, ct), lambda j, b, g=g: (b, 0, g * nct + j))

    def rowspec(rows, g):
        return pl.BlockSpec((rows, ct), lambda j, b, g=g: (0, g * nct + j))

    def specspec(g):
        return _resident((H, ct), lambda j, b, g=g: (0, g * nct + j))

    full = lambda j, b: (0, 0)
    table = _resident((H, H), full)
    return pl.pallas_call(
        functools.partial(_hyena_kernel, rc=rc),
        grid=(nct, B),
        in_specs=[
            pspec(0), pspec(1), pspec(2),
            rowspec(3, 0), rowspec(3, 1), rowspec(3, 2),
            rowspec(1, 0), rowspec(1, 1), rowspec(1, 2),
            specspec(0), specspec(0), specspec(0), specspec(0), rowspec(2, 0),
            specspec(1), specspec(1), specspec(1), specspec(1), rowspec(2, 1),
            rowspec(1, 0), rowspec(1, 1),
            _resident((H, 1), full),
            table, table, table, table, table, table,
        ],
        out_specs=pl.BlockSpec((1, L, ct), lambda j, b: (b, 0, j)),
        out_shape=jax.ShapeDtypeStruct((B, L, n_ch), BF16),
        scratch_shapes=[pltpu.VMEM((2, H, ct), F32), pltpu.VMEM((2, H, ct), BF16),
                        pltpu.VMEM((4, H, ct), BF16), pltpu.VMEM((ct // LANES, 2 * rc, LANES), F32)],
        name="hyena",
        compiler_params=_params(56, 2),
    )(p3, p3, p3, cw, cw, cw, cb, cb, cb,
      kra, kia, krb, kib, kh, kra, kia, krb, kib, kh, hb, hb, alt, *tabs)


def _gelu(x):
    return 0.5 * x * (1.0 + lax.erf(x * math.sqrt(0.5)))


def _sgu_kernel(u_ref, v_ref, g_ref, b_ref, ws_ref, bs_ref, o_ref, *, heads, chunk):
    u = _gelu(u_ref[...].astype(F32))
    v = _gelu(v_ref[...].astype(F32))
    mu = jnp.mean(v, axis=-1, keepdims=True)
    vc = v - mu
    var = jnp.mean(vc * vc, axis=-1, keepdims=True)
    vn = (vc * lax.rsqrt(var + EPS) * g_ref[...] + b_ref[...]).astype(BF16)
    tm, ds = vn.shape
    hd = ds // heads
    for n in range(tm // chunk):
        rows = slice(n * chunk, (n + 1) * chunk)
        for h in range(heads):
            cols = slice(h * hd, (h + 1) * hd)
            s = _dot(ws_ref[h], vn[rows, cols]) + bs_ref[:, h:h + 1]
            o_ref[rows, cols] = (u[rows, cols] * s).astype(o_ref.dtype)


def _sgu(p2, ln_g, ln_b, w_s, b_s, col0, ds, tm):
    M = p2.shape[0]
    heads, chunk, _ = w_s.shape
    cb = col0 // ds
    full2 = lambda i: (0, 0)
    return pl.pallas_call(
        functools.partial(_sgu_kernel, heads=heads, chunk=chunk),
        grid=(M // tm,),
        in_specs=[
            pl.BlockSpec((tm, ds), lambda i: (i, cb)),
            pl.BlockSpec((tm, ds), lambda i: (i, cb + 1)),
            _resident((1, ds), full2),
            _resident((1, ds), full2),
            _resident((heads, chunk, chunk), lambda i: (0, 0, 0)),
            _resident((chunk, heads), full2),
        ],
        out_specs=pl.BlockSpec((tm, ds), lambda i: (i, 0)),
        out_shape=jax.ShapeDtypeStruct((M, ds), BF16),
        name="sgu",
        compiler_params=_params(48, 1),
    )(p2, p2, ln_g.reshape(1, ds).astype(F32), ln_b.reshape(1, ds).astype(F32),
      w_s.astype(BF16), b_s.T.astype(F32))


def _merge_kernel(x_ref, yh_ref, ys_ref, gh_ref, gs_ref, wbh_ref, wbs_ref, wo_ref, o_ref):
    a = (_dot(yh_ref[...], wbh_ref[...]) * gh_ref[...].astype(F32)
         + _dot(ys_ref[...], wbs_ref[...]) * gs_ref[...].astype(F32))
    o_ref[...] = x_ref[...] + _dot(a.astype(BF16), wo_ref[...])


def _merge(x2, yh, ys, gates, wbh, wbs, wo, tm):
    M, D = x2.shape
    full = lambda i: (0, 0)
    return pl.pallas_call(
        _merge_kernel,
        grid=(M // tm,),
        in_specs=[
            pl.BlockSpec((tm, D), lambda i: (i, 0)),
            pl.BlockSpec((tm, yh.shape[1]), lambda i: (i, 0)),
            pl.BlockSpec((tm, ys.shape[1]), lambda i: (i, 0)),
            pl.BlockSpec((tm, D), lambda i: (i, 0)),
            pl.BlockSpec((tm, D), lambda i: (i, 1)),
            _resident(wbh.shape, full),
            _resident(wbs.shape, full),
            _resident(wo.shape, full),
        ],
        out_specs=pl.BlockSpec((tm, D), lambda i: (i, 0)),
        out_shape=jax.ShapeDtypeStruct((M, D), F32),
        name="merge",
        compiler_params=_params(56, 1),
    )(x2, yh, ys, gates, gates, wbh, wbs, wo)


def _ffn_kernel(x_ref, g_ref, wg_ref, wu_ref, wo_ref, gf_ref, o_ref, h_ref, *, final_norm):
    f = pl.program_id(1)

    @pl.when(f == 0)
    def _():
        x = x_ref[...]
        h_ref[...] = _rms(x, g_ref[...]).astype(BF16)
        o_ref[...] = x

    h = h_ref[...]
    gate = _dot(h, wg_ref[...])
    up = _dot(h, wu_ref[...])
    a = (gate * jax.nn.sigmoid(gate) * up).astype(BF16)
    o_ref[...] += _dot(a, wo_ref[...])

    if final_norm:
        @pl.when(f == pl.num_programs(1) - 1)
        def _():
            o_ref[...] = _rms(o_ref[...], gf_ref[...])


def _ffn(x2, g, w_in, w_out, g_final, final_norm, tm, tf):
    M, D = x2.shape
    FF = w_out.shape[0]
    nf = FF // tf
    full = lambda i, f: (0, 0)
    return pl.pallas_call(
        functools.partial(_ffn_kernel, final_norm=final_norm),
        grid=(M // tm, nf),
        in_specs=[
            pl.BlockSpec((tm, D), lambda i, f: (i, 0)),
            _resident((1, D), full),
            pl.BlockSpec((D, tf), lambda i, f: (0, f)),
            pl.BlockSpec((D, tf), lambda i, f: (0, nf + f)),
            pl.BlockSpec((tf, D), lambda i, f: (f, 0)),
            _resident((1, D), full),
        ],
        out_specs=pl.BlockSpec((tm, D), lambda i, f: (i, 0)),
        out_shape=jax.ShapeDtypeStruct((M, D), F32),
        scratch_shapes=[pltpu.VMEM((tm, D), BF16)],
        name="ffn",
        compiler_params=_params(56, 2),
    )(x2, g, w_in, w_in, w_out, g_final)


def _transform_tables(L):
    H = L // 2
    k = np.arange(H, dtype=np.int64)[:, None]
    m = np.arange(H, dtype=np.int64)[None, :]

    def tab(fn, idx):
        return (fn((idx % (2 * L)) * (np.pi / L)).astype(np.float32)).astype(BF16)

    ce, se = tab(np.cos, k * (2 * m)), tab(np.sin, k * (2 * m))
    co, so = tab(np.cos, k * (2 * m + 1)), tab(np.sin, k * (2 * m + 1))
    tabs = tuple(jnp.asarray(t) for t in (ce, se, co, so, np.ascontiguousarray(co.T), np.ascontiguousarray(so.T)))
    alt = jnp.asarray((1.0 - 2.0 * (np.arange(H) % 2)).astype(np.float32)[:, None])
    return tabs, alt


def _position_features(L, emb):
    t = jnp.linspace(0.0, 1.0, L, dtype=F32)[:, None]
    bands = (emb - 1) // 2
    w = 2.0 * math.pi * jnp.arange(L, dtype=F32)[:, None] / L
    f = jnp.linspace(1e-4, bands - 1, bands, dtype=F32)[None, :]
    return jnp.concatenate([t, jnp.cos(f * w), -jnp.sin(f * w)], axis=-1), t


def _pick(n, pref):
    t = min(n, pref)
    while n % t:
        t -= 1
    return t


def kernel(x, norm_mix_g, w_in, short_conv_w, short_conv_b, filt_w1, filt_b1, filt_w2, filt_b2, filt_w3, filt_b3, filt_freq, filt_w4, hyena_bias, sgu_ln_g, sgu_ln_b, sgu_w_s, sgu_b_s, w_branch_hyena, w_branch_sgu, w_out, norm_ffn_g, w_ffn_in, w_ffn_out, norm_final_g):
    B, L, D = x.shape
    depth = w_in.shape[0]
    n_ch = w_branch_hyena.shape[1]
    ds = w_branch_sgu.shape[1]
    assert n_ch == ds and 2 * n_ch == D and L % 2 == 0 and depth >= 1
    n_mix = 3 * n_ch + 2 * ds
    M = B * L
    H = L // 2

    tabs, alt = _transform_tables(L)
    zfeat, t_col = _position_features(L, filt_w1.shape[1])
    max_decay = math.log(DECAY_TARGET) / FAST_DECAY
    min_decay = math.log(DECAY_TARGET) / SLOW_DECAY
    absdelta = jnp.abs(jnp.linspace(min_decay, max_decay, n_ch, dtype=F32))[None, :]

    ct = _pick(n_ch, 256)
    tm = _pick(M, 1024)
    tn = _pick(n_ch, 1024)

    x2 = x.reshape(M, D)
    for l in range(depth):
        spectra = _hyena_filters(zfeat, t_col, absdelta, alt, tabs,
                                 filt_w1[l], filt_b1[l], filt_w2[l], filt_b2[l],
                                 filt_w3[l], filt_b3[l], filt_freq[l], filt_w4[l], ct)
        mix, gates = _proj(x2, norm_mix_g[l].reshape(1, D), w_in[l].astype(BF16), n_mix, tm, tn)
        y_hy = _hyena(mix.reshape(B, L, n_mix), short_conv_w[l], short_conv_b[l], spectra,
                      hyena_bias[l], alt, tabs, n_ch, ct)
        y_sgu = _sgu(mix, sgu_ln_g[l], sgu_ln_b[l], sgu_w_s[l], sgu_b_s[l], 3 * n_ch, ds, tm)
        x2 = _merge(x2, y_hy.reshape(M, n_ch), y_sgu, gates,
                    w_branch_hyena[l].astype(BF16), w_branch_sgu[l].astype(BF16),
                    w_out[l].astype(BF16), _pick(M, 512))
        last = l == depth - 1
        x2 = _ffn(x2, norm_ffn_g[l].reshape(1, D), w_ffn_in[l].astype(BF16),
                  w_ffn_out[l].astype(BF16), norm_final_g.reshape(1, D), last,
                  _pick(M, 512), _pick(w_ffn_out.shape[1], 512))
    return x2.reshape(B, L, D)
```

```python
import functools
import math

import numpy as np
import jax
import jax.numpy as jnp
from jax import lax
from jax.experimental import pallas as pl
from jax.experimental.pallas import tpu as pltpu

F32 = jnp.float32
BF16 = jnp.bfloat16

EPS = 1e-6
FAST_DECAY = 0.3
SLOW_DECAY = 1.5
DECAY_TARGET = 1e-2
MOD_SHIFT = 0.0
N_DIR = 2
HYENA_ORDER = 2
LANES = 128
BF16_ROWS = 16
MIB = 1024 * 1024


def _params(vmem_mib, n_grid):
    return pltpu.CompilerParams(
        dimension_semantics=("arbitrary",) * n_grid,
        vmem_limit_bytes=vmem_mib * MIB)


def _resident(block_shape, index_map):
    return pl.BlockSpec(block_shape, index_map, pipeline_mode=pl.Buffered(1))


def _rms(x, g):
    ms = jnp.mean(x * x, axis=-1, keepdims=True)
    return x * lax.rsqrt(ms + EPS) * g


def _dot(a, b):
    return jnp.dot(a, b, preferred_element_type=F32)


def _filter_feat_kernel(z_ref, w1_ref, b1_ref, w2_ref, b2_ref, w3_ref, b3_ref, fr_ref, o_ref):
    hp = lax.Precision.HIGHEST
    fr = fr_ref[...]
    h = jnp.sin(fr * (jnp.dot(z_ref[...], w1_ref[...], precision=hp, preferred_element_type=F32) + b1_ref[...]))
    h = jnp.sin(fr * (jnp.dot(h, w2_ref[...], precision=hp, preferred_element_type=F32) + b2_ref[...]))
    h = jnp.sin(fr * (jnp.dot(h, w3_ref[...], precision=hp, preferred_element_type=F32) + b3_ref[...]))
    o_ref[...] = h


def _filter_spec_kernel(fe_ref, fo_ref, w4f_ref, w4b_ref, te_ref, to_ref, ad_ref, alt_ref,
                        ce_ref, se_ref, co_ref, so_ref,
                        kra_ref, kia_ref, krb_ref, kib_ref, kh_ref):
    hp = lax.Precision.HIGHEST
    H = fe_ref.shape[0]
    inv_n = 1.0 / (4 * H)
    ad = ad_ref[...]

    def taps(f_ref, t_ref):
        f = f_ref[...]
        decay = jnp.exp(-t_ref[...] * ad) + MOD_SHIFT
        return (jnp.dot(f, w4f_ref[...], precision=hp, preferred_element_type=F32) * decay,
                jnp.dot(f, w4b_ref[...], precision=hp, preferred_element_type=F32) * decay)

    hf_e, hb_e = taps(fe_ref, te_ref)
    hf_o, hb_o = taps(fo_ref, to_ref)
    row = lax.broadcasted_iota(jnp.int32, hf_e.shape, 0)
    hb_e = jnp.where(row == 0, 0.0, hb_e)

    def colsum(a):
        return jnp.sum(a, axis=0, keepdims=True)

    norm = colsum(jnp.abs(hf_e)) + colsum(jnp.abs(hf_o)) + colsum(jnp.abs(hb_e)) + colsum(jnp.abs(hb_o))
    inv = 1.0 / norm
    ev_e, ev_o = (hf_e + hb_e) * inv, (hf_o + hb_o) * inv
    od_e, od_o = (hf_e - hb_e) * inv, (hf_o - hb_o) * inv
    ur = _dot(ce_ref[...], ev_e.astype(BF16))
    vr = _dot(co_ref[...], ev_o.astype(BF16))
    ui = _dot(se_ref[...], od_e.astype(BF16))
    vi = _dot(so_ref[...], od_o.astype(BF16))
    w = jnp.where(row[:, :1] == 0, inv_n, 2.0 * inv_n)
    kra_ref[...] = (ur + vr) * w
    krb_ref[...] = (ur - vr) * w
    kia_ref[...] = (ui + vi) * w
    kib_ref[...] = (vi - ui) * w
    alt = alt_ref[...]
    kh_ref[0:1, :] = colsum(ev_e * alt) * (2.0 * inv_n)
    kh_ref[1:2, :] = colsum(od_o * alt) * (2.0 * inv_n)


def _hyena_filters(zfeat, t_col, absdelta, alt, tabs, w1, b1, w2, b2, w3, b3, freq, w4, tc):
    L = zfeat.shape[0]
    H = L // 2
    c2 = w4.shape[1] // N_DIR
    n_ch = c2 // HYENA_ORDER
    ce, se, co, so = tabs[:4]

    def pad2(a, r, c):
        return jnp.pad(a.astype(F32), ((0, r - a.shape[0]), (0, c - a.shape[1])))

    feat = pl.pallas_call(
        _filter_feat_kernel,
        out_shape=jax.ShapeDtypeStruct((L, LANES), F32),
        name="filter_feat",
        compiler_params=_params(32, 0),
    )(pad2(jnp.concatenate([zfeat[0::2], zfeat[1::2]], axis=0), L, LANES),
      pad2(w1, LANES, LANES), pad2(b1[None], 1, LANES),
      pad2(w2, LANES, LANES), pad2(b2[None], 1, LANES),
      pad2(w3, LANES, LANES), pad2(b3[None], 1, LANES), pad2(freq[None], 1, LANES))

    w4p = pad2(w4, LANES, w4.shape[1])
    nt = c2 // tc
    nct = n_ch // tc
    full = lambda j: (0, 0)
    spec = jax.ShapeDtypeStruct((H, c2), F32)
    tile = pl.BlockSpec((H, tc), lambda j: (0, j))
    return pl.pallas_call(
        _filter_spec_kernel,
        grid=(nt,),
        in_specs=[
            _resident((H, LANES), full),
            _resident((H, LANES), lambda j: (1, 0)),
            pl.BlockSpec((LANES, tc), lambda j: (0, j)),
            pl.BlockSpec((LANES, tc), lambda j: (0, nt + j)),
            _resident((H, 1), full),
            _resident((H, 1), full),
            pl.BlockSpec((1, tc), lambda j: (0, j % nct)),
            _resident((H, 1), full),
            _resident((H, H), full), _resident((H, H), full),
            _resident((H, H), full), _resident((H, H), full),
        ],
        out_specs=[tile, tile, tile, tile, pl.BlockSpec((2, tc), lambda j: (0, j))],
        out_shape=[spec, spec, spec, spec, jax.ShapeDtypeStruct((2, c2), F32)],
        name="filter_spec",
        compiler_params=_params(48, 1),
    )(feat, feat, w4p, w4p, t_col[0::2], t_col[1::2], absdelta, alt, ce, se, co, so)


def _proj_kernel(x_ref, g_ref, w_ref, o_ref, h_ref, *, n_plain):
    j = pl.program_id(1)

    @pl.when(j == 0)
    def _():
        h_ref[...] = _rms(x_ref[...], g_ref[...]).astype(BF16)

    acc = _dot(h_ref[...], w_ref[...].astype(BF16))
    o_ref[...] = jnp.where(j >= n_plain, jax.nn.sigmoid(acc), acc).astype(BF16)


def _proj(x2, g, w, n_mix, tm, tn):
    M, D = x2.shape
    E = w.shape[1]
    return pl.pallas_call(
        functools.partial(_proj_kernel, n_plain=n_mix // tn),
        grid=(M // tm, E // tn),
        in_specs=[
            pl.BlockSpec((tm, D), lambda i, j: (i, 0)),
            _resident((1, D), lambda i, j: (0, 0)),
            pl.BlockSpec((D, tn), lambda i, j: (0, j)),
        ],
        out_specs=pl.BlockSpec((tm, tn), lambda i, j: (i, j)),
        out_shape=jax.ShapeDtypeStruct((M, E), BF16),
        scratch_shapes=[pltpu.VMEM((tm, D), BF16)],
        name="proj",
        compiler_params=_params(56, 2),
    )(x2, g, w)


def _short_conv_rows(p_ref, il_ref, w_ref, b_ref, r0, rc):
    L = p_ref.shape[1]
    t0, t1 = 2 * r0, 2 * (r0 + rc)
    pe, po = [], []
    for c in range(il_ref.shape[0]):
        il_ref[c] = p_ref[0, t0:t1, c * LANES:(c + 1) * LANES].astype(F32)
        pe.append(il_ref[c, pl.ds(0, rc, stride=2), :])
        po.append(il_ref[c, pl.ds(1, rc, stride=2), :])
    pe = jnp.concatenate(pe, axis=1)
    po = jnp.concatenate(po, axis=1)
    row = lax.broadcasted_iota(jnp.int32, pe.shape, 0)
    if t0 == 0:
        before = 0.0
    else:
        before = p_ref[0, t0 - BF16_ROWS:t0, :].astype(F32)[BF16_ROWS - 1:BF16_ROWS, :]
    if t1 == L:
        after = 0.0
    else:
        after = p_ref[0, t1:t1 + BF16_ROWS, :].astype(F32)[0:1, :]
    po_prev = jnp.where(row == 0, before, pltpu.roll(po, 1, axis=0))
    pe_next = jnp.where(row == rc - 1, after, pltpu.roll(pe, rc - 1, axis=0))
    w0, w1, w2, b = w_ref[0:1, :], w_ref[1:2, :], w_ref[2:3, :], b_ref[...]
    return (po_prev * w0 + pe * w1 + po * w2 + b,
            pe * w0 + po * w1 + pe_next * w2 + b)


def _hyena_kernel(v_ref, x1_ref, x2_ref,
                  wv_ref, w1_ref, w2_ref, bv_ref, b1_ref, b2_ref,
                  kra0_ref, kia0_ref, krb0_ref, kib0_ref, kh0_ref,
                  kra1_ref, kia1_ref, krb1_ref, kib1_ref, kh1_ref, hb0_ref, hb1_ref,
                  alt_ref, ce_ref, se_ref, co_ref, so_ref, cot_ref, sot_ref,
                  *refs, rc, n_cast):
    o_ref = refs[n_cast]
    zf_ref, zb_ref, p_ref, il_ref = refs[2 * n_cast + 1:]
    for src, dst in zip(refs[:n_cast], refs[n_cast + 1:2 * n_cast + 1]):
        dst[...] = src[...].astype(dst.dtype)
    H = ce_ref.shape[0]
    chunks = [(r0, slice(r0, r0 + rc)) for r0 in range(0, H, rc)]

    def put_z(ze, zo, rows):
        zf_ref[0, rows, :] = ze
        zf_ref[1, rows, :] = zo
        zb_ref[0, rows, :] = ze.astype(BF16)
        zb_ref[1, rows, :] = zo.astype(BF16)
        alt = alt_ref[rows, :]
        return (jnp.sum(ze * alt, axis=0, keepdims=True), jnp.sum(zo * alt, axis=0, keepdims=True))

    def spectrum_product(kra_ref, kia_ref, krb_ref, kib_ref):
        zbe = zb_ref[0]
        zbo = zb_ref[1]
        for _, rows in chunks:
            ur = _dot(ce_ref[rows, :], zbe)
            ui = _dot(se_ref[rows, :], zbe)
            vr = _dot(co_ref[rows, :], zbo)
            vi = _dot(so_ref[rows, :], zbo)
            xra, xia, xrb, xib = ur + vr, ui + vi, ur - vr, vi - ui
            kra, kia, krb, kib = kra_ref[rows, :], kia_ref[rows, :], krb_ref[rows, :], kib_ref[rows, :]
            pra = xra * kra - xia * kia
            pia = xra * kia + xia * kra
            prb = xrb * krb - xib * kib
            pib = xrb * kib + xib * krb
            p_ref[0, rows, :] = (pra + prb).astype(BF16)
            p_ref[1, rows, :] = (pia - pib).astype(BF16)
            p_ref[2, rows, :] = (pra - prb).astype(BF16)
            p_ref[3, rows, :] = (pia + pib).astype(BF16)

    def conv_rows(rows, xh, kh_ref, hb_ref):
        xrh, xih = xh
        krh, kih = kh_ref[0:1, :], kh_ref[1:2, :]
        prh = xrh * krh - xih * kih
        pih = xrh * kih + xih * krh
        alt = alt_ref[rows, :]
        bias = hb_ref[...]
        ye = (_dot(ce_ref[rows, :], p_ref[0]) + _dot(se_ref[rows, :], p_ref[1])
              + alt * prh + zf_ref[0, rows, :] * bias)
        yo = (_dot(cot_ref[rows, :], p_ref[2]) + _dot(sot_ref[rows, :], p_ref[3])
              + alt * pih + zf_ref[1, rows, :] * bias)
        return ye, yo

    def add2(a, b):
        return (a[0] + b[0], a[1] + b[1])

    xh = (0.0, 0.0)
    for r0, rows in chunks:
        ze, zo = _short_conv_rows(v_ref, il_ref, wv_ref, bv_ref, r0, rc)
        xh = add2(xh, put_z(ze, zo, rows))
    spectrum_product(kra0_ref, kia0_ref, krb0_ref, kib0_ref)
    xh1 = (0.0, 0.0)
    for r0, rows in chunks:
        ye, yo = conv_rows(rows, xh, kh0_ref, hb0_ref)
        ge, go = _short_conv_rows(x1_ref, il_ref, w1_ref, b1_ref, r0, rc)
        xh1 = add2(xh1, put_z(ge * ye, go * yo, rows))
    spectrum_product(kra1_ref, kia1_ref, krb1_ref, kib1_ref)
    for r0, rows in chunks:
        ye, yo = conv_rows(rows, xh1, kh1_ref, hb1_ref)
        ge, go = _short_conv_rows(x2_ref, il_ref, w2_ref, b2_ref, r0, rc)
        oe, oo = ge * ye, go * yo
        for c in range(il_ref.shape[0]):
            lanes = slice(c * LANES, (c + 1) * LANES)
            il_ref[c, pl.ds(0, rc, stride=2), :] = oe[:, lanes]
            il_ref[c, pl.ds(1, rc, stride=2), :] = oo[:, lanes]
            o_ref[0, 2 * r0:2 * (r0 + rc), lanes] = il_ref[c].astype(o_ref.dtype)


def _hyena(p3, conv_w, conv_b, spectra, hbias, alt, tabs, n_ch, ct, to_cast):
    B, L, _ = p3.shape
    H = L // 2
    nct = n_ch // ct
    rc = _pick(H, 512)
    kra, kia, krb, kib, kh = spectra
    hb = hbias.reshape(1, HYENA_ORDER * n_ch).astype(F32)
    cb = conv_b.reshape(1, -1).astype(F32)
    cw = conv_w.astype(F32)

    def pspec(g):
        return pl.BlockSpec((1, L, ct), lambda j, b, g=g: (b, 0, g * nct + j))

    def rowspec(rows, g):
        return pl.BlockSpec((rows, ct), lambda j, b, g=g: (0, g * nct + j))

    def specspec(g):
        return _resident((H, ct), lambda j, b, g=g: (0, g * nct + j))

    steps = nct * B
    cast_specs = []
    for w in to_cast:
        slab = w.shape[0] // steps
        assert slab * steps == w.shape[0] and slab % BF16_ROWS == 0, w.shape
        cast_specs.append(pl.BlockSpec((slab, w.shape[1]), lambda j, b: (j * B + b, 0)))

    full = lambda j, b: (0, 0)
    table = _resident((H, H), full)
    return pl.pallas_call(
        functools.partial(_hyena_kernel, rc=rc, n_cast=len(to_cast)),
        grid=(nct, B),
        in_specs=[
            pspec(0), pspec(1), pspec(2),
            rowspec(3, 0), rowspec(3, 1), rowspec(3, 2),
            rowspec(1, 0), rowspec(1, 1), rowspec(1, 2),
            specspec(0), specspec(0), specspec(0), specspec(0), rowspec(2, 0),
            specspec(1), specspec(1), specspec(1), specspec(1), rowspec(2, 1),
            rowspec(1, 0), rowspec(1, 1),
            _resident((H, 1), full),
            table, table, table, table, table, table,
        ] + cast_specs,
        out_specs=[pl.BlockSpec((1, L, ct), lambda j, b: (b, 0, j))] + cast_specs,
        out_shape=[jax.ShapeDtypeStruct((B, L, n_ch), BF16)]
                  + [jax.ShapeDtypeStruct(w.shape, BF16) for w in to_cast],
        scratch_shapes=[pltpu.VMEM((2, H, ct), F32), pltpu.VMEM((2, H, ct), BF16),
                        pltpu.VMEM((4, H, ct), BF16), pltpu.VMEM((ct // LANES, 2 * rc, LANES), F32)],
        name="hyena",
        compiler_params=_params(58, 2),
    )(p3, p3, p3, cw, cw, cw, cb, cb, cb,
      kra, kia, krb, kib, kh, kra, kia, krb, kib, kh, hb, hb, alt, *tabs, *to_cast)


def _gelu(x):
    return 0.5 * x * (1.0 + lax.erf(x * math.sqrt(0.5)))


def _sgu_kernel(u_ref, v_ref, g_ref, b_ref, ws_ref, bs_ref, o_ref, *, heads, chunk):
    u = _gelu(u_ref[...].astype(F32))
    v = _gelu(v_ref[...].astype(F32))
    mu = jnp.mean(v, axis=-1, keepdims=True)
    vc = v - mu
    var = jnp.mean(vc * vc, axis=-1, keepdims=True)
    vn = (vc * lax.rsqrt(var + EPS) * g_ref[...] + b_ref[...]).astype(BF16)
    tm, ds = vn.shape
    hd = ds // heads
    for n in range(tm // chunk):
        rows = slice(n * chunk, (n + 1) * chunk)
        for h in range(heads):
            cols = slice(h * hd, (h + 1) * hd)
            s = _dot(ws_ref[h], vn[rows, cols]) + bs_ref[:, h:h + 1]
            o_ref[rows, cols] = (u[rows, cols] * s).astype(o_ref.dtype)


def _sgu(p2, ln_g, ln_b, w_s, b_s, col0, ds, tm):
    M = p2.shape[0]
    heads, chunk, _ = w_s.shape
    cb = col0 // ds
    full2 = lambda i: (0, 0)
    return pl.pallas_call(
        functools.partial(_sgu_kernel, heads=heads, chunk=chunk),
        grid=(M // tm,),
        in_specs=[
            pl.BlockSpec((tm, ds), lambda i: (i, cb)),
            pl.BlockSpec((tm, ds), lambda i: (i, cb + 1)),
            _resident((1, ds), full2),
            _resident((1, ds), full2),
            _resident((heads, chunk, chunk), lambda i: (0, 0, 0)),
            _resident((chunk, heads), full2),
        ],
        out_specs=pl.BlockSpec((tm, ds), lambda i: (i, 0)),
        out_shape=jax.ShapeDtypeStruct((M, ds), BF16),
        name="sgu",
        compiler_params=_params(48, 1),
    )(p2, p2, ln_g.reshape(1, ds).astype(F32), ln_b.reshape(1, ds).astype(F32),
      w_s.astype(BF16), b_s.T.astype(F32))


def _merge_kernel(x_ref, yh_ref, ys_ref, gh0_ref, gh1_ref, gs0_ref, gs1_ref,
                  wbh_ref, wbs_ref, wo_ref, o_ref):
    gh = jnp.concatenate([gh0_ref[...], gh1_ref[...]], axis=1).astype(F32)
    gs = jnp.concatenate([gs0_ref[...], gs1_ref[...]], axis=1).astype(F32)
    a = _dot(yh_ref[...], wbh_ref[...]) * gh + _dot(ys_ref[...], wbs_ref[...]) * gs
    o_ref[...] = x_ref[...] + _dot(a.astype(BF16), wo_ref[...])


def _merge(x2, yh, ys, proj, gate_col, wbh, wbs, wo, tm):
    M, D = x2.shape
    gw = D // 2
    gb = gate_col // gw
    assert gate_col % gw == 0
    full = lambda i: (0, 0)

    def gate(c):
        return pl.BlockSpec((tm, gw), lambda i, c=c: (i, gb + c))

    return pl.pallas_call(
        _merge_kernel,
        grid=(M // tm,),
        in_specs=[
            pl.BlockSpec((tm, D), lambda i: (i, 0)),
            pl.BlockSpec((tm, yh.shape[1]), lambda i: (i, 0)),
            pl.BlockSpec((tm, ys.shape[1]), lambda i: (i, 0)),
            gate(0), gate(1), gate(2), gate(3),
            _resident(wbh.shape, full),
            _resident(wbs.shape, full),
            _resident(wo.shape, full),
        ],
        out_specs=pl.BlockSpec((tm, D), lambda i: (i, 0)),
        out_shape=jax.ShapeDtypeStruct((M, D), F32),
        name="merge",
        compiler_params=_params(56, 1),
    )(x2, yh, ys, proj, proj, proj, proj, wbh, wbs, wo)


def _ffn_kernel(x_ref, g_ref, wg_ref, wu_ref, wo_ref, gf_ref, o_ref, h_ref, *, final_norm):
    f = pl.program_id(1)

    @pl.when(f == 0)
    def _():
        x = x_ref[...]
        h_ref[...] = _rms(x, g_ref[...]).astype(BF16)
        o_ref[...] = x

    h = h_ref[...]
    gate = _dot(h, wg_ref[...])
    up = _dot(h, wu_ref[...])
    a = (gate * jax.nn.sigmoid(gate) * up).astype(BF16)
    o_ref[...] += _dot(a, wo_ref[...])

    if final_norm:
        @pl.when(f == pl.num_programs(1) - 1)
        def _():
            o_ref[...] = _rms(o_ref[...], gf_ref[...])


def _ffn(x2, g, w_in, w_out, g_final, final_norm, tm, tf):
    M, D = x2.shape
    FF = w_out.shape[0]
    nf = FF // tf
    full = lambda i, f: (0, 0)
    return pl.pallas_call(
        functools.partial(_ffn_kernel, final_norm=final_norm),
        grid=(M // tm, nf),
        in_specs=[
            pl.BlockSpec((tm, D), lambda i, f: (i, 0)),
            _resident((1, D), full),
            pl.BlockSpec((D, tf), lambda i, f: (0, f)),
            pl.BlockSpec((D, tf), lambda i, f: (0, nf + f)),
            pl.BlockSpec((tf, D), lambda i, f: (f, 0)),
            _resident((1, D), full),
        ],
        out_specs=pl.BlockSpec((tm, D), lambda i, f: (i, 0)),
        out_shape=jax.ShapeDtypeStruct((M, D), F32),
        scratch_shapes=[pltpu.VMEM((tm, D), BF16)],
        name="ffn",
        compiler_params=_params(56, 2),
    )(x2, g, w_in, w_in, w_out, g_final)


def _transform_tables(L):
    H = L // 2
    k = np.arange(H, dtype=np.int64)[:, None]
    m = np.arange(H, dtype=np.int64)[None, :]

    def tab(fn, idx):
        return (fn((idx % (2 * L)) * (np.pi / L)).astype(np.float32)).astype(BF16)

    ce, se = tab(np.cos, k * (2 * m)), tab(np.sin, k * (2 * m))
    co, so = tab(np.cos, k * (2 * m + 1)), tab(np.sin, k * (2 * m + 1))
    tabs = tuple(jnp.asarray(t) for t in (ce, se, co, so, np.ascontiguousarray(co.T), np.ascontiguousarray(so.T)))
    alt = jnp.asarray((1.0 - 2.0 * (np.arange(H) % 2)).astype(np.float32)[:, None])
    return tabs, alt


def _position_features(L, emb):
    t = jnp.linspace(0.0, 1.0, L, dtype=F32)[:, None]
    bands = (emb - 1) // 2
    w = 2.0 * math.pi * jnp.arange(L, dtype=F32)[:, None] / L
    f = jnp.linspace(1e-4, bands - 1, bands, dtype=F32)[None, :]
    return jnp.concatenate([t, jnp.cos(f * w), -jnp.sin(f * w)], axis=-1), t


def _pick(n, pref):
    t = min(n, pref)
    while n % t:
        t -= 1
    return t


def kernel(x, norm_mix_g, w_in, short_conv_w, short_conv_b, filt_w1, filt_b1, filt_w2, filt_b2, filt_w3, filt_b3, filt_freq, filt_w4, hyena_bias, sgu_ln_g, sgu_ln_b, sgu_w_s, sgu_b_s, w_branch_hyena, w_branch_sgu, w_out, norm_ffn_g, w_ffn_in, w_ffn_out, norm_final_g):
    B, L, D = x.shape
    depth = w_in.shape[0]
    n_ch = w_branch_hyena.shape[1]
    ds = w_branch_sgu.shape[1]
    assert n_ch == ds and 2 * n_ch == D and L % 2 == 0 and depth >= 1
    n_mix = 3 * n_ch + 2 * ds
    M = B * L
    H = L // 2

    tabs, alt = _transform_tables(L)
    zfeat, t_col = _position_features(L, filt_w1.shape[1])
    max_decay = math.log(DECAY_TARGET) / FAST_DECAY
    min_decay = math.log(DECAY_TARGET) / SLOW_DECAY
    absdelta = jnp.abs(jnp.linspace(min_decay, max_decay, n_ch, dtype=F32))[None, :]

    ct = _pick(n_ch, 256)
    tm = _pick(M, 1024)
    tn = _pick(n_ch, 1024)

    x2 = x.reshape(M, D)
    for l in range(depth):
        spectra = _hyena_filters(zfeat, t_col, absdelta, alt, tabs,
                                 filt_w1[l], filt_b1[l], filt_w2[l], filt_b2[l],
                                 filt_w3[l], filt_b3[l], filt_freq[l], filt_w4[l], ct)
        proj = _proj(x2, norm_mix_g[l].reshape(1, D), w_in[l], n_mix, tm, tn)
        y_hy, wbh, wbs, wo, wfi, wfo = _hyena(
            proj.reshape(B, L, proj.shape[1]), short_conv_w[l], short_conv_b[l], spectra,
            hyena_bias[l], alt, tabs, n_ch, ct,
            [w_branch_hyena[l], w_branch_sgu[l], w_out[l], w_ffn_in[l], w_ffn_out[l]])
        y_sgu = _sgu(proj, sgu_ln_g[l], sgu_ln_b[l], sgu_w_s[l], sgu_b_s[l], 3 * n_ch, ds, tm)
        x2 = _merge(x2, y_hy.reshape(M, n_ch), y_sgu, proj, n_mix, wbh, wbs, wo, _pick(M, 512))
        last = l == depth - 1
        x2 = _ffn(x2, norm_ffn_g[l].reshape(1, D), wfi, wfo, norm_final_g.reshape(1, D), last,
                  _pick(M, 512), _pick(w_ffn_out.shape[1], 512))
    return x2.reshape(B, L, D)
```

```python
import functools
import math

import numpy as np
import jax
import jax.numpy as jnp
from jax import lax
from jax.experimental import pallas as pl
from jax.experimental.pallas import tpu as pltpu

F32 = jnp.float32
BF16 = jnp.bfloat16

EPS = 1e-6
FAST_DECAY = 0.3
SLOW_DECAY = 1.5
DECAY_TARGET = 1e-2
MOD_SHIFT = 0.0
N_DIR = 2
HYENA_ORDER = 2
LANES = 128
BF16_ROWS = 16
MIB = 1024 * 1024


def _params(vmem_mib, n_grid):
    return pltpu.CompilerParams(
        dimension_semantics=("arbitrary",) * n_grid,
        vmem_limit_bytes=vmem_mib * MIB)


def _resident(block_shape, index_map):
    return pl.BlockSpec(block_shape, index_map, pipeline_mode=pl.Buffered(1))


def _rms(x, g):
    ms = jnp.mean(x * x, axis=-1, keepdims=True)
    return x * lax.rsqrt(ms + EPS) * g


def _dot(a, b):
    return jnp.dot(a, b, preferred_element_type=F32)


def _filter_feat_kernel(z_ref, w1_ref, b1_ref, w2_ref, b2_ref, w3_ref, b3_ref, fr_ref, o_ref):
    hp = lax.Precision.HIGHEST
    fr = fr_ref[...]
    h = jnp.sin(fr * (jnp.dot(z_ref[...], w1_ref[...], precision=hp, preferred_element_type=F32) + b1_ref[...]))
    h = jnp.sin(fr * (jnp.dot(h, w2_ref[...], precision=hp, preferred_element_type=F32) + b2_ref[...]))
    h = jnp.sin(fr * (jnp.dot(h, w3_ref[...], precision=hp, preferred_element_type=F32) + b3_ref[...]))
    o_ref[...] = h


def _filter_spec_kernel(fe_ref, fo_ref, w4f_ref, w4b_ref, te_ref, to_ref, ad_ref, alt_ref,
                        ce_ref, se_ref, co_ref, so_ref, wsrc_ref,
                        kra_ref, kia_ref, krb_ref, kib_ref, kh_ref, wdst_ref):
    wdst_ref[...] = wsrc_ref[...].astype(wdst_ref.dtype)
    hp = lax.Precision.HIGHEST
    H = fe_ref.shape[0]
    inv_n = 1.0 / (4 * H)
    ad = ad_ref[...]

    def taps(f_ref, t_ref):
        f = f_ref[...]
        decay = jnp.exp(-t_ref[...] * ad) + MOD_SHIFT
        return (jnp.dot(f, w4f_ref[...], precision=hp, preferred_element_type=F32) * decay,
                jnp.dot(f, w4b_ref[...], precision=hp, preferred_element_type=F32) * decay)

    hf_e, hb_e = taps(fe_ref, te_ref)
    hf_o, hb_o = taps(fo_ref, to_ref)
    row = lax.broadcasted_iota(jnp.int32, hf_e.shape, 0)
    hb_e = jnp.where(row == 0, 0.0, hb_e)

    def colsum(a):
        return jnp.sum(a, axis=0, keepdims=True)

    norm = colsum(jnp.abs(hf_e)) + colsum(jnp.abs(hf_o)) + colsum(jnp.abs(hb_e)) + colsum(jnp.abs(hb_o))
    inv = 1.0 / norm
    ev_e, ev_o = (hf_e + hb_e) * inv, (hf_o + hb_o) * inv
    od_e, od_o = (hf_e - hb_e) * inv, (hf_o - hb_o) * inv
    ur = _dot(ce_ref[...], ev_e.astype(BF16))
    vr = _dot(co_ref[...], ev_o.astype(BF16))
    ui = _dot(se_ref[...], od_e.astype(BF16))
    vi = _dot(so_ref[...], od_o.astype(BF16))
    w = jnp.where(row[:, :1] == 0, inv_n, 2.0 * inv_n)
    kra_ref[...] = (ur + vr) * w
    krb_ref[...] = (ur - vr) * w
    kia_ref[...] = (ui + vi) * w
    kib_ref[...] = (vi - ui) * w
    alt = alt_ref[...]
    kh_ref[0:1, :] = colsum(ev_e * alt) * (2.0 * inv_n)
    kh_ref[1:2, :] = colsum(od_o * alt) * (2.0 * inv_n)


def _hyena_filters(zfeat, t_col, absdelta, alt, tabs, w1, b1, w2, b2, w3, b3, freq, w4, tc, w_cast):
    L = zfeat.shape[0]
    H = L // 2
    c2 = w4.shape[1] // N_DIR
    n_ch = c2 // HYENA_ORDER
    ce, se, co, so = tabs[:4]

    def pad2(a, r, c):
        return jnp.pad(a.astype(F32), ((0, r - a.shape[0]), (0, c - a.shape[1])))

    feat = pl.pallas_call(
        _filter_feat_kernel,
        out_shape=jax.ShapeDtypeStruct((L, LANES), F32),
        name="filter_feat",
        compiler_params=_params(32, 0),
    )(pad2(jnp.concatenate([zfeat[0::2], zfeat[1::2]], axis=0), L, LANES),
      pad2(w1, LANES, LANES), pad2(b1[None], 1, LANES),
      pad2(w2, LANES, LANES), pad2(b2[None], 1, LANES),
      pad2(w3, LANES, LANES), pad2(b3[None], 1, LANES), pad2(freq[None], 1, LANES))

    w4p = pad2(w4, LANES, w4.shape[1])
    nt = c2 // tc
    nct = n_ch // tc
    full = lambda j: (0, 0)
    spec = jax.ShapeDtypeStruct((H, c2), F32)
    tile = pl.BlockSpec((H, tc), lambda j: (0, j))
    slab = w_cast.shape[0] // nt
    assert slab * nt == w_cast.shape[0] and slab % BF16_ROWS == 0
    cast_spec = pl.BlockSpec((slab, w_cast.shape[1]), lambda j: (j, 0))
    *spectra, w_bf16 = pl.pallas_call(
        _filter_spec_kernel,
        grid=(nt,),
        in_specs=[
            _resident((H, LANES), full),
            _resident((H, LANES), lambda j: (1, 0)),
            pl.BlockSpec((LANES, tc), lambda j: (0, j)),
            pl.BlockSpec((LANES, tc), lambda j: (0, nt + j)),
            _resident((H, 1), full),
            _resident((H, 1), full),
            pl.BlockSpec((1, tc), lambda j: (0, j % nct)),
            _resident((H, 1), full),
            _resident((H, H), full), _resident((H, H), full),
            _resident((H, H), full), _resident((H, H), full),
            cast_spec,
        ],
        out_specs=[tile, tile, tile, tile, pl.BlockSpec((2, tc), lambda j: (0, j)), cast_spec],
        out_shape=[spec, spec, spec, spec, jax.ShapeDtypeStruct((2, c2), F32),
                   jax.ShapeDtypeStruct(w_cast.shape, BF16)],
        name="filter_spec",
        compiler_params=_params(58, 1),
    )(feat, feat, w4p, w4p, t_col[0::2], t_col[1::2], absdelta, alt, ce, se, co, so, w_cast)
    return spectra, w_bf16


def _proj_kernel(x_ref, g_ref, w_ref, o_ref, h_ref, *, n_mix):
    j = pl.program_id(1)

    @pl.when(j == 0)
    def _():
        h_ref[...] = _rms(x_ref[...], g_ref[...]).astype(BF16)

    acc = _dot(h_ref[...], w_ref[...])
    col = j * acc.shape[1] + lax.broadcasted_iota(jnp.int32, acc.shape, 1)
    o_ref[...] = jnp.where(col >= n_mix, jax.nn.sigmoid(acc), acc).astype(BF16)


def _proj(x2, g, w, n_mix, tm, tn):
    M, D = x2.shape
    E = w.shape[1]
    return pl.pallas_call(
        functools.partial(_proj_kernel, n_mix=n_mix),
        grid=(M // tm, E // tn),
        in_specs=[
            pl.BlockSpec((tm, D), lambda i, j: (i, 0)),
            _resident((1, D), lambda i, j: (0, 0)),
            pl.BlockSpec((D, tn), lambda i, j: (0, j)),
        ],
        out_specs=pl.BlockSpec((tm, tn), lambda i, j: (i, j)),
        out_shape=jax.ShapeDtypeStruct((M, E), BF16),
        scratch_shapes=[pltpu.VMEM((tm, D), BF16)],
        name="proj",
        compiler_params=_params(56, 2),
    )(x2, g, w)


def _short_conv_rows(p_ref, il_ref, w_ref, b_ref, r0, rc):
    L = p_ref.shape[1]
    t0, t1 = 2 * r0, 2 * (r0 + rc)
    pe, po = [], []
    for c in range(il_ref.shape[0]):
        il_ref[c] = p_ref[0, t0:t1, c * LANES:(c + 1) * LANES].astype(F32)
        pe.append(il_ref[c, pl.ds(0, rc, stride=2), :])
        po.append(il_ref[c, pl.ds(1, rc, stride=2), :])
    pe = jnp.concatenate(pe, axis=1)
    po = jnp.concatenate(po, axis=1)
    row = lax.broadcasted_iota(jnp.int32, pe.shape, 0)
    if t0 == 0:
        before = 0.0
    else:
        before = p_ref[0, t0 - BF16_ROWS:t0, :].astype(F32)[BF16_ROWS - 1:BF16_ROWS, :]
    if t1 == L:
        after = 0.0
    else:
        after = p_ref[0, t1:t1 + BF16_ROWS, :].astype(F32)[0:1, :]
    po_prev = jnp.where(row == 0, before, pltpu.roll(po, 1, axis=0))
    pe_next = jnp.where(row == rc - 1, after, pltpu.roll(pe, rc - 1, axis=0))
    w0, w1, w2, b = w_ref[0:1, :], w_ref[1:2, :], w_ref[2:3, :], b_ref[...]
    return (po_prev * w0 + pe * w1 + po * w2 + b,
            pe * w0 + po * w1 + pe_next * w2 + b)


def _hyena_kernel(v_ref, x1_ref, x2_ref,
                  wv_ref, w1_ref, w2_ref, bv_ref, b1_ref, b2_ref,
                  kra0_ref, kia0_ref, krb0_ref, kib0_ref, kh0_ref,
                  kra1_ref, kia1_ref, krb1_ref, kib1_ref, kh1_ref, hb0_ref, hb1_ref,
                  alt_ref, ce_ref, se_ref, co_ref, so_ref, cot_ref, sot_ref,
                  *refs, rc, n_cast):
    o_ref = refs[n_cast]
    zf_ref, zb_ref, p_ref, il_ref = refs[2 * n_cast + 1:]
    for src, dst in zip(refs[:n_cast], refs[n_cast + 1:2 * n_cast + 1]):
        dst[...] = src[...].astype(dst.dtype)
    H = ce_ref.shape[0]
    chunks = [(r0, slice(r0, r0 + rc)) for r0 in range(0, H, rc)]

    def put_z(ze, zo, rows):
        zf_ref[0, rows, :] = ze
        zf_ref[1, rows, :] = zo
        zb_ref[0, rows, :] = ze.astype(BF16)
        zb_ref[1, rows, :] = zo.astype(BF16)
        alt = alt_ref[rows, :]
        return (jnp.sum(ze * alt, axis=0, keepdims=True), jnp.sum(zo * alt, axis=0, keepdims=True))

    def spectrum_product(kra_ref, kia_ref, krb_ref, kib_ref):
        zbe = zb_ref[0]
        zbo = zb_ref[1]
        for _, rows in chunks:
            ur = _dot(ce_ref[rows, :], zbe)
            ui = _dot(se_ref[rows, :], zbe)
            vr = _dot(co_ref[rows, :], zbo)
            vi = _dot(so_ref[rows, :], zbo)
            xra, xia, xrb, xib = ur + vr, ui + vi, ur - vr, vi - ui
            kra, kia, krb, kib = kra_ref[rows, :], kia_ref[rows, :], krb_ref[rows, :], kib_ref[rows, :]
            pra = xra * kra - xia * kia
            pia = xra * kia + xia * kra
            prb = xrb * krb - xib * kib
            pib = xrb * kib + xib * krb
            p_ref[0, rows, :] = (pra + prb).astype(BF16)
            p_ref[1, rows, :] = (pia - pib).astype(BF16)
            p_ref[2, rows, :] = (pra - prb).astype(BF16)
            p_ref[3, rows, :] = (pia + pib).astype(BF16)

    def conv_rows(rows, xh, kh_ref, hb_ref):
        xrh, xih = xh
        krh, kih = kh_ref[0:1, :], kh_ref[1:2, :]
        prh = xrh * krh - xih * kih
        pih = xrh * kih + xih * krh
        alt = alt_ref[rows, :]
        bias = hb_ref[...]
        ye = (_dot(ce_ref[rows, :], p_ref[0]) + _dot(se_ref[rows, :], p_ref[1])
              + alt * prh + zf_ref[0, rows, :] * bias)
        yo = (_dot(cot_ref[rows, :], p_ref[2]) + _dot(sot_ref[rows, :], p_ref[3])
              + alt * pih + zf_ref[1, rows, :] * bias)
        return ye, yo

    def add2(a, b):
        return (a[0] + b[0], a[1] + b[1])

    xh = (0.0, 0.0)
    for r0, rows in chunks:
        ze, zo = _short_conv_rows(v_ref, il_ref, wv_ref, bv_ref, r0, rc)
        xh = add2(xh, put_z(ze, zo, rows))
    spectrum_product(kra0_ref, kia0_ref, krb0_ref, kib0_ref)
    xh1 = (0.0, 0.0)
    for r0, rows in chunks:
        ye, yo = conv_rows(rows, xh, kh0_ref, hb0_ref)
        ge, go = _short_conv_rows(x1_ref, il_ref, w1_ref, b1_ref, r0, rc)
        xh1 = add2(xh1, put_z(ge * ye, go * yo, rows))
    spectrum_product(kra1_ref, kia1_ref, krb1_ref, kib1_ref)
    for r0, rows in chunks:
        ye, yo = conv_rows(rows, xh1, kh1_ref, hb1_ref)
        ge, go = _short_conv_rows(x2_ref, il_ref, w2_ref, b2_ref, r0, rc)
        oe, oo = ge * ye, go * yo
        for c in range(il_ref.shape[0]):
            lanes = slice(c * LANES, (c + 1) * LANES)
            il_ref[c, pl.ds(0, rc, stride=2), :] = oe[:, lanes]
            il_ref[c, pl.ds(1, rc, stride=2), :] = oo[:, lanes]
            o_ref[0, 2 * r0:2 * (r0 + rc), lanes] = il_ref[c].astype(o_ref.dtype)


def _hyena(p3, conv_w, conv_b, spectra, hbias, alt, tabs, n_ch, ct, to_cast):
    B, L, _ = p3.shape
    H = L // 2
    nct = n_ch // ct
    rc = _pick(H, 512)
    kra, kia, krb, kib, kh = spectra
    hb = hbias.reshape(1, HYENA_ORDER * n_ch).astype(F32)
    cb = conv_b.reshape(1, -1).astype(F32)
    cw = conv_w.astype(F32)

    def pspec(g):
        return pl.BlockSpec((1, L, ct), lambda j, b, g=g: (b, 0, g * nct + j))

    def rowspec(rows, g):
        return pl.BlockSpec((rows, ct), lambda j, b, g=g: (0, g * nct + j))

    def specspec(g):
        return _resident((H, ct), lambda j, b, g=g: (0, g * nct + j))

    steps = nct * B
    cast_specs = []
    for w in to_cast:
        slab = w.shape[0] // steps
        assert slab * steps == w.shape[0] and slab % BF16_ROWS == 0, w.shape
        cast_specs.append(pl.BlockSpec((slab, w.shape[1]), lambda j, b: (j * B + b, 0)))

    full = lambda j, b: (0, 0)
    table = _resident((H, H), full)
    return pl.pallas_call(
        functools.partial(_hyena_kernel, rc=rc, n_cast=len(to_cast)),
        grid=(nct, B),
        in_specs=[
            pspec(0), pspec(1), pspec(2),
            rowspec(3, 0), rowspec(3, 1), rowspec(3, 2),
            rowspec(1, 0), rowspec(1, 1), rowspec(1, 2),
            specspec(0), specspec(0), specspec(0), specspec(0), rowspec(2, 0),
            specspec(1), specspec(1), specspec(1), specspec(1), rowspec(2, 1),
            rowspec(1, 0), rowspec(1, 1),
            _resident((H, 1), full),
            table, table, table, table, table, table,
        ] + cast_specs,
        out_specs=[pl.BlockSpec((1, L, ct), lambda j, b: (b, 0, j))] + cast_specs,
        out_shape=[jax.ShapeDtypeStruct((B, L, n_ch), BF16)]
                  + [jax.ShapeDtypeStruct(w.shape, BF16) for w in to_cast],
        scratch_shapes=[pltpu.VMEM((2, H, ct), F32), pltpu.VMEM((2, H, ct), BF16),
                        pltpu.VMEM((4, H, ct), BF16), pltpu.VMEM((ct // LANES, 2 * rc, LANES), F32)],
        name="hyena",
        compiler_params=_params(58, 2),
    )(p3, p3, p3, cw, cw, cw, cb, cb, cb,
      kra, kia, krb, kib, kh, kra, kia, krb, kib, kh, hb, hb, alt, *tabs, *to_cast)


def _gelu(x):
    return 0.5 * x * (1.0 + lax.erf(x * math.sqrt(0.5)))


def _sgu_kernel(u_ref, v_ref, g_ref, b_ref, ws_ref, bs_ref, o_ref, *, heads, chunk):
    u = _gelu(u_ref[...].astype(F32))
    v = _gelu(v_ref[...].astype(F32))
    mu = jnp.mean(v, axis=-1, keepdims=True)
    vc = v - mu
    var = jnp.mean(vc * vc, axis=-1, keepdims=True)
    vn = (vc * lax.rsqrt(var + EPS) * g_ref[...] + b_ref[...]).astype(BF16)
    tm, ds = vn.shape
    hd = ds // heads
    for n in range(tm // chunk):
        rows = slice(n * chunk, (n + 1) * chunk)
        for h in range(heads):
            cols = slice(h * hd, (h + 1) * hd)
            s = _dot(ws_ref[h], vn[rows, cols]) + bs_ref[:, h:h + 1]
            o_ref[rows, cols] = (u[rows, cols] * s).astype(o_ref.dtype)


def _sgu(p2, ln_g, ln_b, w_s, b_s, col0, ds, tm):
    M = p2.shape[0]
    heads, chunk, _ = w_s.shape
    cb = col0 // ds
    full2 = lambda i: (0, 0)
    return pl.pallas_call(
        functools.partial(_sgu_kernel, heads=heads, chunk=chunk),
        grid=(M // tm,),
        in_specs=[
            pl.BlockSpec((tm, ds), lambda i: (i, cb)),
            pl.BlockSpec((tm, ds), lambda i: (i, cb + 1)),
            _resident((1, ds), full2),
            _resident((1, ds), full2),
            _resident((heads, chunk, chunk), lambda i: (0, 0, 0)),
            _resident((chunk, heads), full2),
        ],
        out_specs=pl.BlockSpec((tm, ds), lambda i: (i, 0)),
        out_shape=jax.ShapeDtypeStruct((M, ds), BF16),
        name="sgu",
        compiler_params=_params(48, 1),
    )(p2, p2, ln_g.reshape(1, ds).astype(F32), ln_b.reshape(1, ds).astype(F32),
      w_s.astype(BF16), b_s.T.astype(F32))


def _merge_kernel(x_ref, yh_ref, ys_ref, gh0_ref, gh1_ref, gs0_ref, gs1_ref,
                  wbh_ref, wbs_ref, wo_ref, o_ref):
    gh = jnp.concatenate([gh0_ref[...], gh1_ref[...]], axis=1).astype(F32)
    gs = jnp.concatenate([gs0_ref[...], gs1_ref[...]], axis=1).astype(F32)
    a = _dot(yh_ref[...], wbh_ref[...]) * gh + _dot(ys_ref[...], wbs_ref[...]) * gs
    o_ref[...] = x_ref[...] + _dot(a.astype(BF16), wo_ref[...])


def _merge(x2, yh, ys, proj, gate_col, wbh, wbs, wo, tm):
    M, D = x2.shape
    gw = D // 2
    gb = gate_col // gw
    assert gate_col % gw == 0
    full = lambda i: (0, 0)

    def gate(c):
        return pl.BlockSpec((tm, gw), lambda i, c=c: (i, gb + c))

    return pl.pallas_call(
        _merge_kernel,
        grid=(M // tm,),
        in_specs=[
            pl.BlockSpec((tm, D), lambda i: (i, 0)),
            pl.BlockSpec((tm, yh.shape[1]), lambda i: (i, 0)),
            pl.BlockSpec((tm, ys.shape[1]), lambda i: (i, 0)),
            gate(0), gate(1), gate(2), gate(3),
            _resident(wbh.shape, full),
            _resident(wbs.shape, full),
            _resident(wo.shape, full),
        ],
        out_specs=pl.BlockSpec((tm, D), lambda i: (i, 0)),
        out_shape=jax.ShapeDtypeStruct((M, D), F32),
        name="merge",
        compiler_params=_params(56, 1),
    )(x2, yh, ys, proj, proj, proj, proj, wbh, wbs, wo)


def _ffn_kernel(x_ref, g_ref, wg_ref, wu_ref, wo_ref, gf_ref, o_ref, h_ref, *, final_norm):
    f = pl.program_id(1)

    @pl.when(f == 0)
    def _():
        x = x_ref[...]
        h_ref[...] = _rms(x, g_ref[...]).astype(BF16)
        o_ref[...] = x

    h = h_ref[...]
    gate = _dot(h, wg_ref[...])
    up = _dot(h, wu_ref[...])
    a = (gate * jax.nn.sigmoid(gate) * up).astype(BF16)
    o_ref[...] += _dot(a, wo_ref[...])

    if final_norm:
        @pl.when(f == pl.num_programs(1) - 1)
        def _():
            o_ref[...] = _rms(o_ref[...], gf_ref[...])


def _ffn(x2, g, w_in, w_out, g_final, final_norm, tm, tf):
    M, D = x2.shape
    FF = w_out.shape[0]
    nf = FF // tf
    full = lambda i, f: (0, 0)
    return pl.pallas_call(
        functools.partial(_ffn_kernel, final_norm=final_norm),
        grid=(M // tm, nf),
        in_specs=[
            pl.BlockSpec((tm, D), lambda i, f: (i, 0)),
            _resident((1, D), full),
            pl.BlockSpec((D, tf), lambda i, f: (0, f)),
            pl.BlockSpec((D, tf), lambda i, f: (0, nf + f)),
            pl.BlockSpec((tf, D), lambda i, f: (f, 0)),
            _resident((1, D), full),
        ],
        out_specs=pl.BlockSpec((tm, D), lambda i, f: (i, 0)),
        out_shape=jax.ShapeDtypeStruct((M, D), F32),
        scratch_shapes=[pltpu.VMEM((tm, D), BF16)],
        name="ffn",
        compiler_params=_params(60, 2),
    )(x2, g, w_in, w_in, w_out, g_final)


def _transform_tables(L):
    H = L // 2
    k = np.arange(H, dtype=np.int64)[:, None]
    m = np.arange(H, dtype=np.int64)[None, :]

    def tab(fn, idx):
        return (fn((idx % (2 * L)) * (np.pi / L)).astype(np.float32)).astype(BF16)

    ce, se = tab(np.cos, k * (2 * m)), tab(np.sin, k * (2 * m))
    co, so = tab(np.cos, k * (2 * m + 1)), tab(np.sin, k * (2 * m + 1))
    tabs = tuple(jnp.asarray(t) for t in (ce, se, co, so, np.ascontiguousarray(co.T), np.ascontiguousarray(so.T)))
    alt = jnp.asarray((1.0 - 2.0 * (np.arange(H) % 2)).astype(np.float32)[:, None])
    return tabs, alt


def _position_features(L, emb):
    t = jnp.linspace(0.0, 1.0, L, dtype=F32)[:, None]
    bands = (emb - 1) // 2
    w = 2.0 * math.pi * jnp.arange(L, dtype=F32)[:, None] / L
    f = jnp.linspace(1e-4, bands - 1, bands, dtype=F32)[None, :]
    return jnp.concatenate([t, jnp.cos(f * w), -jnp.sin(f * w)], axis=-1), t


def _pick(n, pref):
    t = min(n, pref)
    while n % t:
        t -= 1
    return t


def kernel(x, norm_mix_g, w_in, short_conv_w, short_conv_b, filt_w1, filt_b1, filt_w2, filt_b2, filt_w3, filt_b3, filt_freq, filt_w4, hyena_bias, sgu_ln_g, sgu_ln_b, sgu_w_s, sgu_b_s, w_branch_hyena, w_branch_sgu, w_out, norm_ffn_g, w_ffn_in, w_ffn_out, norm_final_g):
    B, L, D = x.shape
    depth = w_in.shape[0]
    n_ch = w_branch_hyena.shape[1]
    ds = w_branch_sgu.shape[1]
    assert n_ch == ds and 2 * n_ch == D and L % 2 == 0 and depth >= 1
    n_mix = 3 * n_ch + 2 * ds
    M = B * L
    H = L // 2

    tabs, alt = _transform_tables(L)
    zfeat, t_col = _position_features(L, filt_w1.shape[1])
    max_decay = math.log(DECAY_TARGET) / FAST_DECAY
    min_decay = math.log(DECAY_TARGET) / SLOW_DECAY
    absdelta = jnp.abs(jnp.linspace(min_decay, max_decay, n_ch, dtype=F32))[None, :]

    ct = _pick(n_ch, 256)
    tm = _pick(M, 1024)
    tn = LANES * _pick(w_in.shape[2] // LANES, 1536 // LANES)

    x2 = x.reshape(M, D)
    for l in range(depth):
        spectra, w_in_bf16 = _hyena_filters(zfeat, t_col, absdelta, alt, tabs,
                                            filt_w1[l], filt_b1[l], filt_w2[l], filt_b2[l],
                                            filt_w3[l], filt_b3[l], filt_freq[l], filt_w4[l], ct, w_in[l])
        proj = _proj(x2, norm_mix_g[l].reshape(1, D), w_in_bf16, n_mix, tm, tn)
        y_hy, wbh, wbs, wo, wfi, wfo = _hyena(
            proj.reshape(B, L, proj.shape[1]), short_conv_w[l], short_conv_b[l], spectra,
            hyena_bias[l], alt, tabs, n_ch, ct,
            [w_branch_hyena[l], w_branch_sgu[l], w_out[l], w_ffn_in[l], w_ffn_out[l]])
        y_sgu = _sgu(proj, sgu_ln_g[l], sgu_ln_b[l], sgu_w_s[l], sgu_b_s[l], 3 * n_ch, ds, tm)
        x2 = _merge(x2, y_hy.reshape(M, n_ch), y_sgu, proj, n_mix, wbh, wbs, wo, _pick(M, 512))
        last = l == depth - 1
        x2 = _ffn(x2, norm_ffn_g[l].reshape(1, D), wfi, wfo, norm_final_g.reshape(1, D), last,
                  _pick(M, 1024), _pick(w_ffn_out.shape[1], 512))
    return x2.reshape(B, L, D)
```

```python
import functools
import math

import numpy as np
import jax
import jax.numpy as jnp
from jax import lax
from jax.experimental import pallas as pl
from jax.experimental.pallas import tpu as pltpu

F32 = jnp.float32
BF16 = jnp.bfloat16

EPS = 1e-6
FAST_DECAY = 0.3
SLOW_DECAY = 1.5
DECAY_TARGET = 1e-2
MOD_SHIFT = 0.0
N_DIR = 2
HYENA_ORDER = 2
LANES = 128
BF16_ROWS = 16
MIB = 1024 * 1024


def _params(vmem_mib, n_grid):
    return pltpu.CompilerParams(
        dimension_semantics=("arbitrary",) * n_grid,
        vmem_limit_bytes=vmem_mib * MIB)


def _resident(block_shape, index_map):
    return pl.BlockSpec(block_shape, index_map, pipeline_mode=pl.Buffered(1))


def _rms(x, g):
    ms = jnp.mean(x * x, axis=-1, keepdims=True)
    return x * lax.rsqrt(ms + EPS) * g


def _dot(a, b):
    return jnp.dot(a, b, preferred_element_type=F32)


def _sigmoid(x):
    return 0.5 * jnp.tanh(0.5 * x) + 0.5


def _filter_feat_kernel(z_ref, w1_ref, b1_ref, w2_ref, b2_ref, w3_ref, b3_ref, fr_ref, o_ref):
    hp = lax.Precision.HIGHEST
    fr = fr_ref[...]
    h = jnp.sin(fr * (jnp.dot(z_ref[...], w1_ref[...], precision=hp, preferred_element_type=F32) + b1_ref[...]))
    h = jnp.sin(fr * (jnp.dot(h, w2_ref[...], precision=hp, preferred_element_type=F32) + b2_ref[...]))
    h = jnp.sin(fr * (jnp.dot(h, w3_ref[...], precision=hp, preferred_element_type=F32) + b3_ref[...]))
    o_ref[...] = h


def _filter_spec_kernel(fe_ref, fo_ref, w4f_ref, w4b_ref, te_ref, to_ref, ad_ref, alt_ref,
                        ce_ref, se_ref, co_ref, so_ref, wsrc_ref,
                        kra_ref, kia_ref, krb_ref, kib_ref, kh_ref, wdst_ref):
    wdst_ref[...] = wsrc_ref[...].astype(wdst_ref.dtype)
    hp = lax.Precision.HIGHEST
    H = fe_ref.shape[0]
    inv_n = 1.0 / (4 * H)
    ad = ad_ref[...]

    def taps(f_ref, t_ref):
        f = f_ref[...]
        decay = jnp.exp(-t_ref[...] * ad) + MOD_SHIFT
        return (jnp.dot(f, w4f_ref[...], precision=hp, preferred_element_type=F32) * decay,
                jnp.dot(f, w4b_ref[...], precision=hp, preferred_element_type=F32) * decay)

    hf_e, hb_e = taps(fe_ref, te_ref)
    hf_o, hb_o = taps(fo_ref, to_ref)
    row = lax.broadcasted_iota(jnp.int32, hf_e.shape, 0)
    hb_e = jnp.where(row == 0, 0.0, hb_e)

    def colsum(a):
        return jnp.sum(a, axis=0, keepdims=True)

    norm = colsum(jnp.abs(hf_e)) + colsum(jnp.abs(hf_o)) + colsum(jnp.abs(hb_e)) + colsum(jnp.abs(hb_o))
    inv = 1.0 / norm
    ev_e, ev_o = (hf_e + hb_e) * inv, (hf_o + hb_o) * inv
    od_e, od_o = (hf_e - hb_e) * inv, (hf_o - hb_o) * inv
    ur = _dot(ce_ref[...], ev_e.astype(BF16))
    vr = _dot(co_ref[...], ev_o.astype(BF16))
    ui = _dot(se_ref[...], od_e.astype(BF16))
    vi = _dot(so_ref[...], od_o.astype(BF16))
    w = jnp.where(row[:, :1] == 0, inv_n, 2.0 * inv_n)
    kra_ref[...] = (ur + vr) * w
    krb_ref[...] = (ur - vr) * w
    kia_ref[...] = (ui + vi) * w
    kib_ref[...] = (vi - ui) * w
    alt = alt_ref[...]
    kh_ref[0:1, :] = colsum(ev_e * alt) * (2.0 * inv_n)
    kh_ref[1:2, :] = colsum(od_o * alt) * (2.0 * inv_n)


def _hyena_filters(zfeat, t_col, absdelta, alt, tabs, w1, b1, w2, b2, w3, b3, freq, w4, tc, w_cast):
    L = zfeat.shape[0]
    H = L // 2
    c2 = w4.shape[1] // N_DIR
    n_ch = c2 // HYENA_ORDER
    ce, se, co, so = tabs[:4]

    def pad2(a, r, c):
        return jnp.pad(a.astype(F32), ((0, r - a.shape[0]), (0, c - a.shape[1])))

    feat = pl.pallas_call(
        _filter_feat_kernel,
        out_shape=jax.ShapeDtypeStruct((L, LANES), F32),
        name="filter_feat",
        compiler_params=_params(32, 0),
    )(pad2(jnp.concatenate([zfeat[0::2], zfeat[1::2]], axis=0), L, LANES),
      pad2(w1, LANES, LANES), pad2(b1[None], 1, LANES),
      pad2(w2, LANES, LANES), pad2(b2[None], 1, LANES),
      pad2(w3, LANES, LANES), pad2(b3[None], 1, LANES), pad2(freq[None], 1, LANES))

    w4p = pad2(w4, LANES, w4.shape[1])
    nt = c2 // tc
    nct = n_ch // tc
    full = lambda j: (0, 0)
    spec = jax.ShapeDtypeStruct((H, c2), F32)
    tile = pl.BlockSpec((H, tc), lambda j: (0, j))
    slab = w_cast.shape[0] // nt
    assert slab * nt == w_cast.shape[0] and slab % BF16_ROWS == 0
    cast_spec = pl.BlockSpec((slab, w_cast.shape[1]), lambda j: (j, 0))
    *spectra, w_bf16 = pl.pallas_call(
        _filter_spec_kernel,
        grid=(nt,),
        in_specs=[
            _resident((H, LANES), full),
            _resident((H, LANES), lambda j: (1, 0)),
            pl.BlockSpec((LANES, tc), lambda j: (0, j)),
            pl.BlockSpec((LANES, tc), lambda j: (0, nt + j)),
            _resident((H, 1), full),
            _resident((H, 1), full),
            pl.BlockSpec((1, tc), lambda j: (0, j % nct)),
            _resident((H, 1), full),
            _resident((H, H), full), _resident((H, H), full),
            _resident((H, H), full), _resident((H, H), full),
            cast_spec,
        ],
        out_specs=[tile, tile, tile, tile, pl.BlockSpec((2, tc), lambda j: (0, j)), cast_spec],
        out_shape=[spec, spec, spec, spec, jax.ShapeDtypeStruct((2, c2), F32),
                   jax.ShapeDtypeStruct(w_cast.shape, BF16)],
        name="filter_spec",
        compiler_params=_params(58, 1),
    )(feat, feat, w4p, w4p, t_col[0::2], t_col[1::2], absdelta, alt, ce, se, co, so, w_cast)
    return spectra, w_bf16


def _proj_kernel(x_ref, g_ref, w_ref, o_ref, h_ref, *, n_mix):
    j = pl.program_id(1)

    @pl.when(j == 0)
    def _():
        h_ref[...] = _rms(x_ref[...], g_ref[...]).astype(BF16)

    tn = o_ref.shape[1]
    has_gates = (j + 1) * tn > n_mix

    @pl.when(jnp.logical_not(has_gates))
    def _():
        o_ref[...] = _dot(h_ref[...], w_ref[...]).astype(BF16)

    @pl.when(has_gates)
    def _():
        acc = _dot(h_ref[...], w_ref[...])
        col = j * tn + lax.broadcasted_iota(jnp.int32, acc.shape, 1)
        o_ref[...] = jnp.where(col >= n_mix, _sigmoid(acc), acc).astype(BF16)


def _proj(x2, g, w, n_mix, tm, tn):
    M, D = x2.shape
    E = w.shape[1]
    return pl.pallas_call(
        functools.partial(_proj_kernel, n_mix=n_mix),
        grid=(M // tm, E // tn),
        in_specs=[
            pl.BlockSpec((tm, D), lambda i, j: (i, 0)),
            _resident((1, D), lambda i, j: (0, 0)),
            pl.BlockSpec((D, tn), lambda i, j: (0, j)),
        ],
        out_specs=pl.BlockSpec((tm, tn), lambda i, j: (i, j)),
        out_shape=jax.ShapeDtypeStruct((M, E), BF16),
        scratch_shapes=[pltpu.VMEM((tm, D), BF16)],
        name="proj",
        compiler_params=_params(56, 2),
    )(x2, g, w)


def _short_conv_rows(p_ref, il_ref, w_ref, b_ref, r0, rc):
    L = p_ref.shape[1]
    t0, t1 = 2 * r0, 2 * (r0 + rc)
    pe, po = [], []
    for c in range(il_ref.shape[0]):
        il_ref[c] = p_ref[0, t0:t1, c * LANES:(c + 1) * LANES].astype(F32)
        pe.append(il_ref[c, pl.ds(0, rc, stride=2), :])
        po.append(il_ref[c, pl.ds(1, rc, stride=2), :])
    pe = jnp.concatenate(pe, axis=1)
    po = jnp.concatenate(po, axis=1)
    row = lax.broadcasted_iota(jnp.int32, pe.shape, 0)
    if t0 == 0:
        before = 0.0
    else:
        before = p_ref[0, t0 - BF16_ROWS:t0, :].astype(F32)[BF16_ROWS - 1:BF16_ROWS, :]
    if t1 == L:
        after = 0.0
    else:
        after = p_ref[0, t1:t1 + BF16_ROWS, :].astype(F32)[0:1, :]
    po_prev = jnp.where(row == 0, before, pltpu.roll(po, 1, axis=0))
    pe_next = jnp.where(row == rc - 1, after, pltpu.roll(pe, rc - 1, axis=0))
    w0, w1, w2, b = w_ref[0:1, :], w_ref[1:2, :], w_ref[2:3, :], b_ref[...]
    return (po_prev * w0 + pe * w1 + po * w2 + b,
            pe * w0 + po * w1 + pe_next * w2 + b)


def _hyena_kernel(v_ref, x1_ref, x2_ref,
                  wv_ref, w1_ref, w2_ref, bv_ref, b1_ref, b2_ref,
                  kra0_ref, kia0_ref, krb0_ref, kib0_ref, kh0_ref,
                  kra1_ref, kia1_ref, krb1_ref, kib1_ref, kh1_ref, hb0_ref, hb1_ref,
                  alt_ref, ce_ref, se_ref, co_ref, so_ref, cot_ref, sot_ref,
                  *refs, rc, n_cast):
    o_ref = refs[n_cast]
    zf_ref, zb_ref, p_ref, il_ref = refs[2 * n_cast + 1:]
    for src, dst in zip(refs[:n_cast], refs[n_cast + 1:2 * n_cast + 1]):
        dst[...] = src[...].astype(dst.dtype)
    H = ce_ref.shape[0]
    chunks = [(r0, slice(r0, r0 + rc)) for r0 in range(0, H, rc)]

    def put_z(ze, zo, rows):
        zf_ref[0, rows, :] = ze
        zf_ref[1, rows, :] = zo
        zb_ref[0, rows, :] = ze.astype(BF16)
        zb_ref[1, rows, :] = zo.astype(BF16)
        alt = alt_ref[rows, :]
        return (jnp.sum(ze * alt, axis=0, keepdims=True), jnp.sum(zo * alt, axis=0, keepdims=True))

    def spectrum_product(kra_ref, kia_ref, krb_ref, kib_ref):
        zbe = zb_ref[0]
        zbo = zb_ref[1]
        for _, rows in chunks:
            ur = _dot(ce_ref[rows, :], zbe)
            ui = _dot(se_ref[rows, :], zbe)
            vr = _dot(co_ref[rows, :], zbo)
            vi = _dot(so_ref[rows, :], zbo)
            xra, xia, xrb, xib = ur + vr, ui + vi, ur - vr, vi - ui
            kra, kia, krb, kib = kra_ref[rows, :], kia_ref[rows, :], krb_ref[rows, :], kib_ref[rows, :]
            pra = xra * kra - xia * kia
            pia = xra * kia + xia * kra
            prb = xrb * krb - xib * kib
            pib = xrb * kib + xib * krb
            p_ref[0, rows, :] = (pra + prb).astype(BF16)
            p_ref[1, rows, :] = (pia - pib).astype(BF16)
            p_ref[2, rows, :] = (pra - prb).astype(BF16)
            p_ref[3, rows, :] = (pia + pib).astype(BF16)

    def conv_rows(rows, xh, kh_ref, hb_ref):
        xrh, xih = xh
        krh, kih = kh_ref[0:1, :], kh_ref[1:2, :]
        prh = xrh * krh - xih * kih
        pih = xrh * kih + xih * krh
        alt = alt_ref[rows, :]
        bias = hb_ref[...]
        ye = (_dot(ce_ref[rows, :], p_ref[0]) + _dot(se_ref[rows, :], p_ref[1])
              + alt * prh + zf_ref[0, rows, :] * bias)
        yo = (_dot(cot_ref[rows, :], p_ref[2]) + _dot(sot_ref[rows, :], p_ref[3])
              + alt * pih + zf_ref[1, rows, :] * bias)
        return ye, yo

    def add2(a, b):
        return (a[0] + b[0], a[1] + b[1])

    xh = (0.0, 0.0)
    for r0, rows in chunks:
        ze, zo = _short_conv_rows(v_ref, il_ref, wv_ref, bv_ref, r0, rc)
        xh = add2(xh, put_z(ze, zo, rows))
    spectrum_product(kra0_ref, kia0_ref, krb0_ref, kib0_ref)
    xh1 = (0.0, 0.0)
    for r0, rows in chunks:
        ye, yo = conv_rows(rows, xh, kh0_ref, hb0_ref)
        ge, go = _short_conv_rows(x1_ref, il_ref, w1_ref, b1_ref, r0, rc)
        xh1 = add2(xh1, put_z(ge * ye, go * yo, rows))
    spectrum_product(kra1_ref, kia1_ref, krb1_ref, kib1_ref)
    for r0, rows in chunks:
        ye, yo = conv_rows(rows, xh1, kh1_ref, hb1_ref)
        ge, go = _short_conv_rows(x2_ref, il_ref, w2_ref, b2_ref, r0, rc)
        oe, oo = ge * ye, go * yo
        for c in range(il_ref.shape[0]):
            lanes = slice(c * LANES, (c + 1) * LANES)
            il_ref[c, pl.ds(0, rc, stride=2), :] = oe[:, lanes]
            il_ref[c, pl.ds(1, rc, stride=2), :] = oo[:, lanes]
            o_ref[0, 2 * r0:2 * (r0 + rc), lanes] = il_ref[c].astype(o_ref.dtype)


def _hyena(p3, conv_w, conv_b, spectra, hbias, alt, tabs, n_ch, ct, to_cast):
    B, L, _ = p3.shape
    H = L // 2
    nct = n_ch // ct
    rc = _pick(H, 512)
    kra, kia, krb, kib, kh = spectra
    hb = hbias.reshape(1, HYENA_ORDER * n_ch).astype(F32)
    cb = conv_b.reshape(1, -1).astype(F32)
    cw = conv_w.astype(F32)

    def pspec(g):
        return pl.BlockSpec((1, L, ct), lambda j, b, g=g: (b, 0, g * nct + j))

    def rowspec(rows, g):
        return pl.BlockSpec((rows, ct), lambda j, b, g=g: (0, g * nct + j))

    def specspec(g):
        return _resident((H, ct), lambda j, b, g=g: (0, g * nct + j))

    steps = nct * B
    cast_specs = []
    for w in to_cast:
        slab = w.shape[0] // steps
        assert slab * steps == w.shape[0] and slab % BF16_ROWS == 0, w.shape
        cast_specs.append(pl.BlockSpec((slab, w.shape[1]), lambda j, b: (j * B + b, 0)))

    full = lambda j, b: (0, 0)
    table = _resident((H, H), full)
    return pl.pallas_call(
        functools.partial(_hyena_kernel, rc=rc, n_cast=len(to_cast)),
        grid=(nct, B),
        in_specs=[
            pspec(0), pspec(1), pspec(2),
            rowspec(3, 0), rowspec(3, 1), rowspec(3, 2),
            rowspec(1, 0), rowspec(1, 1), rowspec(1, 2),
            specspec(0), specspec(0), specspec(0), specspec(0), rowspec(2, 0),
            specspec(1), specspec(1), specspec(1), specspec(1), rowspec(2, 1),
            rowspec(1, 0), rowspec(1, 1),
            _resident((H, 1), full),
            table, table, table, table, table, table,
        ] + cast_specs,
        out_specs=[pl.BlockSpec((1, L, ct), lambda j, b: (b, 0, j))] + cast_specs,
        out_shape=[jax.ShapeDtypeStruct((B, L, n_ch), BF16)]
                  + [jax.ShapeDtypeStruct(w.shape, BF16) for w in to_cast],
        scratch_shapes=[pltpu.VMEM((2, H, ct), F32), pltpu.VMEM((2, H, ct), BF16),
                        pltpu.VMEM((4, H, ct), BF16), pltpu.VMEM((ct // LANES, 2 * rc, LANES), F32)],
        name="hyena",
        compiler_params=_params(58, 2),
    )(p3, p3, p3, cw, cw, cw, cb, cb, cb,
      kra, kia, krb, kib, kh, kra, kia, krb, kib, kh, hb, hb, alt, *tabs, *to_cast)


def _gelu(x):
    return 0.5 * x * (1.0 + lax.erf(x * math.sqrt(0.5)))


def _sgu_kernel(u_ref, v_ref, g_ref, b_ref, ws_ref, bs_ref, o_ref, *, heads, chunk):
    u = _gelu(u_ref[...].astype(F32))
    v = _gelu(v_ref[...].astype(F32))
    mu = jnp.mean(v, axis=-1, keepdims=True)
    vc = v - mu
    var = jnp.mean(vc * vc, axis=-1, keepdims=True)
    vn = (vc * lax.rsqrt(var + EPS) * g_ref[...] + b_ref[...]).astype(BF16)
    tm, ds = vn.shape
    hd = ds // heads
    for n in range(tm // chunk):
        rows = slice(n * chunk, (n + 1) * chunk)
        for h in range(heads):
            cols = slice(h * hd, (h + 1) * hd)
            s = _dot(ws_ref[h], vn[rows, cols]) + bs_ref[:, h:h + 1]
            o_ref[rows, cols] = (u[rows, cols] * s).astype(o_ref.dtype)


def _sgu(p2, ln_g, ln_b, w_s, b_s, col0, ds, tm):
    M = p2.shape[0]
    heads, chunk, _ = w_s.shape
    cb = col0 // ds
    full2 = lambda i: (0, 0)
    return pl.pallas_call(
        functools.partial(_sgu_kernel, heads=heads, chunk=chunk),
        grid=(M // tm,),
        in_specs=[
            pl.BlockSpec((tm, ds), lambda i: (i, cb)),
            pl.BlockSpec((tm, ds), lambda i: (i, cb + 1)),
            _resident((1, ds), full2),
            _resident((1, ds), full2),
            _resident((heads, chunk, chunk), lambda i: (0, 0, 0)),
            _resident((chunk, heads), full2),
        ],
        out_specs=pl.BlockSpec((tm, ds), lambda i: (i, 0)),
        out_shape=jax.ShapeDtypeStruct((M, ds), BF16),
        name="sgu",
        compiler_params=_params(48, 1),
    )(p2, p2, ln_g.reshape(1, ds).astype(F32), ln_b.reshape(1, ds).astype(F32),
      w_s.astype(BF16), b_s.T.astype(F32))


def _merge_kernel(x_ref, yh_ref, ys_ref, gh0_ref, gh1_ref, gs0_ref, gs1_ref,
                  wbh_ref, wbs_ref, wo_ref, o_ref):
    gh = jnp.concatenate([gh0_ref[...], gh1_ref[...]], axis=1).astype(F32)
    gs = jnp.concatenate([gs0_ref[...], gs1_ref[...]], axis=1).astype(F32)
    a = _dot(yh_ref[...], wbh_ref[...]) * gh + _dot(ys_ref[...], wbs_ref[...]) * gs
    o_ref[...] = x_ref[...] + _dot(a.astype(BF16), wo_ref[...])


def _merge(x2, yh, ys, proj, gate_col, wbh, wbs, wo, tm):
    M, D = x2.shape
    gw = D // 2
    gb = gate_col // gw
    assert gate_col % gw == 0
    full = lambda i: (0, 0)

    def gate(c):
        return pl.BlockSpec((tm, gw), lambda i, c=c: (i, gb + c))

    return pl.pallas_call(
        _merge_kernel,
        grid=(M // tm,),
        in_specs=[
            pl.BlockSpec((tm, D), lambda i: (i, 0)),
            pl.BlockSpec((tm, yh.shape[1]), lambda i: (i, 0)),
            pl.BlockSpec((tm, ys.shape[1]), lambda i: (i, 0)),
            gate(0), gate(1), gate(2), gate(3),
            _resident(wbh.shape, full),
            _resident(wbs.shape, full),
            _resident(wo.shape, full),
        ],
        out_specs=pl.BlockSpec((tm, D), lambda i: (i, 0)),
        out_shape=jax.ShapeDtypeStruct((M, D), F32),
        name="merge",
        compiler_params=_params(56, 1),
    )(x2, yh, ys, proj, proj, proj, proj, wbh, wbs, wo)


def _ffn_kernel(x_ref, g_ref, wg_ref, wu_ref, wo_ref, gf_ref, o_ref, h_ref, *, final_norm):
    f = pl.program_id(1)

    @pl.when(f == 0)
    def _():
        x = x_ref[...]
        h_ref[...] = _rms(x, g_ref[...]).astype(BF16)
        o_ref[...] = x

    h = h_ref[...]
    gate = _dot(h, wg_ref[...])
    up = _dot(h, wu_ref[...])
    a = (gate * _sigmoid(gate) * up).astype(BF16)
    o_ref[...] += _dot(a, wo_ref[...])

    if final_norm:
        @pl.when(f == pl.num_programs(1) - 1)
        def _():
            o_ref[...] = _rms(o_ref[...], gf_ref[...])


def _ffn(x2, g, w_in, w_out, g_final, final_norm, tm, tf):
    M, D = x2.shape
    FF = w_out.shape[0]
    nf = FF // tf
    full = lambda i, f: (0, 0)
    return pl.pallas_call(
        functools.partial(_ffn_kernel, final_norm=final_norm),
        grid=(M // tm, nf),
        in_specs=[
            pl.BlockSpec((tm, D), lambda i, f: (i, 0)),
            _resident((1, D), full),
            pl.BlockSpec((D, tf), lambda i, f: (0, f)),
            pl.BlockSpec((D, tf), lambda i, f: (0, nf + f)),
            pl.BlockSpec((tf, D), lambda i, f: (f, 0)),
            _resident((1, D), full),
        ],
        out_specs=pl.BlockSpec((tm, D), lambda i, f: (i, 0)),
        out_shape=jax.ShapeDtypeStruct((M, D), F32),
        scratch_shapes=[pltpu.VMEM((tm, D), BF16)],
        name="ffn",
        compiler_params=_params(60, 2),
    )(x2, g, w_in, w_in, w_out, g_final)


def _transform_tables(L):
    H = L // 2
    k = np.arange(H, dtype=np.int64)[:, None]
    m = np.arange(H, dtype=np.int64)[None, :]

    def tab(fn, idx):
        return (fn((idx % (2 * L)) * (np.pi / L)).astype(np.float32)).astype(BF16)

    ce, se = tab(np.cos, k * (2 * m)), tab(np.sin, k * (2 * m))
    co, so = tab(np.cos, k * (2 * m + 1)), tab(np.sin, k * (2 * m + 1))
    tabs = tuple(jnp.asarray(t) for t in (ce, se, co, so, np.ascontiguousarray(co.T), np.ascontiguousarray(so.T)))
    alt = jnp.asarray((1.0 - 2.0 * (np.arange(H) % 2)).astype(np.float32)[:, None])
    return tabs, alt


def _position_features(L, emb):
    t = jnp.linspace(0.0, 1.0, L, dtype=F32)[:, None]
    bands = (emb - 1) // 2
    w = 2.0 * math.pi * jnp.arange(L, dtype=F32)[:, None] / L
    f = jnp.linspace(1e-4, bands - 1, bands, dtype=F32)[None, :]
    return jnp.concatenate([t, jnp.cos(f * w), -jnp.sin(f * w)], axis=-1), t


def _pick(n, pref):
    t = min(n, pref)
    while n % t:
        t -= 1
    return t


def kernel(x, norm_mix_g, w_in, short_conv_w, short_conv_b, filt_w1, filt_b1, filt_w2, filt_b2, filt_w3, filt_b3, filt_freq, filt_w4, hyena_bias, sgu_ln_g, sgu_ln_b, sgu_w_s, sgu_b_s, w_branch_hyena, w_branch_sgu, w_out, norm_ffn_g, w_ffn_in, w_ffn_out, norm_final_g):
    B, L, D = x.shape
    depth = w_in.shape[0]
    n_ch = w_branch_hyena.shape[1]
    ds = w_branch_sgu.shape[1]
    assert n_ch == ds and 2 * n_ch == D and L % 2 == 0 and depth >= 1
    n_mix = 3 * n_ch + 2 * ds
    M = B * L
    H = L // 2

    tabs, alt = _transform_tables(L)
    zfeat, t_col = _position_features(L, filt_w1.shape[1])
    max_decay = math.log(DECAY_TARGET) / FAST_DECAY
    min_decay = math.log(DECAY_TARGET) / SLOW_DECAY
    absdelta = jnp.abs(jnp.linspace(min_decay, max_decay, n_ch, dtype=F32))[None, :]

    ct = _pick(n_ch, 256)
    tm = _pick(M, 1024)
    tn = LANES * _pick(w_in.shape[2] // LANES, 1536 // LANES)

    x2 = x.reshape(M, D)
    for l in range(depth):
        spectra, w_in_bf16 = _hyena_filters(zfeat, t_col, absdelta, alt, tabs,
                                            filt_w1[l], filt_b1[l], filt_w2[l], filt_b2[l],
                                            filt_w3[l], filt_b3[l], filt_freq[l], filt_w4[l], ct, w_in[l])
        proj = _proj(x2, norm_mix_g[l].reshape(1, D), w_in_bf16, n_mix, tm, tn)
        y_hy, wbh, wbs, wo, wfi, wfo = _hyena(
            proj.reshape(B, L, proj.shape[1]), short_conv_w[l], short_conv_b[l], spectra,
            hyena_bias[l], alt, tabs, n_ch, ct,
            [w_branch_hyena[l], w_branch_sgu[l], w_out[l], w_ffn_in[l], w_ffn_out[l]])
        y_sgu = _sgu(proj, sgu_ln_g[l], sgu_ln_b[l], sgu_w_s[l], sgu_b_s[l], 3 * n_ch, ds, tm)
        x2 = _merge(x2, y_hy.reshape(M, n_ch), y_sgu, proj, n_mix, wbh, wbs, wo, _pick(M, 512))
        last = l == depth - 1
        x2 = _ffn(x2, norm_ffn_g[l].reshape(1, D), wfi, wfo, norm_final_g.reshape(1, D), last,
                  _pick(M, 1024), _pick(w_ffn_out.shape[1], 512))
    return x2.reshape(B, L, D)
```

```python
import functools
import math

import numpy as np
import jax
import jax.numpy as jnp
from jax import lax
from jax.experimental import pallas as pl
from jax.experimental.pallas import tpu as pltpu

F32 = jnp.float32
BF16 = jnp.bfloat16

EPS = 1e-6
FAST_DECAY = 0.3
SLOW_DECAY = 1.5
DECAY_TARGET = 1e-2
MOD_SHIFT = 0.0
N_DIR = 2
HYENA_ORDER = 2
RADIX = 4
SQH = math.sqrt(0.5)
LANES = 128
BF16_ROWS = 16
MIB = 1024 * 1024


def _params(vmem_mib, n_grid):
    return pltpu.CompilerParams(
        dimension_semantics=("arbitrary",) * n_grid,
        vmem_limit_bytes=vmem_mib * MIB)


def _resident(block_shape, index_map):
    return pl.BlockSpec(block_shape, index_map, pipeline_mode=pl.Buffered(1))


def _rms(x, g):
    ms = jnp.mean(x * x, axis=-1, keepdims=True)
    return x * lax.rsqrt(ms + EPS) * g


def _dot(a, b):
    return jnp.dot(a, b, preferred_element_type=F32)


def _sigmoid(x):
    return 0.5 * jnp.tanh(0.5 * x) + 0.5


def _colsum(a):
    return jnp.sum(a, axis=0, keepdims=True)


def _cmul(xr, xi, kr, ki):
    return xr * kr - xi * ki, xr * ki + xi * kr


def _special_bins(t):
    d, s = SQH * (t[1] - t[3]), SQH * (t[1] + t[3])
    return (t[0] + d, t[2] + s), (t[0] - d, s - t[2])


def _filter_feat_kernel(z_ref, w1_ref, b1_ref, w2_ref, b2_ref, w3_ref, b3_ref, fr_ref, o_ref):
    hp = lax.Precision.HIGHEST
    fr = fr_ref[...]
    h = jnp.sin(fr * (jnp.dot(z_ref[...], w1_ref[...], precision=hp, preferred_element_type=F32) + b1_ref[...]))
    h = jnp.sin(fr * (jnp.dot(h, w2_ref[...], precision=hp, preferred_element_type=F32) + b2_ref[...]))
    h = jnp.sin(fr * (jnp.dot(h, w3_ref[...], precision=hp, preferred_element_type=F32) + b3_ref[...]))
    o_ref[...] = h


def _filter_spec_kernel(f_ref, w4f_ref, w4b_ref, t_ref, ad_ref, alt_ref, tab_ref, wsrc_ref,
                        sp_ref, kq_ref, wdst_ref):
    wdst_ref[...] = wsrc_ref[...].astype(wdst_ref.dtype)
    hp = lax.Precision.HIGHEST
    Q = f_ref.shape[1]
    inv_n = 1.0 / (2 * RADIX * Q)
    ad = ad_ref[...]
    row = lax.broadcasted_iota(jnp.int32, (Q, ad.shape[1]), 0)

    hf, hb = [], []
    for r in range(RADIX):
        f = f_ref[r]
        decay = jnp.exp(-t_ref[r] * ad) + MOD_SHIFT
        hf.append(jnp.dot(f, w4f_ref[...], precision=hp, preferred_element_type=F32) * decay)
        hb.append(jnp.dot(f, w4b_ref[...], precision=hp, preferred_element_type=F32) * decay)
    hb[0] = jnp.where(row == 0, 0.0, hb[0])
    norm = sum(_colsum(jnp.abs(h)) for h in hf + hb)
    inv = 1.0 / norm
    ev = [(hf[r] + hb[r]) * inv for r in range(RADIX)]
    od = [(hf[r] - hb[r]) * inv for r in range(RADIX)]
    evb = [e.astype(BF16) for e in ev]
    odb = [o.astype(BF16) for o in od]

    def cos_t(r, zb):
        return _dot(tab_ref[r], zb[r])

    def sin_t(r, zb):
        return _dot(tab_ref[RADIX + r], zb[r])

    a0, a2 = cos_t(0, evb), cos_t(2, evb)
    e, f_ = a0 + a2, a0 - a2
    g = cos_t(1, evb) + cos_t(3, evb)
    v = sin_t(1, evb) - sin_t(3, evb)
    b0, b2 = sin_t(0, odb), sin_t(2, odb)
    p, q = b0 + b2, b0 - b2
    u = sin_t(1, odb) + sin_t(3, odb)
    h = cos_t(1, odb) - cos_t(3, odb)
    first = row[:, :1] == 0
    w_end = jnp.where(first, inv_n, 2.0 * inv_n)
    w_dup = jnp.where(first, 0.0, 2.0 * inv_n)
    w_mid = 2.0 * inv_n
    for s, (kr, ki, w) in enumerate([(e + g, p + u, w_end), (e - g, u - p, w_end),
                                     (f_ + v, h - q, w_mid), (f_ - v, q + h, w_dup)]):
        sp_ref[s] = kr * w
        sp_ref[RADIX + s] = ki * w
    alt = alt_ref[...]
    (krq, _), (kr3, _) = _special_bins([_colsum(x * alt) for x in ev])
    (_, kiq), (_, ki3) = _special_bins([_colsum(x * alt) for x in od])
    for i, kval in enumerate((krq, kiq, kr3, ki3)):
        kq_ref[i:i + 1, :] = kval * w_mid


def _hyena_filters(zfeat, t_col, absdelta, alt, tabs, w1, b1, w2, b2, w3, b3, freq, w4, tc, w_cast):
    L = zfeat.shape[0]
    Q = L // RADIX
    c2 = w4.shape[1] // N_DIR
    n_ch = c2 // HYENA_ORDER

    def pad2(a, r, c):
        return jnp.pad(a.astype(F32), ((0, r - a.shape[0]), (0, c - a.shape[1])))

    def by_class(a):
        return jnp.stack([a[r::RADIX] for r in range(RADIX)], axis=0)

    feat = pl.pallas_call(
        _filter_feat_kernel,
        out_shape=jax.ShapeDtypeStruct((L, LANES), F32),
        name="filter_feat",
        compiler_params=_params(32, 0),
    )(pad2(by_class(zfeat).reshape(L, -1), L, LANES),
      pad2(w1, LANES, LANES), pad2(b1[None], 1, LANES),
      pad2(w2, LANES, LANES), pad2(b2[None], 1, LANES),
      pad2(w3, LANES, LANES), pad2(b3[None], 1, LANES), pad2(freq[None], 1, LANES))

    w4p = pad2(w4, LANES, w4.shape[1])
    nt = c2 // tc
    nct = n_ch // tc
    slab = w_cast.shape[0] // nt
    assert slab * nt == w_cast.shape[0] and slab % BF16_ROWS == 0
    cast_spec = pl.BlockSpec((slab, w_cast.shape[1]), lambda j: (j, 0))
    sp, kq, w_bf16 = pl.pallas_call(
        _filter_spec_kernel,
        grid=(nt,),
        in_specs=[
            _resident((RADIX, Q, LANES), lambda j: (0, 0, 0)),
            pl.BlockSpec((LANES, tc), lambda j: (0, j)),
            pl.BlockSpec((LANES, tc), lambda j: (0, nt + j)),
            _resident((RADIX, Q, 1), lambda j: (0, 0, 0)),
            pl.BlockSpec((1, tc), lambda j: (0, j % nct)),
            _resident((Q, 1), lambda j: (0, 0)),
            _resident((2 * RADIX, Q, Q), lambda j: (0, 0, 0)),
            cast_spec,
        ],
        out_specs=[pl.BlockSpec((2 * RADIX, Q, tc), lambda j: (0, 0, j)),
                   pl.BlockSpec((RADIX, tc), lambda j: (0, j)), cast_spec],
        out_shape=[jax.ShapeDtypeStruct((2 * RADIX, Q, c2), F32),
                   jax.ShapeDtypeStruct((RADIX, c2), F32),
                   jax.ShapeDtypeStruct(w_cast.shape, BF16)],
        name="filter_spec",
        compiler_params=_params(58, 1),
    )(feat.reshape(RADIX, Q, LANES), w4p, w4p, by_class(t_col), absdelta, alt, tabs, w_cast)
    return (sp, kq), w_bf16


def _proj_kernel(x_ref, g_ref, w_ref, o_ref, h_ref, *, n_mix):
    j = pl.program_id(1)

    @pl.when(j == 0)
    def _():
        h_ref[...] = _rms(x_ref[...], g_ref[...]).astype(BF16)

    tn = o_ref.shape[1]
    has_gates = (j + 1) * tn > n_mix

    @pl.when(jnp.logical_not(has_gates))
    def _():
        o_ref[...] = _dot(h_ref[...], w_ref[...]).astype(BF16)

    @pl.when(has_gates)
    def _():
        acc = _dot(h_ref[...], w_ref[...])
        col = j * tn + lax.broadcasted_iota(jnp.int32, acc.shape, 1)
        o_ref[...] = jnp.where(col >= n_mix, _sigmoid(acc), acc).astype(BF16)


def _proj(x2, g, w, n_mix, tm, tn):
    M, D = x2.shape
    E = w.shape[1]
    return pl.pallas_call(
        functools.partial(_proj_kernel, n_mix=n_mix),
        grid=(M // tm, E // tn),
        in_specs=[
            pl.BlockSpec((tm, D), lambda i, j: (i, 0)),
            _resident((1, D), lambda i, j: (0, 0)),
            pl.BlockSpec((D, tn), lambda i, j: (0, j)),
        ],
        out_specs=pl.BlockSpec((tm, tn), lambda i, j: (i, j)),
        out_shape=jax.ShapeDtypeStruct((M, E), BF16),
        scratch_shapes=[pltpu.VMEM((tm, D), BF16)],
        name="proj",
        compiler_params=_params(56, 2),
    )(x2, g, w)


def _short_conv_rows(p_ref, il_ref, w_ref, b_ref, r0, rc):
    L = p_ref.shape[1]
    t0, t1 = RADIX * r0, RADIX * (r0 + rc)
    parts = [[] for _ in range(RADIX)]
    for c in range(il_ref.shape[0]):
        il_ref[c] = p_ref[0, t0:t1, c * LANES:(c + 1) * LANES].astype(F32)
        for r in range(RADIX):
            parts[r].append(il_ref[c, pl.ds(r, rc, stride=RADIX), :])
    p = [jnp.concatenate(x, axis=1) for x in parts]
    row = lax.broadcasted_iota(jnp.int32, p[0].shape, 0)
    if t0 == 0:
        before = 0.0
    else:
        before = p_ref[0, t0 - BF16_ROWS:t0, :].astype(F32)[BF16_ROWS - 1:BF16_ROWS, :]
    if t1 == L:
        after = 0.0
    else:
        after = p_ref[0, t1:t1 + BF16_ROWS, :].astype(F32)[0:1, :]
    last_prev = jnp.where(row == 0, before, pltpu.roll(p[RADIX - 1], 1, axis=0))
    first_next = jnp.where(row == rc - 1, after, pltpu.roll(p[0], rc - 1, axis=0))
    prev = [last_prev] + p[:-1]
    nxt = p[1:] + [first_next]
    w0, w1, w2, b = w_ref[0:1, :], w_ref[1:2, :], w_ref[2:3, :], b_ref[...]
    return [prev[r] * w0 + p[r] * w1 + nxt[r] * w2 + b for r in range(RADIX)]


def _hyena_kernel(v_ref, x1_ref, x2_ref, wv_ref, w1_ref, w2_ref, bv_ref, b1_ref, b2_ref,
                  sp0_ref, kq0_ref, sp1_ref, kq1_ref, hb0_ref, hb1_ref, alt_ref, tab_ref,
                  *refs, rc, n_cast):
    o_ref = refs[n_cast]
    zf_ref, zb_ref, p_ref, il_ref = refs[2 * n_cast + 1:]
    for src, dst in zip(refs[:n_cast], refs[n_cast + 1:2 * n_cast + 1]):
        dst[...] = src[...].astype(dst.dtype)
    Q = tab_ref.shape[1]
    R = RADIX
    chunks = [(r0, slice(r0, r0 + rc)) for r0 in range(0, Q, rc)]

    def put_z(zs, rows):
        alt = alt_ref[rows, :]
        for r in range(R):
            zf_ref[r, rows, :] = zs[r]
            zb_ref[r, rows, :] = zs[r].astype(BF16)
        return [_colsum(z * alt) for z in zs]

    def spectrum_product(sp_ref):
        zb = [zb_ref[r] for r in range(R)]
        for _, rows in chunks:
            a = [_dot(tab_ref[r, rows, :], zb[r]) for r in range(R)]
            b = [_dot(tab_ref[R + r, rows, :], zb[r]) for r in range(R)]
            e, f = a[0] + a[2], a[0] - a[2]
            g, h = a[1] + a[3], a[1] - a[3]
            p, q = b[0] + b[2], b[0] - b[2]
            u, v = b[1] + b[3], b[1] - b[3]
            xr = [e + g, e - g, f + v, f - v]
            xi = [p + u, u - p, h - q, q + h]
            pr, pi = zip(*[_cmul(xr[s], xi[s], sp_ref[s, rows, :], sp_ref[R + s, rows, :])
                           for s in range(R)])
            s01p, s01m, s23p, s23m = pr[0] + pr[1], pr[0] - pr[1], pr[2] + pr[3], pr[2] - pr[3]
            t01p, t01m, t23p, t23m = pi[0] + pi[1], pi[0] - pi[1], pi[2] + pi[3], pi[2] - pi[3]
            for r, val in enumerate([s01p + s23p, s01m + t23p, s01p - s23p, s01m - t23p,
                                     t01m - t23m, t01p + s23m, t01m + t23m, t01p - s23m]):
                p_ref[r, rows, :] = val.astype(BF16)

    def conv_rows(rows, t, kq_ref, hb_ref):
        (xrq, xiq), (xr3, xi3) = _special_bins(t)
        prq, piq = _cmul(xrq, xiq, kq_ref[0:1, :], kq_ref[1:2, :])
        pr3, pi3 = _cmul(xr3, xi3, kq_ref[2:3, :], kq_ref[3:4, :])
        sp = [prq + pr3, SQH * (prq + piq + pi3 - pr3), piq - pi3, SQH * (piq - prq + pr3 + pi3)]
        alt = alt_ref[rows, :]
        bias = hb_ref[...]
        return [_dot(tab_ref[2 * R + r, rows, :], p_ref[r]) + _dot(tab_ref[3 * R + r, rows, :], p_ref[R + r])
                + alt * sp[r] + zf_ref[r, rows, :] * bias for r in range(R)]

    def add(acc, new):
        return new if acc is None else [x + y for x, y in zip(acc, new)]

    t = None
    for r0, rows in chunks:
        t = add(t, put_z(_short_conv_rows(v_ref, il_ref, wv_ref, bv_ref, r0, rc), rows))
    spectrum_product(sp0_ref)
    t1 = None
    for r0, rows in chunks:
        y = conv_rows(rows, t, kq0_ref, hb0_ref)
        gate = _short_conv_rows(x1_ref, il_ref, w1_ref, b1_ref, r0, rc)
        t1 = add(t1, put_z([gate[r] * y[r] for r in range(R)], rows))
    spectrum_product(sp1_ref)
    for r0, rows in chunks:
        y = conv_rows(rows, t1, kq1_ref, hb1_ref)
        gate = _short_conv_rows(x2_ref, il_ref, w2_ref, b2_ref, r0, rc)
        out = [gate[r] * y[r] for r in range(R)]
        for c in range(il_ref.shape[0]):
            lanes = slice(c * LANES, (c + 1) * LANES)
            for r in range(R):
                il_ref[c, pl.ds(r, rc, stride=R), :] = out[r][:, lanes]
            o_ref[0, R * r0:R * (r0 + rc), lanes] = il_ref[c].astype(o_ref.dtype)


def _hyena(p3, conv_w, conv_b, spectra, hbias, alt, tabs, n_ch, ct, to_cast):
    B, L, _ = p3.shape
    Q = L // RADIX
    nct = n_ch // ct
    rc = _pick(Q, 512)
    sp, kq = spectra
    hb = hbias.reshape(1, HYENA_ORDER * n_ch).astype(F32)
    cb = conv_b.reshape(1, -1).astype(F32)
    cw = conv_w.astype(F32)

    def pspec(g):
        return pl.BlockSpec((1, L, ct), lambda j, b, g=g: (b, 0, g * nct + j))

    def rowspec(rows, g):
        return pl.BlockSpec((rows, ct), lambda j, b, g=g: (0, g * nct + j))

    def specspec(g):
        return _resident((2 * RADIX, Q, ct), lambda j, b, g=g: (0, 0, g * nct + j))

    steps = nct * B
    cast_specs = []
    for w in to_cast:
        slab = w.shape[0] // steps
        assert slab * steps == w.shape[0] and slab % BF16_ROWS == 0, w.shape
        cast_specs.append(pl.BlockSpec((slab, w.shape[1]), lambda j, b: (j * B + b, 0)))

    return pl.pallas_call(
        functools.partial(_hyena_kernel, rc=rc, n_cast=len(to_cast)),
        grid=(nct, B),
        in_specs=[
            pspec(0), pspec(1), pspec(2),
            rowspec(3, 0), rowspec(3, 1), rowspec(3, 2),
            rowspec(1, 0), rowspec(1, 1), rowspec(1, 2),
            specspec(0), rowspec(RADIX, 0), specspec(1), rowspec(RADIX, 1),
            rowspec(1, 0), rowspec(1, 1),
            _resident((Q, 1), lambda j, b: (0, 0)),
            _resident(tabs.shape, lambda j, b: (0, 0, 0)),
        ] + cast_specs,
        out_specs=[pl.BlockSpec((1, L, ct), lambda j, b: (b, 0, j))] + cast_specs,
        out_shape=[jax.ShapeDtypeStruct((B, L, n_ch), BF16)]
                  + [jax.ShapeDtypeStruct(w.shape, BF16) for w in to_cast],
        scratch_shapes=[pltpu.VMEM((RADIX, Q, ct), F32), pltpu.VMEM((RADIX, Q, ct), BF16),
                        pltpu.VMEM((2 * RADIX, Q, ct), BF16),
                        pltpu.VMEM((ct // LANES, RADIX * rc, LANES), F32)],
        name="hyena",
        compiler_params=_params(58, 2),
    )(p3, p3, p3, cw, cw, cw, cb, cb, cb, sp, kq, sp, kq, hb, hb, alt, tabs, *to_cast)


def _gelu(x):
    return 0.5 * x * (1.0 + lax.erf(x * math.sqrt(0.5)))


def _sgu_kernel(u_ref, v_ref, g_ref, b_ref, ws_ref, bs_ref, o_ref, *, heads, chunk):
    u = _gelu(u_ref[...].astype(F32))
    v = _gelu(v_ref[...].astype(F32))
    mu = jnp.mean(v, axis=-1, keepdims=True)
    vc = v - mu
    var = jnp.mean(vc * vc, axis=-1, keepdims=True)
    vn = (vc * lax.rsqrt(var + EPS) * g_ref[...] + b_ref[...]).astype(BF16)
    tm, ds = vn.shape
    hd = ds // heads
    for n in range(tm // chunk):
        rows = slice(n * chunk, (n + 1) * chunk)
        for h in range(heads):
            cols = slice(h * hd, (h + 1) * hd)
            s = _dot(ws_ref[h], vn[rows, cols]) + bs_ref[:, h:h + 1]
            o_ref[rows, cols] = (u[rows, cols] * s).astype(o_ref.dtype)


def _sgu(p2, ln_g, ln_b, w_s, b_s, col0, ds, tm):
    M = p2.shape[0]
    heads, chunk, _ = w_s.shape
    cb = col0 // ds
    full2 = lambda i: (0, 0)
    return pl.pallas_call(
        functools.partial(_sgu_kernel, heads=heads, chunk=chunk),
        grid=(M // tm,),
        in_specs=[
            pl.BlockSpec((tm, ds), lambda i: (i, cb)),
            pl.BlockSpec((tm, ds), lambda i: (i, cb + 1)),
            _resident((1, ds), full2),
            _resident((1, ds), full2),
            _resident((heads, chunk, chunk), lambda i: (0, 0, 0)),
            _resident((chunk, heads), full2),
        ],
        out_specs=pl.BlockSpec((tm, ds), lambda i: (i, 0)),
        out_shape=jax.ShapeDtypeStruct((M, ds), BF16),
        name="sgu",
        compiler_params=_params(48, 1),
    )(p2, p2, ln_g.reshape(1, ds).astype(F32), ln_b.reshape(1, ds).astype(F32),
      w_s.astype(BF16), b_s.T.astype(F32))


def _merge_kernel(x_ref, yh_ref, ys_ref, gh0_ref, gh1_ref, gs0_ref, gs1_ref,
                  wbh_ref, wbs_ref, wo_ref, o_ref):
    gh = jnp.concatenate([gh0_ref[...], gh1_ref[...]], axis=1).astype(F32)
    gs = jnp.concatenate([gs0_ref[...], gs1_ref[...]], axis=1).astype(F32)
    a = _dot(yh_ref[...], wbh_ref[...]) * gh + _dot(ys_ref[...], wbs_ref[...]) * gs
    o_ref[...] = x_ref[...] + _dot(a.astype(BF16), wo_ref[...])


def _merge(x2, yh, ys, proj, gate_col, wbh, wbs, wo, tm):
    M, D = x2.shape
    gw = D // 2
    gb = gate_col // gw
    assert gate_col % gw == 0
    full = lambda i: (0, 0)

    def gate(c):
        return pl.BlockSpec((tm, gw), lambda i, c=c: (i, gb + c))

    return pl.pallas_call(
        _merge_kernel,
        grid=(M // tm,),
        in_specs=[
            pl.BlockSpec((tm, D), lambda i: (i, 0)),
            pl.BlockSpec((tm, yh.shape[1]), lambda i: (i, 0)),
            pl.BlockSpec((tm, ys.shape[1]), lambda i: (i, 0)),
            gate(0), gate(1), gate(2), gate(3),
            _resident(wbh.shape, full),
            _resident(wbs.shape, full),
            _resident(wo.shape, full),
        ],
        out_specs=pl.BlockSpec((tm, D), lambda i: (i, 0)),
        out_shape=jax.ShapeDtypeStruct((M, D), F32),
        name="merge",
        compiler_params=_params(56, 1),
    )(x2, yh, ys, proj, proj, proj, proj, wbh, wbs, wo)


def _ffn_kernel(x_ref, g_ref, wg_ref, wu_ref, wo_ref, gf_ref, o_ref, h_ref, *, final_norm):
    f = pl.program_id(1)

    @pl.when(f == 0)
    def _():
        x = x_ref[...]
        h_ref[...] = _rms(x, g_ref[...]).astype(BF16)
        o_ref[...] = x

    h = h_ref[...]
    gate = _dot(h, wg_ref[...])
    up = _dot(h, wu_ref[...])
    a = (gate * _sigmoid(gate) * up).astype(BF16)
    o_ref[...] += _dot(a, wo_ref[...])

    if final_norm:
        @pl.when(f == pl.num_programs(1) - 1)
        def _():
            o_ref[...] = _rms(o_ref[...], gf_ref[...])


def _ffn(x2, g, w_in, w_out, g_final, final_norm, tm, tf):
    M, D = x2.shape
    FF = w_out.shape[0]
    nf = FF // tf
    full = lambda i, f: (0, 0)
    return pl.pallas_call(
        functools.partial(_ffn_kernel, final_norm=final_norm),
        grid=(M // tm, nf),
        in_specs=[
            pl.BlockSpec((tm, D), lambda i, f: (i, 0)),
            _resident((1, D), full),
            pl.BlockSpec((D, tf), lambda i, f: (0, f)),
            pl.BlockSpec((D, tf), lambda i, f: (0, nf + f)),
            pl.BlockSpec((tf, D), lambda i, f: (f, 0)),
            _resident((1, D), full),
        ],
        out_specs=pl.BlockSpec((tm, D), lambda i, f: (i, 0)),
        out_shape=jax.ShapeDtypeStruct((M, D), F32),
        scratch_shapes=[pltpu.VMEM((tm, D), BF16)],
        name="ffn",
        compiler_params=_params(60, 2),
    )(x2, g, w_in, w_in, w_out, g_final)


def _transform_tables(L):
    Q = L // RADIX
    k = np.arange(Q, dtype=np.int64)[:, None]
    m = np.arange(Q, dtype=np.int64)[None, :]

    def tab(fn, r):
        return fn(((k * (RADIX * m + r)) % (2 * L)) * (np.pi / L)).astype(np.float32)

    cs = [tab(np.cos, r) for r in range(RADIX)] + [tab(np.sin, r) for r in range(RADIX)]
    stack = np.stack(cs + [t.T for t in cs], axis=0).astype(BF16)
    alt = (1.0 - 2.0 * (np.arange(Q) % 2)).astype(np.float32)[:, None]
    return jnp.asarray(stack), jnp.asarray(alt)


def _position_features(L, emb):
    t = jnp.linspace(0.0, 1.0, L, dtype=F32)[:, None]
    bands = (emb - 1) // 2
    w = 2.0 * math.pi * jnp.arange(L, dtype=F32)[:, None] / L
    f = jnp.linspace(1e-4, bands - 1, bands, dtype=F32)[None, :]
    return jnp.concatenate([t, jnp.cos(f * w), -jnp.sin(f * w)], axis=-1), t


def _pick(n, pref):
    t = min(n, pref)
    while n % t:
        t -= 1
    return t


def kernel(x, norm_mix_g, w_in, short_conv_w, short_conv_b, filt_w1, filt_b1, filt_w2, filt_b2, filt_w3, filt_b3, filt_freq, filt_w4, hyena_bias, sgu_ln_g, sgu_ln_b, sgu_w_s, sgu_b_s, w_branch_hyena, w_branch_sgu, w_out, norm_ffn_g, w_ffn_in, w_ffn_out, norm_final_g):
    B, L, D = x.shape
    depth = w_in.shape[0]
    n_ch = w_branch_hyena.shape[1]
    ds = w_branch_sgu.shape[1]
    assert n_ch == ds and 2 * n_ch == D and L % RADIX == 0 and depth >= 1
    n_mix = 3 * n_ch + 2 * ds
    M = B * L

    tabs, alt = _transform_tables(L)
    zfeat, t_col = _position_features(L, filt_w1.shape[1])
    max_decay = math.log(DECAY_TARGET) / FAST_DECAY
    min_decay = math.log(DECAY_TARGET) / SLOW_DECAY
    absdelta = jnp.abs(jnp.linspace(min_decay, max_decay, n_ch, dtype=F32))[None, :]

    ct = _pick(n_ch, 256)
    tm = _pick(M, 1024)
    tn = LANES * _pick(w_in.shape[2] // LANES, 1536 // LANES)

    x2 = x.reshape(M, D)
    for l in range(depth):
        spectra, w_in_bf16 = _hyena_filters(zfeat, t_col, absdelta, alt, tabs,
                                            filt_w1[l], filt_b1[l], filt_w2[l], filt_b2[l],
                                            filt_w3[l], filt_b3[l], filt_freq[l], filt_w4[l], ct, w_in[l])
        proj = _proj(x2, norm_mix_g[l].reshape(1, D), w_in_bf16, n_mix, tm, tn)
        y_hy, wbh, wbs, wo, wfi, wfo = _hyena(
            proj.reshape(B, L, proj.shape[1]), short_conv_w[l], short_conv_b[l], spectra,
            hyena_bias[l], alt, tabs, n_ch, ct,
            [w_branch_hyena[l], w_branch_sgu[l], w_out[l], w_ffn_in[l], w_ffn_out[l]])
        y_sgu = _sgu(proj, sgu_ln_g[l], sgu_ln_b[l], sgu_w_s[l], sgu_b_s[l], 3 * n_ch, ds, tm)
        x2 = _merge(x2, y_hy.reshape(M, n_ch), y_sgu, proj, n_mix, wbh, wbs, wo, _pick(M, 512))
        last = l == depth - 1
        x2 = _ffn(x2, norm_ffn_g[l].reshape(1, D), wfi, wfo, norm_final_g.reshape(1, D), last,
                  _pick(M, 1024), _pick(w_ffn_out.shape[1], 512))
    return x2.reshape(B, L, D)
```

```python
import functools
import math

import numpy as np
import jax
import jax.numpy as jnp
from jax import lax
from jax.experimental import pallas as pl
from jax.experimental.pallas import tpu as pltpu

F32 = jnp.float32
BF16 = jnp.bfloat16

EPS = 1e-6
FAST_DECAY = 0.3
SLOW_DECAY = 1.5
DECAY_TARGET = 1e-2
MOD_SHIFT = 0.0
N_DIR = 2
HYENA_ORDER = 2
RADIX = 4
SQH = math.sqrt(0.5)
LANES = 128
BF16_ROWS = 16
MIB = 1024 * 1024


def _params(vmem_mib, n_grid):
    return pltpu.CompilerParams(
        dimension_semantics=("arbitrary",) * n_grid,
        vmem_limit_bytes=vmem_mib * MIB)


def _resident(block_shape, index_map):
    return pl.BlockSpec(block_shape, index_map, pipeline_mode=pl.Buffered(1))


def _rms(x, g):
    ms = jnp.mean(x * x, axis=-1, keepdims=True)
    return x * lax.rsqrt(ms + EPS) * g


def _dot(a, b):
    return jnp.dot(a, b, preferred_element_type=F32)


def _sigmoid(x):
    return 0.5 * jnp.tanh(0.5 * x) + 0.5


def _colsum(a):
    return jnp.sum(a, axis=0, keepdims=True)


def _cmul(xr, xi, kr, ki):
    return xr * kr - xi * ki, xr * ki + xi * kr


def _special_bins(t):
    d, s = SQH * (t[1] - t[3]), SQH * (t[1] + t[3])
    return (t[0] + d, t[2] + s), (t[0] - d, s - t[2])


def _filter_feat_kernel(z_ref, w1_ref, b1_ref, w2_ref, b2_ref, w3_ref, b3_ref, fr_ref, o_ref):
    hp = lax.Precision.HIGHEST
    fr = fr_ref[...]
    h = jnp.sin(fr * (jnp.dot(z_ref[...], w1_ref[...], precision=hp, preferred_element_type=F32) + b1_ref[...]))
    h = jnp.sin(fr * (jnp.dot(h, w2_ref[...], precision=hp, preferred_element_type=F32) + b2_ref[...]))
    h = jnp.sin(fr * (jnp.dot(h, w3_ref[...], precision=hp, preferred_element_type=F32) + b3_ref[...]))
    o_ref[...] = h


def _filter_spec_kernel(f_ref, w4f_ref, w4b_ref, t_ref, ad_ref, alt_ref, tab_ref, wsrc_ref,
                        sp_ref, kq_ref, wdst_ref):
    wdst_ref[...] = wsrc_ref[...].astype(wdst_ref.dtype)
    hp = lax.Precision.HIGHEST
    Q = f_ref.shape[1]
    inv_n = 1.0 / (2 * RADIX * Q)
    ad = ad_ref[...]
    row = lax.broadcasted_iota(jnp.int32, (Q, ad.shape[1]), 0)

    hf, hb = [], []
    for r in range(RADIX):
        f = f_ref[r]
        decay = jnp.exp(-t_ref[r] * ad) + MOD_SHIFT
        hf.append(jnp.dot(f, w4f_ref[...], precision=hp, preferred_element_type=F32) * decay)
        hb.append(jnp.dot(f, w4b_ref[...], precision=hp, preferred_element_type=F32) * decay)
    hb[0] = jnp.where(row == 0, 0.0, hb[0])
    norm = sum(_colsum(jnp.abs(h)) for h in hf + hb)
    inv = 1.0 / norm
    ev = [(hf[r] + hb[r]) * inv for r in range(RADIX)]
    od = [(hf[r] - hb[r]) * inv for r in range(RADIX)]
    evb = [e.astype(BF16) for e in ev]
    odb = [o.astype(BF16) for o in od]

    def cos_t(r, zb):
        return _dot(tab_ref[r], zb[r])

    def sin_t(r, zb):
        return _dot(tab_ref[RADIX + r], zb[r])

    a0, a2 = cos_t(0, evb), cos_t(2, evb)
    e, f_ = a0 + a2, a0 - a2
    g = cos_t(1, evb) + cos_t(3, evb)
    v = sin_t(1, evb) - sin_t(3, evb)
    b0, b2 = sin_t(0, odb), sin_t(2, odb)
    p, q = b0 + b2, b0 - b2
    u = sin_t(1, odb) + sin_t(3, odb)
    h = cos_t(1, odb) - cos_t(3, odb)
    first = row[:, :1] == 0
    w_end = jnp.where(first, inv_n, 2.0 * inv_n)
    w_dup = jnp.where(first, 0.0, 2.0 * inv_n)
    w_mid = 2.0 * inv_n
    for s, (kr, ki, w) in enumerate([(e + g, p + u, w_end), (e - g, u - p, w_end),
                                     (f_ + v, h - q, w_mid), (f_ - v, q + h, w_dup)]):
        sp_ref[s] = kr * w
        sp_ref[RADIX + s] = ki * w
    alt = alt_ref[...]
    (krq, _), (kr3, _) = _special_bins([_colsum(x * alt) for x in ev])
    (_, kiq), (_, ki3) = _special_bins([_colsum(x * alt) for x in od])
    for i, kval in enumerate((krq, kiq, kr3, ki3)):
        kq_ref[i:i + 1, :] = kval * w_mid


def _hyena_filters(zfeat, t_col, absdelta, alt, tabs, w1, b1, w2, b2, w3, b3, freq, w4, tc, w_cast):
    L = zfeat.shape[0]
    Q = L // RADIX
    c2 = w4.shape[1] // N_DIR
    n_ch = c2 // HYENA_ORDER

    def pad2(a, r, c):
        return jnp.pad(a.astype(F32), ((0, r - a.shape[0]), (0, c - a.shape[1])))

    feat = pl.pallas_call(
        _filter_feat_kernel,
        out_shape=jax.ShapeDtypeStruct((L, LANES), F32),
        name="filter_feat",
        compiler_params=_params(32, 0),
    )(pad2(zfeat, L, LANES),
      pad2(w1, LANES, LANES), pad2(b1[None], 1, LANES),
      pad2(w2, LANES, LANES), pad2(b2[None], 1, LANES),
      pad2(w3, LANES, LANES), pad2(b3[None], 1, LANES), pad2(freq[None], 1, LANES))

    w4p = pad2(w4, LANES, w4.shape[1])
    nt = c2 // tc
    nct = n_ch // tc
    slab = w_cast.shape[0] // nt
    assert slab * nt == w_cast.shape[0] and slab % BF16_ROWS == 0
    cast_spec = pl.BlockSpec((slab, w_cast.shape[1]), lambda j: (j, 0))
    sp, kq, w_bf16 = pl.pallas_call(
        _filter_spec_kernel,
        grid=(nt,),
        in_specs=[
            _resident((RADIX, Q, LANES), lambda j: (0, 0, 0)),
            pl.BlockSpec((LANES, tc), lambda j: (0, j)),
            pl.BlockSpec((LANES, tc), lambda j: (0, nt + j)),
            _resident((RADIX, Q, 1), lambda j: (0, 0, 0)),
            pl.BlockSpec((1, tc), lambda j: (0, j % nct)),
            _resident((Q, 1), lambda j: (0, 0)),
            _resident((2 * RADIX, Q, Q), lambda j: (0, 0, 0)),
            cast_spec,
        ],
        out_specs=[pl.BlockSpec((2 * RADIX, Q, tc), lambda j: (0, 0, j)),
                   pl.BlockSpec((RADIX, tc), lambda j: (0, j)), cast_spec],
        out_shape=[jax.ShapeDtypeStruct((2 * RADIX, Q, c2), F32),
                   jax.ShapeDtypeStruct((RADIX, c2), F32),
                   jax.ShapeDtypeStruct(w_cast.shape, BF16)],
        name="filter_spec",
        compiler_params=_params(58, 1),
    )(feat.reshape(RADIX, Q, LANES), w4p, w4p, t_col.reshape(RADIX, Q, 1), absdelta, alt, tabs, w_cast)
    return (sp, kq), w_bf16


def _proj_kernel(x_ref, g_ref, w_ref, o_ref, h_ref, *, n_mix):
    j = pl.program_id(1)
    tn = o_ref.shape[1]
    has_gates = (j + 1) * tn > n_mix

    def with_gates(acc):
        col = j * tn + lax.broadcasted_iota(jnp.int32, acc.shape, 1)
        return jnp.where(col >= n_mix, _sigmoid(acc), acc)

    @pl.when(j == 0)
    def _():
        h = _rms(x_ref[...], g_ref[...]).astype(BF16)
        h_ref[...] = h
        acc = _dot(h, w_ref[...])
        o_ref[...] = (with_gates(acc) if tn > n_mix else acc).astype(BF16)

    @pl.when((j > 0) & jnp.logical_not(has_gates))
    def _():
        o_ref[...] = _dot(h_ref[...], w_ref[...]).astype(BF16)

    @pl.when((j > 0) & has_gates)
    def _():
        o_ref[...] = with_gates(_dot(h_ref[...], w_ref[...])).astype(BF16)


def _proj(x2, g, w, n_mix, tm, tn):
    M, D = x2.shape
    E = w.shape[1]
    return pl.pallas_call(
        functools.partial(_proj_kernel, n_mix=n_mix),
        grid=(M // tm, E // tn),
        in_specs=[
            pl.BlockSpec((tm, D), lambda i, j: (i, 0)),
            _resident((1, D), lambda i, j: (0, 0)),
            pl.BlockSpec((D, tn), lambda i, j: (0, j)),
        ],
        out_specs=pl.BlockSpec((tm, tn), lambda i, j: (i, j)),
        out_shape=jax.ShapeDtypeStruct((M, E), BF16),
        scratch_shapes=[pltpu.VMEM((tm, D), BF16)],
        name="proj",
        compiler_params=_params(56, 2),
    )(x2, g, w)


def _short_conv_rows(p_ref, il_ref, w_ref, b_ref, r0, rc):
    L = p_ref.shape[1]
    t0, t1 = RADIX * r0, RADIX * (r0 + rc)
    parts = [[] for _ in range(RADIX)]
    for c in range(il_ref.shape[0]):
        il_ref[c] = p_ref[0, t0:t1, c * LANES:(c + 1) * LANES].astype(F32)
        for r in range(RADIX):
            parts[r].append(il_ref[c, pl.ds(r, rc, stride=RADIX), :])
    p = [jnp.concatenate(x, axis=1) for x in parts]
    row = lax.broadcasted_iota(jnp.int32, p[0].shape, 0)
    if t0 == 0:
        before = 0.0
    else:
        before = p_ref[0, t0 - BF16_ROWS:t0, :].astype(F32)[BF16_ROWS - 1:BF16_ROWS, :]
    if t1 == L:
        after = 0.0
    else:
        after = p_ref[0, t1:t1 + BF16_ROWS, :].astype(F32)[0:1, :]
    last_prev = jnp.where(row == 0, before, pltpu.roll(p[RADIX - 1], 1, axis=0))
    first_next = jnp.where(row == rc - 1, after, pltpu.roll(p[0], rc - 1, axis=0))
    prev = [last_prev] + p[:-1]
    nxt = p[1:] + [first_next]
    w0, w1, w2, b = w_ref[0:1, :], w_ref[1:2, :], w_ref[2:3, :], b_ref[...]
    return [prev[r] * w0 + p[r] * w1 + nxt[r] * w2 + b for r in range(RADIX)]


def _hyena_kernel(v_ref, x1_ref, x2_ref, wv_ref, w1_ref, w2_ref, bv_ref, b1_ref, b2_ref,
                  sp0_ref, kq0_ref, sp1_ref, kq1_ref, hb0_ref, hb1_ref, alt_ref, tab_ref,
                  *refs, rc, n_cast):
    o_ref = refs[n_cast]
    zf_ref, zb_ref, p_ref, il_ref = refs[2 * n_cast + 1:]
    for src, dst in zip(refs[:n_cast], refs[n_cast + 1:2 * n_cast + 1]):
        dst[...] = src[...].astype(dst.dtype)
    Q = tab_ref.shape[1]
    R = RADIX
    chunks = [(r0, slice(r0, r0 + rc)) for r0 in range(0, Q, rc)]

    def put_z(zs, rows):
        alt = alt_ref[rows, :]
        for r in range(R):
            zf_ref[r, rows, :] = zs[r]
            zb_ref[r, rows, :] = zs[r].astype(BF16)
        return [_colsum(z * alt) for z in zs]

    def spectrum_product(sp_ref):
        zb = [zb_ref[r] for r in range(R)]
        for _, rows in chunks:
            a = [_dot(tab_ref[r, rows, :], zb[r]) for r in range(R)]
            b = [_dot(tab_ref[R + r, rows, :], zb[r]) for r in range(R)]
            e, f = a[0] + a[2], a[0] - a[2]
            g, h = a[1] + a[3], a[1] - a[3]
            p, q = b[0] + b[2], b[0] - b[2]
            u, v = b[1] + b[3], b[1] - b[3]
            xr = [e + g, e - g, f + v, f - v]
            xi = [p + u, u - p, h - q, q + h]
            pr, pi = zip(*[_cmul(xr[s], xi[s], sp_ref[s, rows, :], sp_ref[R + s, rows, :])
                           for s in range(R)])
            s01p, s01m, s23p, s23m = pr[0] + pr[1], pr[0] - pr[1], pr[2] + pr[3], pr[2] - pr[3]
            t01p, t01m, t23p, t23m = pi[0] + pi[1], pi[0] - pi[1], pi[2] + pi[3], pi[2] - pi[3]
            for r, val in enumerate([s01p + s23p, s01m + t23p, s01p - s23p, s01m - t23p,
                                     t01m - t23m, t01p + s23m, t01m + t23m, t01p - s23m]):
                p_ref[r, rows, :] = val.astype(BF16)

    def conv_rows(rows, t, kq_ref, hb_ref):
        (xrq, xiq), (xr3, xi3) = _special_bins(t)
        prq, piq = _cmul(xrq, xiq, kq_ref[0:1, :], kq_ref[1:2, :])
        pr3, pi3 = _cmul(xr3, xi3, kq_ref[2:3, :], kq_ref[3:4, :])
        sp = [prq + pr3, SQH * (prq + piq + pi3 - pr3), piq - pi3, SQH * (piq - prq + pr3 + pi3)]
        alt = alt_ref[rows, :]
        bias = hb_ref[...]
        return [_dot(tab_ref[2 * R + r, rows, :], p_ref[r]) + _dot(tab_ref[3 * R + r, rows, :], p_ref[R + r])
                + alt * sp[r] + zf_ref[r, rows, :] * bias for r in range(R)]

    def add(acc, new):
        return new if acc is None else [x + y for x, y in zip(acc, new)]

    t = None
    for r0, rows in chunks:
        t = add(t, put_z(_short_conv_rows(v_ref, il_ref, wv_ref, bv_ref, r0, rc), rows))
    spectrum_product(sp0_ref)
    t1 = None
    for r0, rows in chunks:
        y = conv_rows(rows, t, kq0_ref, hb0_ref)
        gate = _short_conv_rows(x1_ref, il_ref, w1_ref, b1_ref, r0, rc)
        t1 = add(t1, put_z([gate[r] * y[r] for r in range(R)], rows))
    spectrum_product(sp1_ref)
    for r0, rows in chunks:
        y = conv_rows(rows, t1, kq1_ref, hb1_ref)
        gate = _short_conv_rows(x2_ref, il_ref, w2_ref, b2_ref, r0, rc)
        out = [gate[r] * y[r] for r in range(R)]
        for c in range(il_ref.shape[0]):
            lanes = slice(c * LANES, (c + 1) * LANES)
            for r in range(R):
                il_ref[c, pl.ds(r, rc, stride=R), :] = out[r][:, lanes]
            o_ref[0, R * r0:R * (r0 + rc), lanes] = il_ref[c].astype(o_ref.dtype)


def _hyena(p3, conv_w, conv_b, spectra, hbias, alt, tabs, n_ch, ct, to_cast):
    B, L, _ = p3.shape
    Q = L // RADIX
    nct = n_ch // ct
    rc = _pick(Q, 512)
    sp, kq = spectra
    hb = hbias.reshape(1, HYENA_ORDER * n_ch).astype(F32)
    cb = conv_b.reshape(1, -1).astype(F32)
    cw = conv_w.astype(F32)

    def pspec(g):
        return pl.BlockSpec((1, L, ct), lambda j, b, g=g: (b, 0, g * nct + j))

    def rowspec(rows, g):
        return pl.BlockSpec((rows, ct), lambda j, b, g=g: (0, g * nct + j))

    def specspec(g):
        return _resident((2 * RADIX, Q, ct), lambda j, b, g=g: (0, 0, g * nct + j))

    steps = nct * B
    cast_specs = []
    for w in to_cast:
        slab = w.shape[0] // steps
        assert slab * steps == w.shape[0] and slab % BF16_ROWS == 0, w.shape
        cast_specs.append(pl.BlockSpec((slab, w.shape[1]), lambda j, b: (j * B + b, 0)))

    return pl.pallas_call(
        functools.partial(_hyena_kernel, rc=rc, n_cast=len(to_cast)),
        grid=(nct, B),
        in_specs=[
            pspec(0), pspec(1), pspec(2),
            rowspec(3, 0), rowspec(3, 1), rowspec(3, 2),
            rowspec(1, 0), rowspec(1, 1), rowspec(1, 2),
            specspec(0), rowspec(RADIX, 0), specspec(1), rowspec(RADIX, 1),
            rowspec(1, 0), rowspec(1, 1),
            _resident((Q, 1), lambda j, b: (0, 0)),
            _resident(tabs.shape, lambda j, b: (0, 0, 0)),
        ] + cast_specs,
        out_specs=[pl.BlockSpec((1, L, ct), lambda j, b: (b, 0, j))] + cast_specs,
        out_shape=[jax.ShapeDtypeStruct((B, L, n_ch), BF16)]
                  + [jax.ShapeDtypeStruct(w.shape, BF16) for w in to_cast],
        scratch_shapes=[pltpu.VMEM((RADIX, Q, ct), F32), pltpu.VMEM((RADIX, Q, ct), BF16),
                        pltpu.VMEM((2 * RADIX, Q, ct), BF16),
                        pltpu.VMEM((ct // LANES, RADIX * rc, LANES), F32)],
        name="hyena",
        compiler_params=_params(58, 2),
    )(p3, p3, p3, cw, cw, cw, cb, cb, cb, sp, kq, sp, kq, hb, hb, alt, tabs, *to_cast)


def _gelu(x):
    return 0.5 * x * (1.0 + lax.erf(x * math.sqrt(0.5)))


def _sgu_kernel(u_ref, v_ref, g_ref, b_ref, ws_ref, bs_ref, o_ref, *, heads, chunk):
    u = _gelu(u_ref[...].astype(F32))
    v = _gelu(v_ref[...].astype(F32))
    mu = jnp.mean(v, axis=-1, keepdims=True)
    vc = v - mu
    var = jnp.mean(vc * vc, axis=-1, keepdims=True)
    vn = (vc * lax.rsqrt(var + EPS) * g_ref[...] + b_ref[...]).astype(BF16)
    tm, ds = vn.shape
    hd = ds // heads
    for n in range(tm // chunk):
        rows = slice(n * chunk, (n + 1) * chunk)
        for h in range(heads):
            cols = slice(h * hd, (h + 1) * hd)
            s = _dot(ws_ref[h], vn[rows, cols]) + bs_ref[:, h:h + 1]
            o_ref[rows, cols] = (u[rows, cols] * s).astype(o_ref.dtype)


def _sgu(p2, ln_g, ln_b, w_s, b_s, col0, ds, tm):
    M = p2.shape[0]
    heads, chunk, _ = w_s.shape
    cb = col0 // ds
    full2 = lambda i: (0, 0)
    return pl.pallas_call(
        functools.partial(_sgu_kernel, heads=heads, chunk=chunk),
        grid=(M // tm,),
        in_specs=[
            pl.BlockSpec((tm, ds), lambda i: (i, cb)),
            pl.BlockSpec((tm, ds), lambda i: (i, cb + 1)),
            _resident((1, ds), full2),
            _resident((1, ds), full2),
            _resident((heads, chunk, chunk), lambda i: (0, 0, 0)),
            _resident((chunk, heads), full2),
        ],
        out_specs=pl.BlockSpec((tm, ds), lambda i: (i, 0)),
        out_shape=jax.ShapeDtypeStruct((M, ds), BF16),
        name="sgu",
        compiler_params=_params(48, 1),
    )(p2, p2, ln_g.reshape(1, ds).astype(F32), ln_b.reshape(1, ds).astype(F32),
      w_s.astype(BF16), b_s.T.astype(F32))


def _merge_kernel(x_ref, yh_ref, ys_ref, gh0_ref, gh1_ref, gs0_ref, gs1_ref,
                  wbh_ref, wbs_ref, wo_ref, o_ref):
    gh = jnp.concatenate([gh0_ref[...], gh1_ref[...]], axis=1).astype(F32)
    gs = jnp.concatenate([gs0_ref[...], gs1_ref[...]], axis=1).astype(F32)
    a = _dot(yh_ref[...], wbh_ref[...]) * gh + _dot(ys_ref[...], wbs_ref[...]) * gs
    o_ref[...] = x_ref[...] + _dot(a.astype(BF16), wo_ref[...])


def _merge(x2, yh, ys, proj, gate_col, wbh, wbs, wo, tm):
    M, D = x2.shape
    gw = D // 2
    gb = gate_col // gw
    assert gate_col % gw == 0
    full = lambda i: (0, 0)

    def gate(c):
        return pl.BlockSpec((tm, gw), lambda i, c=c: (i, gb + c))

    return pl.pallas_call(
        _merge_kernel,
        grid=(M // tm,),
        in_specs=[
            pl.BlockSpec((tm, D), lambda i: (i, 0)),
            pl.BlockSpec((tm, yh.shape[1]), lambda i: (i, 0)),
            pl.BlockSpec((tm, ys.shape[1]), lambda i: (i, 0)),
            gate(0), gate(1), gate(2), gate(3),
            _resident(wbh.shape, full),
            _resident(wbs.shape, full),
            _resident(wo.shape, full),
        ],
        out_specs=pl.BlockSpec((tm, D), lambda i: (i, 0)),
        out_shape=jax.ShapeDtypeStruct((M, D), F32),
        name="merge",
        compiler_params=_params(56, 1),
    )(x2, yh, ys, proj, proj, proj, proj, wbh, wbs, wo)


def _ffn_kernel(x_ref, g_ref, wg_ref, wu_ref, wo_ref, gf_ref, o_ref, h_ref, *, final_norm):
    f = pl.program_id(1)

    def tile(h):
        gate = _dot(h, wg_ref[...])
        up = _dot(h, wu_ref[...])
        a = (gate * _sigmoid(gate) * up).astype(BF16)
        return _dot(a, wo_ref[...])

    @pl.when(f == 0)
    def _():
        x = x_ref[...]
        h = _rms(x, g_ref[...]).astype(BF16)
        h_ref[...] = h
        o_ref[...] = x + tile(h)

    @pl.when(f > 0)
    def _():
        o_ref[...] += tile(h_ref[...])

    if final_norm:
        @pl.when(f == pl.num_programs(1) - 1)
        def _():
            o_ref[...] = _rms(o_ref[...], gf_ref[...])


def _ffn(x2, g, w_in, w_out, g_final, final_norm, tm, tf):
    M, D = x2.shape
    FF = w_out.shape[0]
    nf = FF // tf
    full = lambda i, f: (0, 0)
    return pl.pallas_call(
        functools.partial(_ffn_kernel, final_norm=final_norm),
        grid=(M // tm, nf),
        in_specs=[
            pl.BlockSpec((tm, D), lambda i, f: (i, 0)),
            _resident((1, D), full),
            pl.BlockSpec((D, tf), lambda i, f: (0, f)),
            pl.BlockSpec((D, tf), lambda i, f: (0, nf + f)),
            pl.BlockSpec((tf, D), lambda i, f: (f, 0)),
            _resident((1, D), full),
        ],
        out_specs=pl.BlockSpec((tm, D), lambda i, f: (i, 0)),
        out_shape=jax.ShapeDtypeStruct((M, D), F32),
        scratch_shapes=[pltpu.VMEM((tm, D), BF16)],
        name="ffn",
        compiler_params=_params(60, 2),
    )(x2, g, w_in, w_in, w_out, g_final)


def _transform_tables(L):
    Q = L // RADIX
    k = np.arange(Q, dtype=np.int64)[:, None]
    m = np.arange(Q, dtype=np.int64)[None, :]

    def tab(fn, r):
        return fn(((k * (RADIX * m + r)) % (2 * L)) * (np.pi / L)).astype(np.float32)

    cs = [tab(np.cos, r) for r in range(RADIX)] + [tab(np.sin, r) for r in range(RADIX)]
    stack = np.stack(cs + [t.T for t in cs], axis=0).astype(BF16)
    alt = (1.0 - 2.0 * (np.arange(Q) % 2)).astype(np.float32)[:, None]
    return jnp.asarray(stack), jnp.asarray(alt)


def _position_features(L, emb):
    Q = L // RADIX
    row = jnp.arange(L, dtype=jnp.int32)
    n = ((row % Q) * RADIX + row // Q).astype(F32)[:, None]
    t = n / (L - 1)
    bands = (emb - 1) // 2
    w = 2.0 * math.pi * n / L
    f = jnp.linspace(1e-4, bands - 1, bands, dtype=F32)[None, :]
    return jnp.concatenate([t, jnp.cos(f * w), -jnp.sin(f * w)], axis=-1), t


def _pick(n, pref):
    t = min(n, pref)
    while n % t:
        t -= 1
    return t


def kernel(x, norm_mix_g, w_in, short_conv_w, short_conv_b, filt_w1, filt_b1, filt_w2, filt_b2, filt_w3, filt_b3, filt_freq, filt_w4, hyena_bias, sgu_ln_g, sgu_ln_b, sgu_w_s, sgu_b_s, w_branch_hyena, w_branch_sgu, w_out, norm_ffn_g, w_ffn_in, w_ffn_out, norm_final_g):
    B, L, D = x.shape
    depth = w_in.shape[0]
    n_ch = w_branch_hyena.shape[1]
    ds = w_branch_sgu.shape[1]
    assert n_ch == ds and 2 * n_ch == D and L % RADIX == 0 and depth >= 1
    n_mix = 3 * n_ch + 2 * ds
    M = B * L

    tabs, alt = _transform_tables(L)
    zfeat, t_col = _position_features(L, filt_w1.shape[1])
    max_decay = math.log(DECAY_TARGET) / FAST_DECAY
    min_decay = math.log(DECAY_TARGET) / SLOW_DECAY
    absdelta = jnp.abs(jnp.linspace(min_decay, max_decay, n_ch, dtype=F32))[None, :]

    ct = _pick(n_ch, 256)
    tm = _pick(M, 1024)
    tn = LANES * _pick(w_in.shape[2] // LANES, 1536 // LANES)

    x2 = x.reshape(M, D)
    for l in range(depth):
        spectra, w_in_bf16 = _hyena_filters(zfeat, t_col, absdelta, alt, tabs,
                                            filt_w1[l], filt_b1[l], filt_w2[l], filt_b2[l],
                                            filt_w3[l], filt_b3[l], filt_freq[l], filt_w4[l], ct, w_in[l])
        proj = _proj(x2, norm_mix_g[l].reshape(1, D), w_in_bf16, n_mix, tm, tn)
        y_hy, wbh, wbs, wo, wfi, wfo = _hyena(
            proj.reshape(B, L, proj.shape[1]), short_conv_w[l], short_conv_b[l], spectra,
            hyena_bias[l], alt, tabs, n_ch, ct,
            [w_branch_hyena[l], w_branch_sgu[l], w_out[l], w_ffn_in[l], w_ffn_out[l]])
        y_sgu = _sgu(proj, sgu_ln_g[l], sgu_ln_b[l], sgu_w_s[l], sgu_b_s[l], 3 * n_ch, ds, tm)
        x2 = _merge(x2, y_hy.reshape(M, n_ch), y_sgu, proj, n_mix, wbh, wbs, wo, _pick(M, 512))
        last = l == depth - 1
        x2 = _ffn(x2, norm_ffn_g[l].reshape(1, D), wfi, wfo, norm_final_g.reshape(1, D), last,
                  _pick(M, 1024), _pick(w_ffn_out.shape[1], 512))
    return x2.reshape(B, L, D)
```

```python
import functools
import math

import numpy as np
import jax
import jax.numpy as jnp
from jax import lax
from jax.experimental import pallas as pl
from jax.experimental.pallas import tpu as pltpu

F32 = jnp.float32
BF16 = jnp.bfloat16

EPS = 1e-6
FAST_DECAY = 0.3
SLOW_DECAY = 1.5
DECAY_TARGET = 1e-2
MOD_SHIFT = 0.0
N_DIR = 2
HYENA_ORDER = 2
RADIX = 4
SQH = math.sqrt(0.5)
LANES = 128
BF16_ROWS = 16
MIB = 1024 * 1024


def _params(vmem_mib, n_grid):
    return pltpu.CompilerParams(
        dimension_semantics=("arbitrary",) * n_grid,
        vmem_limit_bytes=vmem_mib * MIB)


def _resident(block_shape, index_map):
    return pl.BlockSpec(block_shape, index_map, pipeline_mode=pl.Buffered(1))


def _rms(x, g):
    ms = jnp.mean(x * x, axis=-1, keepdims=True)
    return x * lax.rsqrt(ms + EPS) * g


def _dot(a, b):
    return jnp.dot(a, b, preferred_element_type=F32)


def _sigmoid(x):
    return 0.5 * jnp.tanh(0.5 * x) + 0.5


def _colsum(a):
    return jnp.sum(a, axis=0, keepdims=True)


def _cmul(xr, xi, kr, ki):
    return xr * kr - xi * ki, xr * ki + xi * kr


def _special_bins(t):
    d, s = SQH * (t[1] - t[3]), SQH * (t[1] + t[3])
    return (t[0] + d, t[2] + s), (t[0] - d, s - t[2])


def _filter_feat_kernel(z_ref, w1_ref, b1_ref, w2_ref, b2_ref, w3_ref, b3_ref, fr_ref, o_ref):
    hp = lax.Precision.HIGHEST
    fr = fr_ref[...]
    h = jnp.sin(fr * (jnp.dot(z_ref[...], w1_ref[...], precision=hp, preferred_element_type=F32) + b1_ref[...]))
    h = jnp.sin(fr * (jnp.dot(h, w2_ref[...], precision=hp, preferred_element_type=F32) + b2_ref[...]))
    h = jnp.sin(fr * (jnp.dot(h, w3_ref[...], precision=hp, preferred_element_type=F32) + b3_ref[...]))
    o_ref[...] = h


def _filter_spec_kernel(f_ref, w4f_ref, w4b_ref, t_ref, ad_ref, bias_ref, alt_ref, tab_ref, wsrc_ref,
                        sp_ref, kq_ref, wdst_ref):
    wdst_ref[...] = wsrc_ref[...].astype(wdst_ref.dtype)
    Q = f_ref.shape[1]
    inv_n = 1.0 / (2 * RADIX * Q)
    ad = ad_ref[...]
    row = lax.broadcasted_iota(jnp.int32, (Q, ad.shape[1]), 0)

    def split(x):
        hi = x.astype(BF16)
        return hi, (x - hi.astype(F32)).astype(BF16)

    def dot_split(a, b):
        return ((_dot(a[1], b[1]) + _dot(a[1], b[0])) + _dot(a[0], b[1])) + _dot(a[0], b[0])

    w4f, w4b = split(w4f_ref[...]), split(w4b_ref[...])
    hf, hb = [], []
    for r in range(RADIX):
        f = split(f_ref[r])
        decay = jnp.exp(-t_ref[r] * ad) + MOD_SHIFT
        hf.append(dot_split(f, w4f) * decay)
        hb.append(dot_split(f, w4b) * decay)
    hb[0] = jnp.where(row == 0, 0.0, hb[0])
    norm = sum(_colsum(jnp.abs(h)) for h in hf + hb)
    inv = 1.0 / norm
    ev = [(hf[r] + hb[r]) * inv for r in range(RADIX)]
    od = [(hf[r] - hb[r]) * inv for r in range(RADIX)]
    evb = [e.astype(BF16) for e in ev]
    odb = [o.astype(BF16) for o in od]

    def cos_t(r, zb):
        return _dot(tab_ref[r], zb[r])

    def sin_t(r, zb):
        return _dot(tab_ref[RADIX + r], zb[r])

    a0, a2 = cos_t(0, evb), cos_t(2, evb)
    e, f_ = a0 + a2, a0 - a2
    g = cos_t(1, evb) + cos_t(3, evb)
    v = sin_t(1, evb) - sin_t(3, evb)
    b0, b2 = sin_t(0, odb), sin_t(2, odb)
    p, q = b0 + b2, b0 - b2
    u = sin_t(1, odb) + sin_t(3, odb)
    h = cos_t(1, odb) - cos_t(3, odb)
    first = row[:, :1] == 0
    w_end = jnp.where(first, inv_n, 2.0 * inv_n)
    w_dup = jnp.where(first, 0.0, 2.0 * inv_n)
    w_mid = 2.0 * inv_n
    bias = bias_ref[...]
    for s, (kr, ki, w) in enumerate([(e + g, p + u, w_end), (e - g, u - p, w_end),
                                     (f_ + v, h - q, w_mid), (f_ - v, q + h, w_dup)]):
        sp_ref[s] = (kr + bias) * w
        sp_ref[RADIX + s] = ki * w
    alt = alt_ref[...]
    (krq, _), (kr3, _) = _special_bins([_colsum(x * alt) for x in ev])
    (_, kiq), (_, ki3) = _special_bins([_colsum(x * alt) for x in od])
    for i, kval in enumerate((krq + bias, kiq, kr3 + bias, ki3)):
        kq_ref[i:i + 1, :] = kval * w_mid


def _hyena_filters(zfeat, t_col, absdelta, alt, tabs, w1, b1, w2, b2, w3, b3, freq, w4, hbias, tc, w_cast):
    L = zfeat.shape[0]
    Q = L // RADIX
    c2 = w4.shape[1] // N_DIR
    n_ch = c2 // HYENA_ORDER

    def pad2(a, r, c):
        return jnp.pad(a.astype(F32), ((0, r - a.shape[0]), (0, c - a.shape[1])))

    feat = pl.pallas_call(
        _filter_feat_kernel,
        out_shape=jax.ShapeDtypeStruct((L, LANES), F32),
        name="filter_feat",
        compiler_params=_params(32, 0),
    )(pad2(zfeat, L, LANES),
      pad2(w1, LANES, LANES), pad2(b1[None], 1, LANES),
      pad2(w2, LANES, LANES), pad2(b2[None], 1, LANES),
      pad2(w3, LANES, LANES), pad2(b3[None], 1, LANES), pad2(freq[None], 1, LANES))

    w4p = pad2(w4, LANES, w4.shape[1])
    nt = c2 // tc
    nct = n_ch // tc
    slab = w_cast.shape[0] // nt
    assert slab * nt == w_cast.shape[0] and slab % BF16_ROWS == 0
    cast_spec = pl.BlockSpec((slab, w_cast.shape[1]), lambda j: (j, 0))
    sp, kq, w_bf16 = pl.pallas_call(
        _filter_spec_kernel,
        grid=(nt,),
        in_specs=[
            _resident((RADIX, Q, LANES), lambda j: (0, 0, 0)),
            pl.BlockSpec((LANES, tc), lambda j: (0, j)),
            pl.BlockSpec((LANES, tc), lambda j: (0, nt + j)),
            _resident((RADIX, Q, 1), lambda j: (0, 0, 0)),
            pl.BlockSpec((1, tc), lambda j: (0, j % nct)),
            pl.BlockSpec((1, tc), lambda j: (0, j)),
            _resident((Q, 1), lambda j: (0, 0)),
            _resident((2 * RADIX, Q, Q), lambda j: (0, 0, 0)),
            cast_spec,
        ],
        out_specs=[pl.BlockSpec((2 * RADIX, Q, tc), lambda j: (0, 0, j)),
                   pl.BlockSpec((RADIX, tc), lambda j: (0, j)), cast_spec],
        out_shape=[jax.ShapeDtypeStruct((2 * RADIX, Q, c2), F32),
                   jax.ShapeDtypeStruct((RADIX, c2), F32),
                   jax.ShapeDtypeStruct(w_cast.shape, BF16)],
        name="filter_spec",
        compiler_params=_params(58, 1),
    )(feat.reshape(RADIX, Q, LANES), w4p, w4p, t_col.reshape(RADIX, Q, 1), absdelta,
      hbias.reshape(1, c2).astype(F32), alt, tabs, w_cast)
    return (sp, kq), w_bf16


def _proj_kernel(x_ref, g_ref, w_ref, o_ref, h_ref, *, n_mix):
    j = pl.program_id(1)
    tn = o_ref.shape[1]
    has_gates = (j + 1) * tn > n_mix

    def with_gates(acc):
        col = j * tn + lax.broadcasted_iota(jnp.int32, acc.shape, 1)
        return jnp.where(col >= n_mix, _sigmoid(acc), acc)

    @pl.when(j == 0)
    def _():
        h = _rms(x_ref[...], g_ref[...]).astype(BF16)
        h_ref[...] = h
        acc = _dot(h, w_ref[...])
        o_ref[...] = (with_gates(acc) if tn > n_mix else acc).astype(BF16)

    @pl.when((j > 0) & jnp.logical_not(has_gates))
    def _():
        o_ref[...] = _dot(h_ref[...], w_ref[...]).astype(BF16)

    @pl.when((j > 0) & has_gates)
    def _():
        o_ref[...] = with_gates(_dot(h_ref[...], w_ref[...])).astype(BF16)


def _proj(x2, g, w, n_mix, tm, tn):
    M, D = x2.shape
    E = w.shape[1]
    return pl.pallas_call(
        functools.partial(_proj_kernel, n_mix=n_mix),
        grid=(M // tm, E // tn),
        in_specs=[
            pl.BlockSpec((tm, D), lambda i, j: (i, 0)),
            _resident((1, D), lambda i, j: (0, 0)),
            pl.BlockSpec((D, tn), lambda i, j: (0, j)),
        ],
        out_specs=pl.BlockSpec((tm, tn), lambda i, j: (i, j)),
        out_shape=jax.ShapeDtypeStruct((M, E), BF16),
        scratch_shapes=[pltpu.VMEM((tm, D), BF16)],
        name="proj",
        compiler_params=_params(56, 2),
    )(x2, g, w)


def _short_conv_rows(p_ref, il_ref, w_ref, b_ref, r0, rc):
    L = p_ref.shape[1]
    t0, t1 = RADIX * r0, RADIX * (r0 + rc)
    parts = [[] for _ in range(RADIX)]
    for c in range(il_ref.shape[0]):
        il_ref[c] = p_ref[0, t0:t1, c * LANES:(c + 1) * LANES].astype(F32)
        for r in range(RADIX):
            parts[r].append(il_ref[c, pl.ds(r, rc, stride=RADIX), :])
    p = [jnp.concatenate(x, axis=1) for x in parts]
    row = lax.broadcasted_iota(jnp.int32, p[0].shape, 0)
    if t0 == 0:
        before = 0.0
    else:
        before = p_ref[0, t0 - BF16_ROWS:t0, :].astype(F32)[BF16_ROWS - 1:BF16_ROWS, :]
    if t1 == L:
        after = 0.0
    else:
        after = p_ref[0, t1:t1 + BF16_ROWS, :].astype(F32)[0:1, :]
    last_prev = jnp.where(row == 0, before, pltpu.roll(p[RADIX - 1], 1, axis=0))
    first_next = jnp.where(row == rc - 1, after, pltpu.roll(p[0], rc - 1, axis=0))
    prev = [last_prev] + p[:-1]
    nxt = p[1:] + [first_next]
    w0, w1, w2, b = w_ref[0:1, :], w_ref[1:2, :], w_ref[2:3, :], b_ref[...]
    return [prev[r] * w0 + p[r] * w1 + nxt[r] * w2 + b for r in range(RADIX)]


def _hyena_kernel(v_ref, x1_ref, x2_ref, wv_ref, w1_ref, w2_ref, bv_ref, b1_ref, b2_ref,
                  sp0_ref, kq0_ref, sp1_ref, kq1_ref, alt_ref, tab_ref, *refs, rc, n_cast):
    o_ref = refs[n_cast]
    zb_ref, p_ref, il_ref = refs[2 * n_cast + 1:]
    for src, dst in zip(refs[:n_cast], refs[n_cast + 1:2 * n_cast + 1]):
        dst[...] = src[...].astype(dst.dtype)
    Q = tab_ref.shape[1]
    R = RADIX
    chunks = [(r0, slice(r0, r0 + rc)) for r0 in range(0, Q, rc)]

    def put_z(zs, rows):
        alt = alt_ref[rows, :]
        for r in range(R):
            zb_ref[r, rows, :] = zs[r].astype(BF16)
        return [_colsum(z * alt) for z in zs]

    def spectrum_product(sp_ref):
        zb = [zb_ref[r] for r in range(R)]
        for _, rows in chunks:
            a = [_dot(tab_ref[r, rows, :], zb[r]) for r in range(R)]
            b = [_dot(tab_ref[R + r, rows, :], zb[r]) for r in range(R)]
            e, f = a[0] + a[2], a[0] - a[2]
            g, h = a[1] + a[3], a[1] - a[3]
            p, q = b[0] + b[2], b[0] - b[2]
            u, v = b[1] + b[3], b[1] - b[3]
            xr = [e + g, e - g, f + v, f - v]
            xi = [p + u, u - p, h - q, q + h]
            pr, pi = zip(*[_cmul(xr[s], xi[s], sp_ref[s, rows, :], sp_ref[R + s, rows, :])
                           for s in range(R)])
            s01p, s01m, s23p, s23m = pr[0] + pr[1], pr[0] - pr[1], pr[2] + pr[3], pr[2] - pr[3]
            t01p, t01m, t23p, t23m = pi[0] + pi[1], pi[0] - pi[1], pi[2] + pi[3], pi[2] - pi[3]
            for r, val in enumerate([s01p + s23p, s01m + t23p, s01p - s23p, s01m - t23p,
                                     t01m - t23m, t01p + s23m, t01m + t23m, t01p - s23m]):
                p_ref[r, rows, :] = val.astype(BF16)

    def conv_rows(rows, t, kq_ref):
        (xrq, xiq), (xr3, xi3) = _special_bins(t)
        prq, piq = _cmul(xrq, xiq, kq_ref[0:1, :], kq_ref[1:2, :])
        pr3, pi3 = _cmul(xr3, xi3, kq_ref[2:3, :], kq_ref[3:4, :])
        sp = [prq + pr3, SQH * (prq + piq + pi3 - pr3), piq - pi3, SQH * (piq - prq + pr3 + pi3)]
        alt = alt_ref[rows, :]
        return [_dot(tab_ref[2 * R + r, rows, :], p_ref[r]) + _dot(tab_ref[3 * R + r, rows, :], p_ref[R + r])
                + alt * sp[r] for r in range(R)]

    def add(acc, new):
        return new if acc is None else [x + y for x, y in zip(acc, new)]

    t = None
    for r0, rows in chunks:
        t = add(t, put_z(_short_conv_rows(v_ref, il_ref, wv_ref, bv_ref, r0, rc), rows))
    spectrum_product(sp0_ref)
    t1 = None
    for r0, rows in chunks:
        y = conv_rows(rows, t, kq0_ref)
        gate = _short_conv_rows(x1_ref, il_ref, w1_ref, b1_ref, r0, rc)
        t1 = add(t1, put_z([gate[r] * y[r] for r in range(R)], rows))
    spectrum_product(sp1_ref)
    for r0, rows in chunks:
        y = conv_rows(rows, t1, kq1_ref)
        gate = _short_conv_rows(x2_ref, il_ref, w2_ref, b2_ref, r0, rc)
        out = [gate[r] * y[r] for r in range(R)]
        for c in range(il_ref.shape[0]):
            lanes = slice(c * LANES, (c + 1) * LANES)
            for r in range(R):
                il_ref[c, pl.ds(r, rc, stride=R), :] = out[r][:, lanes]
            o_ref[0, R * r0:R * (r0 + rc), lanes] = il_ref[c].astype(o_ref.dtype)


def _hyena(p3, conv_w, conv_b, spectra, alt, tabs, n_ch, ct, to_cast):
    B, L, _ = p3.shape
    Q = L // RADIX
    nct = n_ch // ct
    rc = _pick(Q, 512)
    sp, kq = spectra
    cb = conv_b.reshape(1, -1).astype(F32)
    cw = conv_w.astype(F32)

    def pspec(g):
        return pl.BlockSpec((1, L, ct), lambda j, b, g=g: (b, 0, g * nct + j))

    def rowspec(rows, g):
        return pl.BlockSpec((rows, ct), lambda j, b, g=g: (0, g * nct + j))

    def specspec(g):
        return _resident((2 * RADIX, Q, ct), lambda j, b, g=g: (0, 0, g * nct + j))

    steps = nct * B
    cast_specs = []
    for w in to_cast:
        slab = w.shape[0] // steps
        assert slab * steps == w.shape[0] and slab % BF16_ROWS == 0, w.shape
        cast_specs.append(pl.BlockSpec((slab, w.shape[1]), lambda j, b: (j * B + b, 0)))

    return pl.pallas_call(
        functools.partial(_hyena_kernel, rc=rc, n_cast=len(to_cast)),
        grid=(nct, B),
        in_specs=[
            pspec(0), pspec(1), pspec(2),
            rowspec(3, 0), rowspec(3, 1), rowspec(3, 2),
            rowspec(1, 0), rowspec(1, 1), rowspec(1, 2),
            specspec(0), rowspec(RADIX, 0), specspec(1), rowspec(RADIX, 1),
            _resident((Q, 1), lambda j, b: (0, 0)),
            _resident(tabs.shape, lambda j, b: (0, 0, 0)),
        ] + cast_specs,
        out_specs=[pl.BlockSpec((1, L, ct), lambda j, b: (b, 0, j))] + cast_specs,
        out_shape=[jax.ShapeDtypeStruct((B, L, n_ch), BF16)]
                  + [jax.ShapeDtypeStruct(w.shape, BF16) for w in to_cast],
        scratch_shapes=[pltpu.VMEM((RADIX, Q, ct), BF16),
                        pltpu.VMEM((2 * RADIX, Q, ct), BF16),
                        pltpu.VMEM((ct // LANES, RADIX * rc, LANES), F32)],
        name="hyena",
        compiler_params=_params(58, 2),
    )(p3, p3, p3, cw, cw, cw, cb, cb, cb, sp, kq, sp, kq, alt, tabs, *to_cast)


def _gelu(x):
    return 0.5 * x * (1.0 + lax.erf(x * math.sqrt(0.5)))


def _sgu_kernel(u_ref, v_ref, g_ref, b_ref, ws_ref, bs_ref, o_ref, *, heads, chunk):
    u = _gelu(u_ref[...].astype(F32))
    v = _gelu(v_ref[...].astype(F32))
    mu = jnp.mean(v, axis=-1, keepdims=True)
    vc = v - mu
    var = jnp.mean(vc * vc, axis=-1, keepdims=True)
    vn = (vc * lax.rsqrt(var + EPS) * g_ref[...] + b_ref[...]).astype(BF16)
    tm, ds = vn.shape
    hd = ds // heads
    for n in range(tm // chunk):
        rows = slice(n * chunk, (n + 1) * chunk)
        for h in range(heads):
            cols = slice(h * hd, (h + 1) * hd)
            s = _dot(ws_ref[h], vn[rows, cols]) + bs_ref[:, h:h + 1]
            o_ref[rows, cols] = (u[rows, cols] * s).astype(o_ref.dtype)


def _sgu(p2, ln_g, ln_b, w_s, b_s, col0, ds, tm):
    M = p2.shape[0]
    heads, chunk, _ = w_s.shape
    cb = col0 // ds
    full2 = lambda i: (0, 0)
    return pl.pallas_call(
        functools.partial(_sgu_kernel, heads=heads, chunk=chunk),
        grid=(M // tm,),
        in_specs=[
            pl.BlockSpec((tm, ds), lambda i: (i, cb)),
            pl.BlockSpec((tm, ds), lambda i: (i, cb + 1)),
            _resident((1, ds), full2),
            _resident((1, ds), full2),
            _resident((heads, chunk, chunk), lambda i: (0, 0, 0)),
            _resident((chunk, heads), full2),
        ],
        out_specs=pl.BlockSpec((tm, ds), lambda i: (i, 0)),
        out_shape=jax.ShapeDtypeStruct((M, ds), BF16),
        name="sgu",
        compiler_params=_params(48, 1),
    )(p2, p2, ln_g.reshape(1, ds).astype(F32), ln_b.reshape(1, ds).astype(F32),
      w_s.astype(BF16), b_s.T.astype(F32))


def _merge_kernel(x_ref, yh_ref, ys_ref, gh0_ref, gh1_ref, gs0_ref, gs1_ref,
                  wbh_ref, wbs_ref, wo_ref, o_ref):
    gh = jnp.concatenate([gh0_ref[...], gh1_ref[...]], axis=1).astype(F32)
    gs = jnp.concatenate([gs0_ref[...], gs1_ref[...]], axis=1).astype(F32)
    a = _dot(yh_ref[...], wbh_ref[...]) * gh + _dot(ys_ref[...], wbs_ref[...]) * gs
    o_ref[...] = x_ref[...] + _dot(a.astype(BF16), wo_ref[...])


def _merge(x2, yh, ys, proj, gate_col, wbh, wbs, wo, tm):
    M, D = x2.shape
    gw = D // 2
    gb = gate_col // gw
    assert gate_col % gw == 0
    full = lambda i: (0, 0)

    def gate(c):
        return pl.BlockSpec((tm, gw), lambda i, c=c: (i, gb + c))

    return pl.pallas_call(
        _merge_kernel,
        grid=(M // tm,),
        in_specs=[
            pl.BlockSpec((tm, D), lambda i: (i, 0)),
            pl.BlockSpec((tm, yh.shape[1]), lambda i: (i, 0)),
            pl.BlockSpec((tm, ys.shape[1]), lambda i: (i, 0)),
            gate(0), gate(1), gate(2), gate(3),
            _resident(wbh.shape, full),
            _resident(wbs.shape, full),
            _resident(wo.shape, full),
        ],
        out_specs=pl.BlockSpec((tm, D), lambda i: (i, 0)),
        out_shape=jax.ShapeDtypeStruct((M, D), F32),
        name="merge",
        compiler_params=_params(56, 1),
    )(x2, yh, ys, proj, proj, proj, proj, wbh, wbs, wo)


def _ffn_kernel(x_ref, g_ref, wg_ref, wu_ref, wo_ref, gf_ref, o_ref, h_ref, *, final_norm):
    f = pl.program_id(1)

    def tile(h):
        gate = _dot(h, wg_ref[...])
        up = _dot(h, wu_ref[...])
        a = (gate * _sigmoid(gate) * up).astype(BF16)
        return _dot(a, wo_ref[...])

    @pl.when(f == 0)
    def _():
        x = x_ref[...]
        h = _rms(x, g_ref[...]).astype(BF16)
        h_ref[...] = h
        o_ref[...] = x + tile(h)

    @pl.when(f > 0)
    def _():
        o_ref[...] += tile(h_ref[...])

    if final_norm:
        @pl.when(f == pl.num_programs(1) - 1)
        def _():
            o_ref[...] = _rms(o_ref[...], gf_ref[...])


def _ffn(x2, g, w_in, w_out, g_final, final_norm, tm, tf):
    M, D = x2.shape
    FF = w_out.shape[0]
    nf = FF // tf
    full = lambda i, f: (0, 0)
    return pl.pallas_call(
        functools.partial(_ffn_kernel, final_norm=final_norm),
        grid=(M // tm, nf),
        in_specs=[
            pl.BlockSpec((tm, D), lambda i, f: (i, 0)),
            _resident((1, D), full),
            pl.BlockSpec((D, tf), lambda i, f: (0, f)),
            pl.BlockSpec((D, tf), lambda i, f: (0, nf + f)),
            pl.BlockSpec((tf, D), lambda i, f: (f, 0)),
            _resident((1, D), full),
        ],
        out_specs=pl.BlockSpec((tm, D), lambda i, f: (i, 0)),
        out_shape=jax.ShapeDtypeStruct((M, D), F32),
        scratch_shapes=[pltpu.VMEM((tm, D), BF16)],
        name="ffn",
        compiler_params=_params(60, 2),
    )(x2, g, w_in, w_in, w_out, g_final)


def _transform_tables(L):
    Q = L // RADIX
    k = np.arange(Q, dtype=np.int64)[:, None]
    m = np.arange(Q, dtype=np.int64)[None, :]

    def tab(fn, r):
        return fn(((k * (RADIX * m + r)) % (2 * L)) * (np.pi / L)).astype(np.float32)

    cs = [tab(np.cos, r) for r in range(RADIX)] + [tab(np.sin, r) for r in range(RADIX)]
    stack = np.stack(cs + [t.T for t in cs], axis=0)
    alt = (1.0 - 2.0 * (np.arange(Q) % 2)).astype(np.float32)[:, None]
    return jnp.asarray(stack).astype(BF16), jnp.asarray(alt)


def _position_features(L, emb):
    Q = L // RADIX
    row = jnp.arange(L, dtype=jnp.int32)
    n = ((row % Q) * RADIX + row // Q).astype(F32)[:, None]
    t = n / (L - 1)
    bands = (emb - 1) // 2
    w = 2.0 * math.pi * n / L
    f = jnp.linspace(1e-4, bands - 1, bands, dtype=F32)[None, :]
    return jnp.concatenate([t, jnp.cos(f * w), -jnp.sin(f * w)], axis=-1), t


def _pick(n, pref):
    t = min(n, pref)
    while n % t:
        t -= 1
    return t


def kernel(x, norm_mix_g, w_in, short_conv_w, short_conv_b, filt_w1, filt_b1, filt_w2, filt_b2, filt_w3, filt_b3, filt_freq, filt_w4, hyena_bias, sgu_ln_g, sgu_ln_b, sgu_w_s, sgu_b_s, w_branch_hyena, w_branch_sgu, w_out, norm_ffn_g, w_ffn_in, w_ffn_out, norm_final_g):
    B, L, D = x.shape
    depth = w_in.shape[0]
    n_ch = w_branch_hyena.shape[1]
    ds = w_branch_sgu.shape[1]
    assert n_ch == ds and 2 * n_ch == D and L % RADIX == 0 and depth >= 1
    n_mix = 3 * n_ch + 2 * ds
    M = B * L

    tabs, alt = _transform_tables(L)
    zfeat, t_col = _position_features(L, filt_w1.shape[1])
    max_decay = math.log(DECAY_TARGET) / FAST_DECAY
    min_decay = math.log(DECAY_TARGET) / SLOW_DECAY
    absdelta = jnp.abs(jnp.linspace(min_decay, max_decay, n_ch, dtype=F32))[None, :]

    ct = _pick(n_ch, 256)
    tm = _pick(M, 1024)
    tn = LANES * _pick(w_in.shape[2] // LANES, 1536 // LANES)

    x2 = x.reshape(M, D)
    for l in range(depth):
        spectra, w_in_bf16 = _hyena_filters(zfeat, t_col, absdelta, alt, tabs,
                                            filt_w1[l], filt_b1[l], filt_w2[l], filt_b2[l],
                                            filt_w3[l], filt_b3[l], filt_freq[l], filt_w4[l],
                                            hyena_bias[l], ct, w_in[l])
        proj = _proj(x2, norm_mix_g[l].reshape(1, D), w_in_bf16, n_mix, tm, tn)
        y_hy, wbh, wbs, wo, wfi, wfo = _hyena(
            proj.reshape(B, L, proj.shape[1]), short_conv_w[l], short_conv_b[l], spectra,
            alt, tabs, n_ch, ct,
            [w_branch_hyena[l], w_branch_sgu[l], w_out[l], w_ffn_in[l], w_ffn_out[l]])
        y_sgu = _sgu(proj, sgu_ln_g[l], sgu_ln_b[l], sgu_w_s[l], sgu_b_s[l], 3 * n_ch, ds, tm)
        x2 = _merge(x2, y_hy.reshape(M, n_ch), y_sgu, proj, n_mix, wbh, wbs, wo, _pick(M, 512))
        last = l == depth - 1
        x2 = _ffn(x2, norm_ffn_g[l].reshape(1, D), wfi, wfo, norm_final_g.reshape(1, D), last,
                  _pick(M, 1024), _pick(w_ffn_out.shape[1], 512))
    return x2.reshape(B, L, D)
```

```python
import functools
import math

import numpy as np
import jax
import jax.numpy as jnp
from jax import lax
from jax.experimental import pallas as pl
from jax.experimental.pallas import tpu as pltpu

F32 = jnp.float32
BF16 = jnp.bfloat16

EPS = 1e-6
FAST_DECAY = 0.3
SLOW_DECAY = 1.5
DECAY_TARGET = 1e-2
MOD_SHIFT = 0.0
N_DIR = 2
HYENA_ORDER = 2
RADIX = 4
SQH = math.sqrt(0.5)
LANES = 128
BF16_ROWS = 16
MIB = 1024 * 1024


def _params(vmem_mib, n_grid):
    return pltpu.CompilerParams(
        dimension_semantics=("arbitrary",) * n_grid,
        vmem_limit_bytes=vmem_mib * MIB)


def _resident(block_shape, index_map):
    return pl.BlockSpec(block_shape, index_map, pipeline_mode=pl.Buffered(1))


def _rms(x, g):
    ms = jnp.mean(x * x, axis=-1, keepdims=True)
    return x * lax.rsqrt(ms + EPS) * g


def _dot(a, b):
    return jnp.dot(a, b, preferred_element_type=F32)


def _sigmoid(x):
    return 0.5 * jnp.tanh(0.5 * x) + 0.5


def _colsum(a):
    return jnp.sum(a, axis=0, keepdims=True)


def _cmul(xr, xi, kr, ki):
    return xr * kr - xi * ki, xr * ki + xi * kr


def _special_bins(t):
    d, s = SQH * (t[1] - t[3]), SQH * (t[1] + t[3])
    return (t[0] + d, t[2] + s), (t[0] - d, s - t[2])


def _filter_feat_kernel(z_ref, w1_ref, b1_ref, w2_ref, b2_ref, w3_ref, b3_ref, fr_ref, o_ref):
    hp = lax.Precision.HIGHEST
    fr = fr_ref[...]
    h = jnp.sin(fr * (jnp.dot(z_ref[...], w1_ref[...], precision=hp, preferred_element_type=F32) + b1_ref[...]))
    h = jnp.sin(fr * (jnp.dot(h, w2_ref[...], precision=hp, preferred_element_type=F32) + b2_ref[...]))
    h = jnp.sin(fr * (jnp.dot(h, w3_ref[...], precision=hp, preferred_element_type=F32) + b3_ref[...]))
    o_ref[...] = h


def _filter_spec_kernel(f_ref, w4f_ref, w4b_ref, t_ref, ad_ref, bias_ref, alt_ref, tab_ref, wsrc_ref,
                        sp_ref, kq_ref, wdst_ref):
    wdst_ref[...] = wsrc_ref[...].astype(wdst_ref.dtype)
    Q = f_ref.shape[1]
    inv_n = 1.0 / (2 * RADIX * Q)
    ad = ad_ref[...]
    row = lax.broadcasted_iota(jnp.int32, (Q, ad.shape[1]), 0)

    def split(x):
        hi = x.astype(BF16)
        return hi, (x - hi.astype(F32)).astype(BF16)

    def dot_split(a, b):
        return ((_dot(a[1], b[1]) + _dot(a[1], b[0])) + _dot(a[0], b[1])) + _dot(a[0], b[0])

    w4f, w4b = split(w4f_ref[...]), split(w4b_ref[...])
    hf, hb = [], []
    for r in range(RADIX):
        f = split(f_ref[r])
        decay = jnp.exp(-t_ref[r] * ad) + MOD_SHIFT
        hf.append(dot_split(f, w4f) * decay)
        hb.append(dot_split(f, w4b) * decay)
    hb[0] = jnp.where(row == 0, 0.0, hb[0])
    norm = sum(_colsum(jnp.abs(h)) for h in hf + hb)
    inv = 1.0 / norm
    ev = [(hf[r] + hb[r]) * inv for r in range(RADIX)]
    od = [(hf[r] - hb[r]) * inv for r in range(RADIX)]
    evb = [e.astype(BF16) for e in ev]
    odb = [o.astype(BF16) for o in od]

    def cos_t(r, zb):
        return _dot(tab_ref[r], zb[r])

    def sin_t(r, zb):
        return _dot(tab_ref[RADIX + r], zb[r])

    a0, a2 = cos_t(0, evb), cos_t(2, evb)
    e, f_ = a0 + a2, a0 - a2
    g = cos_t(1, evb) + cos_t(3, evb)
    v = sin_t(1, evb) - sin_t(3, evb)
    b0, b2 = sin_t(0, odb), sin_t(2, odb)
    p, q = b0 + b2, b0 - b2
    u = sin_t(1, odb) + sin_t(3, odb)
    h = cos_t(1, odb) - cos_t(3, odb)
    first = row[:, :1] == 0
    w_end = jnp.where(first, inv_n, 2.0 * inv_n)
    w_dup = jnp.where(first, 0.0, 2.0 * inv_n)
    w_mid = 2.0 * inv_n
    bias = bias_ref[...]
    for s, (kr, ki, w) in enumerate([(e + g, p + u, w_end), (e - g, u - p, w_end),
                                     (f_ + v, h - q, w_mid), (f_ - v, q + h, w_dup)]):
        sp_ref[s] = (kr + bias) * w
        sp_ref[RADIX + s] = ki * w
    alt = alt_ref[...]
    (krq, _), (kr3, _) = _special_bins([_colsum(x * alt) for x in ev])
    (_, kiq), (_, ki3) = _special_bins([_colsum(x * alt) for x in od])
    for i, kval in enumerate((krq + bias, kiq, kr3 + bias, ki3)):
        kq_ref[i:i + 1, :] = kval * w_mid


def _hyena_filters(zfeat, t_col, absdelta, alt, tabs, w1, b1, w2, b2, w3, b3, freq, w4, hbias, tc, w_cast):
    L = zfeat.shape[0]
    Q = L // RADIX
    c2 = w4.shape[1] // N_DIR
    n_ch = c2 // HYENA_ORDER

    def pad2(a, r, c):
        return jnp.pad(a.astype(F32), ((0, r - a.shape[0]), (0, c - a.shape[1])))

    feat = pl.pallas_call(
        _filter_feat_kernel,
        out_shape=jax.ShapeDtypeStruct((L, LANES), F32),
        name="filter_feat",
        compiler_params=_params(32, 0),
    )(pad2(zfeat, L, LANES),
      pad2(w1, LANES, LANES), pad2(b1[None], 1, LANES),
      pad2(w2, LANES, LANES), pad2(b2[None], 1, LANES),
      pad2(w3, LANES, LANES), pad2(b3[None], 1, LANES), pad2(freq[None], 1, LANES))

    w4p = pad2(w4, LANES, w4.shape[1])
    nt = c2 // tc
    nct = n_ch // tc
    slab = w_cast.shape[0] // nt
    assert slab * nt == w_cast.shape[0] and slab % BF16_ROWS == 0
    cast_spec = pl.BlockSpec((slab, w_cast.shape[1]), lambda j: (j, 0))
    sp, kq, w_bf16 = pl.pallas_call(
        _filter_spec_kernel,
        grid=(nt,),
        in_specs=[
            _resident((RADIX, Q, LANES), lambda j: (0, 0, 0)),
            pl.BlockSpec((LANES, tc), lambda j: (0, j)),
            pl.BlockSpec((LANES, tc), lambda j: (0, nt + j)),
            _resident((RADIX, Q, 1), lambda j: (0, 0, 0)),
            pl.BlockSpec((1, tc), lambda j: (0, j % nct)),
            pl.BlockSpec((1, tc), lambda j: (0, j)),
            _resident((Q, 1), lambda j: (0, 0)),
            _resident((2 * RADIX, Q, Q), lambda j: (0, 0, 0)),
            cast_spec,
        ],
        out_specs=[pl.BlockSpec((2 * RADIX, Q, tc), lambda j: (0, 0, j)),
                   pl.BlockSpec((RADIX, tc), lambda j: (0, j)), cast_spec],
        out_shape=[jax.ShapeDtypeStruct((2 * RADIX, Q, c2), F32),
                   jax.ShapeDtypeStruct((RADIX, c2), F32),
                   jax.ShapeDtypeStruct(w_cast.shape, BF16)],
        name="filter_spec",
        compiler_params=_params(58, 1),
    )(feat.reshape(RADIX, Q, LANES), w4p, w4p, t_col.reshape(RADIX, Q, 1), absdelta,
      hbias.reshape(1, c2).astype(F32), alt, tabs, w_cast)
    return (sp, kq), w_bf16


def _proj_kernel(x_ref, g_ref, w_ref, o_ref, h_ref, *, n_mix):
    j = pl.program_id(1)
    tn = o_ref.shape[1]
    has_gates = (j + 1) * tn > n_mix

    def with_gates(acc):
        col = j * tn + lax.broadcasted_iota(jnp.int32, acc.shape, 1)
        return jnp.where(col >= n_mix, _sigmoid(acc), acc)

    @pl.when(j == 0)
    def _():
        h = _rms(x_ref[...], g_ref[...]).astype(BF16)
        h_ref[...] = h
        acc = _dot(h, w_ref[...])
        o_ref[...] = (with_gates(acc) if tn > n_mix else acc).astype(BF16)

    @pl.when((j > 0) & jnp.logical_not(has_gates))
    def _():
        o_ref[...] = _dot(h_ref[...], w_ref[...]).astype(BF16)

    @pl.when((j > 0) & has_gates)
    def _():
        o_ref[...] = with_gates(_dot(h_ref[...], w_ref[...])).astype(BF16)


def _proj(x2, g, w, n_mix, tm, tn):
    M, D = x2.shape
    E = w.shape[1]
    return pl.pallas_call(
        functools.partial(_proj_kernel, n_mix=n_mix),
        grid=(M // tm, E // tn),
        in_specs=[
            pl.BlockSpec((tm, D), lambda i, j: (i, 0)),
            _resident((1, D), lambda i, j: (0, 0)),
            pl.BlockSpec((D, tn), lambda i, j: (0, j)),
        ],
        out_specs=pl.BlockSpec((tm, tn), lambda i, j: (i, j)),
        out_shape=jax.ShapeDtypeStruct((M, E), BF16),
        scratch_shapes=[pltpu.VMEM((tm, D), BF16)],
        name="proj",
        compiler_params=_params(56, 2),
    )(x2, g, w)


def _short_conv_rows(p_ref, il_ref, w_ref, b_ref, r0, rc):
    L = p_ref.shape[1]
    t0, t1 = RADIX * r0, RADIX * (r0 + rc)
    parts = [[] for _ in range(RADIX)]
    for c in range(il_ref.shape[0]):
        il_ref[c] = p_ref[0, t0:t1, c * LANES:(c + 1) * LANES].astype(F32)
        for r in range(RADIX):
            parts[r].append(il_ref[c, pl.ds(r, rc, stride=RADIX), :])
    p = [jnp.concatenate(x, axis=1) for x in parts]
    row = lax.broadcasted_iota(jnp.int32, p[0].shape, 0)
    if t0 == 0:
        before = 0.0
    else:
        before = p_ref[0, t0 - BF16_ROWS:t0, :].astype(F32)[BF16_ROWS - 1:BF16_ROWS, :]
    if t1 == L:
        after = 0.0
    else:
        after = p_ref[0, t1:t1 + BF16_ROWS, :].astype(F32)[0:1, :]
    last_prev = jnp.where(row == 0, before, pltpu.roll(p[RADIX - 1], 1, axis=0))
    first_next = jnp.where(row == rc - 1, after, pltpu.roll(p[0], rc - 1, axis=0))
    prev = [last_prev] + p[:-1]
    nxt = p[1:] + [first_next]
    w0, w1, w2, b = w_ref[0:1, :], w_ref[1:2, :], w_ref[2:3, :], b_ref[...]
    return [prev[r] * w0 + p[r] * w1 + nxt[r] * w2 + b for r in range(RADIX)]


def _hyena_kernel(v_ref, x1_ref, x2_ref, wv_ref, w1_ref, w2_ref, bv_ref, b1_ref, b2_ref,
                  sp0_ref, kq0_ref, sp1_ref, kq1_ref, alt_ref, tab_ref, *refs, rc, n_cast):
    o_ref = refs[n_cast]
    zb_ref, p_ref, il_ref = refs[2 * n_cast + 1:]
    for src, dst in zip(refs[:n_cast], refs[n_cast + 1:2 * n_cast + 1]):
        dst[...] = src[...].astype(dst.dtype)
    Q = tab_ref.shape[1]
    R = RADIX
    chunks = [(r0, slice(r0, r0 + rc)) for r0 in range(0, Q, rc)]

    def put_z(zs, rows):
        alt = alt_ref[rows, :]
        for r in range(R):
            zb_ref[r, rows, :] = zs[r].astype(BF16)
        return [_colsum(z * alt) for z in zs]

    def spectrum_product(sp_ref):
        zb = [zb_ref[r] for r in range(R)]
        for _, rows in chunks:
            a = [_dot(tab_ref[r, rows, :], zb[r]) for r in range(R)]
            b = [_dot(tab_ref[R + r, rows, :], zb[r]) for r in range(R)]
            e, f = a[0] + a[2], a[0] - a[2]
            g, h = a[1] + a[3], a[1] - a[3]
            p, q = b[0] + b[2], b[0] - b[2]
            u, v = b[1] + b[3], b[1] - b[3]
            xr = [e + g, e - g, f + v, f - v]
            xi = [p + u, u - p, h - q, q + h]
            pr, pi = zip(*[_cmul(xr[s], xi[s], sp_ref[s, rows, :], sp_ref[R + s, rows, :])
                           for s in range(R)])
            s01p, s01m, s23p, s23m = pr[0] + pr[1], pr[0] - pr[1], pr[2] + pr[3], pr[2] - pr[3]
            t01p, t01m, t23p, t23m = pi[0] + pi[1], pi[0] - pi[1], pi[2] + pi[3], pi[2] - pi[3]
            for r, val in enumerate([s01p + s23p, s01m + t23p, s01p - s23p, s01m - t23p,
                                     t01m - t23m, t01p + s23m, t01m + t23m, t01p - s23m]):
                p_ref[r, rows, :] = val.astype(BF16)

    def conv_rows(rows, t, kq_ref):
        (xrq, xiq), (xr3, xi3) = _special_bins(t)
        prq, piq = _cmul(xrq, xiq, kq_ref[0:1, :], kq_ref[1:2, :])
        pr3, pi3 = _cmul(xr3, xi3, kq_ref[2:3, :], kq_ref[3:4, :])
        sp = [prq + pr3, SQH * (prq + piq + pi3 - pr3), piq - pi3, SQH * (piq - prq + pr3 + pi3)]
        alt = alt_ref[rows, :]
        return [_dot(tab_ref[2 * R + r, rows, :], p_ref[r]) + _dot(tab_ref[3 * R + r, rows, :], p_ref[R + r])
                + alt * sp[r] for r in range(R)]

    def add(acc, new):
        return new if acc is None else [x + y for x, y in zip(acc, new)]

    t = None
    for r0, rows in chunks:
        t = add(t, put_z(_short_conv_rows(v_ref, il_ref, wv_ref, bv_ref, r0, rc), rows))
    spectrum_product(sp0_ref)
    t1 = None
    for r0, rows in chunks:
        y = conv_rows(rows, t, kq0_ref)
        gate = _short_conv_rows(x1_ref, il_ref, w1_ref, b1_ref, r0, rc)
        t1 = add(t1, put_z([gate[r] * y[r] for r in range(R)], rows))
    spectrum_product(sp1_ref)
    for r0, rows in chunks:
        y = conv_rows(rows, t1, kq1_ref)
        gate = _short_conv_rows(x2_ref, il_ref, w2_ref, b2_ref, r0, rc)
        out = [gate[r] * y[r] for r in range(R)]
        for c in range(il_ref.shape[0]):
            lanes = slice(c * LANES, (c + 1) * LANES)
            for r in range(R):
                il_ref[c, pl.ds(r, rc, stride=R), :] = out[r][:, lanes]
            o_ref[0, R * r0:R * (r0 + rc), lanes] = il_ref[c].astype(o_ref.dtype)


def _hyena(p3, conv_w, conv_b, spectra, alt, tabs, n_ch, ct, to_cast):
    B, L, _ = p3.shape
    Q = L // RADIX
    nct = n_ch // ct
    rc = _pick(Q, 512)
    sp, kq = spectra
    cb = conv_b.reshape(1, -1).astype(F32)
    cw = conv_w.astype(F32)

    def pspec(g):
        return pl.BlockSpec((1, L, ct), lambda j, b, g=g: (b, 0, g * nct + j))

    def rowspec(rows, g):
        return pl.BlockSpec((rows, ct), lambda j, b, g=g: (0, g * nct + j))

    def specspec(g):
        return _resident((2 * RADIX, Q, ct), lambda j, b, g=g: (0, 0, g * nct + j))

    steps = nct * B
    cast_specs = []
    for w in to_cast:
        slab = w.shape[0] // steps
        assert slab * steps == w.shape[0] and slab % BF16_ROWS == 0, w.shape
        cast_specs.append(pl.BlockSpec((slab, w.shape[1]), lambda j, b: (j * B + b, 0)))

    return pl.pallas_call(
        functools.partial(_hyena_kernel, rc=rc, n_cast=len(to_cast)),
        grid=(nct, B),
        in_specs=[
            pspec(0), pspec(1), pspec(2),
            rowspec(3, 0), rowspec(3, 1), rowspec(3, 2),
            rowspec(1, 0), rowspec(1, 1), rowspec(1, 2),
            specspec(0), rowspec(RADIX, 0), specspec(1), rowspec(RADIX, 1),
            _resident((Q, 1), lambda j, b: (0, 0)),
            _resident(tabs.shape, lambda j, b: (0, 0, 0)),
        ] + cast_specs,
        out_specs=[pl.BlockSpec((1, L, ct), lambda j, b: (b, 0, j))] + cast_specs,
        out_shape=[jax.ShapeDtypeStruct((B, L, n_ch), BF16)]
                  + [jax.ShapeDtypeStruct(w.shape, BF16) for w in to_cast],
        scratch_shapes=[pltpu.VMEM((RADIX, Q, ct), BF16),
                        pltpu.VMEM((2 * RADIX, Q, ct), BF16),
                        pltpu.VMEM((ct // LANES, RADIX * rc, LANES), F32)],
        name="hyena",
        compiler_params=_params(58, 2),
    )(p3, p3, p3, cw, cw, cw, cb, cb, cb, sp, kq, sp, kq, alt, tabs, *to_cast)


def _gelu(x):
    return 0.5 * x * (1.0 + lax.erf(x * math.sqrt(0.5)))


def _sgu_kernel(u_ref, v_ref, g_ref, b_ref, ws_ref, bs_ref, o_ref, *, heads, chunk):
    u = _gelu(u_ref[...].astype(F32))
    v = _gelu(v_ref[...].astype(F32))
    mu = jnp.mean(v, axis=-1, keepdims=True)
    vc = v - mu
    var = jnp.mean(vc * vc, axis=-1, keepdims=True)
    vn = (vc * lax.rsqrt(var + EPS) * g_ref[...] + b_ref[...]).astype(BF16)
    tm, ds = vn.shape
    hd = ds // heads
    for n in range(tm // chunk):
        rows = slice(n * chunk, (n + 1) * chunk)
        for h in range(heads):
            cols = slice(h * hd, (h + 1) * hd)
            s = _dot(ws_ref[h], vn[rows, cols]) + bs_ref[:, h:h + 1]
            o_ref[rows, cols] = (u[rows, cols] * s).astype(o_ref.dtype)


def _sgu(p2, ln_g, ln_b, w_s, b_s, col0, ds, tm):
    M = p2.shape[0]
    heads, chunk, _ = w_s.shape
    cb = col0 // ds
    full2 = lambda i: (0, 0)
    return pl.pallas_call(
        functools.partial(_sgu_kernel, heads=heads, chunk=chunk),
        grid=(M // tm,),
        in_specs=[
            pl.BlockSpec((tm, ds), lambda i: (i, cb)),
            pl.BlockSpec((tm, ds), lambda i: (i, cb + 1)),
            _resident((1, ds), full2),
            _resident((1, ds), full2),
            _resident((heads, chunk, chunk), lambda i: (0, 0, 0)),
            _resident((chunk, heads), full2),
        ],
        out_specs=pl.BlockSpec((tm, ds), lambda i: (i, 0)),
        out_shape=jax.ShapeDtypeStruct((M, ds), BF16),
        name="sgu",
        compiler_params=_params(48, 1),
    )(p2, p2, ln_g.reshape(1, ds).astype(F32), ln_b.reshape(1, ds).astype(F32),
      w_s.astype(BF16), b_s.T.astype(F32))


def _merge_kernel(x_ref, yh_ref, ys_ref, gh0_ref, gh1_ref, gs0_ref, gs1_ref,
                  wbh_ref, wbs_ref, wo_ref, o_ref):
    gh = jnp.concatenate([gh0_ref[...], gh1_ref[...]], axis=1).astype(F32)
    gs = jnp.concatenate([gs0_ref[...], gs1_ref[...]], axis=1).astype(F32)
    a = _dot(yh_ref[...], wbh_ref[...]) * gh + _dot(ys_ref[...], wbs_ref[...]) * gs
    o_ref[...] = x_ref[...] + _dot(a.astype(BF16), wo_ref[...])


def _merge(x2, yh, ys, proj, gate_col, wbh, wbs, wo, tm):
    M, D = x2.shape
    gw = D // 2
    gb = gate_col // gw
    assert gate_col % gw == 0
    full = lambda i: (0, 0)

    def gate(c):
        return pl.BlockSpec((tm, gw), lambda i, c=c: (i, gb + c))

    return pl.pallas_call(
        _merge_kernel,
        grid=(M // tm,),
        in_specs=[
            pl.BlockSpec((tm, D), lambda i: (i, 0)),
            pl.BlockSpec((tm, yh.shape[1]), lambda i: (i, 0)),
            pl.BlockSpec((tm, ys.shape[1]), lambda i: (i, 0)),
            gate(0), gate(1), gate(2), gate(3),
            _resident(wbh.shape, full),
            _resident(wbs.shape, full),
            _resident(wo.shape, full),
        ],
        out_specs=pl.BlockSpec((tm, D), lambda i: (i, 0)),
        out_shape=jax.ShapeDtypeStruct((M, D), F32),
        name="merge",
        compiler_params=_params(56, 1),
    )(x2, yh, ys, proj, proj, proj, proj, wbh, wbs, wo)


def _ffn_kernel(x_ref, g_ref, wg_ref, wu_ref, wo_ref, gf_ref, o_ref, h_ref, *, final_norm):
    f = pl.program_id(1)

    def tile(h):
        gate = _dot(h, wg_ref[...])
        up = _dot(h, wu_ref[...])
        a = (gate * _sigmoid(gate) * up).astype(BF16)
        return _dot(a, wo_ref[...])

    @pl.when(f == 0)
    def _():
        x = x_ref[...]
        h = _rms(x, g_ref[...]).astype(BF16)
        h_ref[...] = h
        o_ref[...] = x + tile(h)

    @pl.when(f > 0)
    def _():
        o_ref[...] += tile(h_ref[...])

    if final_norm:
        @pl.when(f == pl.num_programs(1) - 1)
        def _():
            o_ref[...] = _rms(o_ref[...], gf_ref[...])


def _ffn(x2, g, w_in, w_out, g_final, final_norm, tm, tf):
    M, D = x2.shape
    FF = w_out.shape[0]
    nf = FF // tf
    full = lambda i, f: (0, 0)
    return pl.pallas_call(
        functools.partial(_ffn_kernel, final_norm=final_norm),
        grid=(M // tm, nf),
        in_specs=[
            pl.BlockSpec((tm, D), lambda i, f: (i, 0)),
            _resident((1, D), full),
            pl.BlockSpec((D, tf), lambda i, f: (0, f)),
            pl.BlockSpec((D, tf), lambda i, f: (0, nf + f)),
            pl.BlockSpec((tf, D), lambda i, f: (f, 0)),
            _resident((1, D), full),
        ],
        out_specs=pl.BlockSpec((tm, D), lambda i, f: (i, 0)),
        out_shape=jax.ShapeDtypeStruct((M, D), F32),
        scratch_shapes=[pltpu.VMEM((tm, D), BF16)],
        name="ffn",
        compiler_params=_params(60, 2),
    )(x2, g, w_in, w_in, w_out, g_final)


def _transform_tables(L):
    Q = L // RADIX
    k = np.arange(Q, dtype=np.int64)[:, None]
    m = np.arange(Q, dtype=np.int64)[None, :]

    def tab(fn, r):
        return fn(((k * (RADIX * m + r)) % (2 * L)) * (np.pi / L)).astype(np.float32)

    cs = [tab(np.cos, r) for r in range(RADIX)] + [tab(np.sin, r) for r in range(RADIX)]
    stack = np.stack(cs + [t.T for t in cs], axis=0)
    alt = (1.0 - 2.0 * (np.arange(Q) % 2)).astype(np.float32)[:, None]
    return jnp.asarray(stack).astype(BF16), jnp.asarray(alt)


def _position_features(L, emb):
    Q = L // RADIX
    row = jnp.arange(L, dtype=jnp.int32)
    n = ((row % Q) * RADIX + row // Q).astype(F32)[:, None]
    t = n / (L - 1)
    bands = (emb - 1) // 2
    w = 2.0 * math.pi * n / L
    f = jnp.linspace(1e-4, bands - 1, bands, dtype=F32)[None, :]
    return jnp.concatenate([t, jnp.cos(f * w), -jnp.sin(f * w)], axis=-1), t


def _pick(n, pref):
    t = min(n, pref)
    while n % t:
        t -= 1
    return t


def kernel(x, norm_mix_g, w_in, short_conv_w, short_conv_b, filt_w1, filt_b1, filt_w2, filt_b2, filt_w3, filt_b3, filt_freq, filt_w4, hyena_bias, sgu_ln_g, sgu_ln_b, sgu_w_s, sgu_b_s, w_branch_hyena, w_branch_sgu, w_out, norm_ffn_g, w_ffn_in, w_ffn_out, norm_final_g):
    B, L, D = x.shape
    depth = w_in.shape[0]
    n_ch = w_branch_hyena.shape[1]
    ds = w_branch_sgu.shape[1]
    assert n_ch == ds and 2 * n_ch == D and L % RADIX == 0 and depth >= 1
    n_mix = 3 * n_ch + 2 * ds
    M = B * L

    tabs, alt = _transform_tables(L)
    zfeat, t_col = _position_features(L, filt_w1.shape[1])
    max_decay = math.log(DECAY_TARGET) / FAST_DECAY
    min_decay = math.log(DECAY_TARGET) / SLOW_DECAY
    absdelta = jnp.abs(jnp.linspace(min_decay, max_decay, n_ch, dtype=F32))[None, :]

    ct = _pick(n_ch, 256)
    tm = _pick(M, 1024)
    tn = LANES * _pick(w_in.shape[2] // LANES, 2304 // LANES)

    x2 = x.reshape(M, D)
    for l in range(depth):
        spectra, w_in_bf16 = _hyena_filters(zfeat, t_col, absdelta, alt, tabs,
                                            filt_w1[l], filt_b1[l], filt_w2[l], filt_b2[l],
                                            filt_w3[l], filt_b3[l], filt_freq[l], filt_w4[l],
                                            hyena_bias[l], ct, w_in[l])
        proj = _proj(x2, norm_mix_g[l].reshape(1, D), w_in_bf16, n_mix, tm, tn)
        y_hy, wbh, wbs, wo, wfi, wfo = _hyena(
            proj.reshape(B, L, proj.shape[1]), short_conv_w[l], short_conv_b[l], spectra,
            alt, tabs, n_ch, ct,
            [w_branch_hyena[l], w_branch_sgu[l], w_out[l], w_ffn_in[l], w_ffn_out[l]])
        y_sgu = _sgu(proj, sgu_ln_g[l], sgu_ln_b[l], sgu_w_s[l], sgu_b_s[l], 3 * n_ch, ds, tm)
        x2 = _merge(x2, y_hy.reshape(M, n_ch), y_sgu, proj, n_mix, wbh, wbs, wo, _pick(M, 512))
        last = l == depth - 1
        x2 = _ffn(x2, norm_ffn_g[l].reshape(1, D), wfi, wfo, norm_final_g.reshape(1, D), last,
                  _pick(M, 1024), _pick(w_ffn_out.shape[1], 512))
    return x2.reshape(B, L, D)
```

```python
import functools
import math

import numpy as np
import jax
import jax.numpy as jnp
from jax import lax
from jax.experimental import pallas as pl
from jax.experimental.pallas import tpu as pltpu

F32 = jnp.float32
BF16 = jnp.bfloat16

EPS = 1e-6
FAST_DECAY = 0.3
SLOW_DECAY = 1.5
DECAY_TARGET = 1e-2
MOD_SHIFT = 0.0
N_DIR = 2
HYENA_ORDER = 2
RADIX = 4
SQH = math.sqrt(0.5)
LANES = 128
BF16_ROWS = 16
MIB = 1024 * 1024


def _params(vmem_mib, n_grid):
    return pltpu.CompilerParams(
        dimension_semantics=("arbitrary",) * n_grid,
        vmem_limit_bytes=vmem_mib * MIB)


def _resident(block_shape, index_map):
    return pl.BlockSpec(block_shape, index_map, pipeline_mode=pl.Buffered(1))


def _rms(x, g):
    ms = jnp.mean(x * x, axis=-1, keepdims=True)
    return x * lax.rsqrt(ms + EPS) * g


def _dot(a, b):
    return jnp.dot(a, b, preferred_element_type=F32)


def _sigmoid(x):
    return 0.5 * jnp.tanh(0.5 * x) + 0.5


def _colsum(a):
    return jnp.sum(a, axis=0, keepdims=True)


def _cmul(xr, xi, kr, ki):
    return xr * kr - xi * ki, xr * ki + xi * kr


def _special_bins(t):
    d, s = SQH * (t[1] - t[3]), SQH * (t[1] + t[3])
    return (t[0] + d, t[2] + s), (t[0] - d, s - t[2])


def _filter_feat_kernel(z_ref, w1_ref, b1_ref, w2_ref, b2_ref, w3_ref, b3_ref, fr_ref, o_ref):
    hp = lax.Precision.HIGHEST
    fr = fr_ref[...]
    h = jnp.sin(fr * (jnp.dot(z_ref[...], w1_ref[...], precision=hp, preferred_element_type=F32) + b1_ref[...]))
    h = jnp.sin(fr * (jnp.dot(h, w2_ref[...], precision=hp, preferred_element_type=F32) + b2_ref[...]))
    h = jnp.sin(fr * (jnp.dot(h, w3_ref[...], precision=hp, preferred_element_type=F32) + b3_ref[...]))
    o_ref[...] = h


def _filter_spec_kernel(f_ref, w4f_ref, w4b_ref, t_ref, ad_ref, bias_ref, alt_ref, tab_ref, wsrc_ref,
                        sp_ref, kq_ref, wdst_ref):
    wdst_ref[...] = wsrc_ref[...].astype(wdst_ref.dtype)
    Q = f_ref.shape[1]
    inv_n = 1.0 / (2 * RADIX * Q)
    ad = ad_ref[...]
    row = lax.broadcasted_iota(jnp.int32, (Q, ad.shape[1]), 0)

    def split(x):
        hi = x.astype(BF16)
        return hi, (x - hi.astype(F32)).astype(BF16)

    def dot_split(a, b):
        return ((_dot(a[1], b[1]) + _dot(a[1], b[0])) + _dot(a[0], b[1])) + _dot(a[0], b[0])

    w4f, w4b = split(w4f_ref[...]), split(w4b_ref[...])
    hf, hb = [], []
    for r in range(RADIX):
        f = split(f_ref[r])
        decay = jnp.exp(-t_ref[r] * ad) + MOD_SHIFT
        hf.append(dot_split(f, w4f) * decay)
        hb.append(dot_split(f, w4b) * decay)
    hb[0] = jnp.where(row == 0, 0.0, hb[0])
    norm = sum(_colsum(jnp.abs(h)) for h in hf + hb)
    inv = 1.0 / norm
    ev = [(hf[r] + hb[r]) * inv for r in range(RADIX)]
    od = [(hf[r] - hb[r]) * inv for r in range(RADIX)]
    evb = [e.astype(BF16) for e in ev]
    odb = [o.astype(BF16) for o in od]

    def cos_t(r, zb):
        return _dot(tab_ref[r], zb[r])

    def sin_t(r, zb):
        return _dot(tab_ref[RADIX + r], zb[r])

    a0, a2 = cos_t(0, evb), cos_t(2, evb)
    e, f_ = a0 + a2, a0 - a2
    g = cos_t(1, evb) + cos_t(3, evb)
    v = sin_t(1, evb) - sin_t(3, evb)
    b0, b2 = sin_t(0, odb), sin_t(2, odb)
    p, q = b0 + b2, b0 - b2
    u = sin_t(1, odb) + sin_t(3, odb)
    h = cos_t(1, odb) - cos_t(3, odb)
    first = row[:, :1] == 0
    w_end = jnp.where(first, inv_n, 2.0 * inv_n)
    w_dup = jnp.where(first, 0.0, 2.0 * inv_n)
    w_mid = 2.0 * inv_n
    bias = bias_ref[...]
    for s, (kr, ki, w) in enumerate([(e + g, p + u, w_end), (e - g, u - p, w_end),
                                     (f_ + v, h - q, w_mid), (f_ - v, q + h, w_dup)]):
        sp_ref[s] = (kr + bias) * w
        sp_ref[RADIX + s] = ki * w
    alt = alt_ref[...]
    (krq, _), (kr3, _) = _special_bins([_colsum(x * alt) for x in ev])
    (_, kiq), (_, ki3) = _special_bins([_colsum(x * alt) for x in od])
    for i, kval in enumerate((krq + bias, kiq, kr3 + bias, ki3)):
        kq_ref[i:i + 1, :] = kval * w_mid


def _hyena_filters(zfeat, t_col, absdelta, alt, tabs, w1, b1, w2, b2, w3, b3, freq, w4, hbias, tc, w_cast):
    L = zfeat.shape[0]
    Q = L // RADIX
    c2 = w4.shape[1] // N_DIR
    n_ch = c2 // HYENA_ORDER

    def pad2(a, r, c):
        return jnp.pad(a.astype(F32), ((0, r - a.shape[0]), (0, c - a.shape[1])))

    feat = pl.pallas_call(
        _filter_feat_kernel,
        out_shape=jax.ShapeDtypeStruct((L, LANES), F32),
        name="filter_feat",
        compiler_params=_params(32, 0),
    )(pad2(zfeat, L, LANES),
      pad2(w1, LANES, LANES), pad2(b1[None], 1, LANES),
      pad2(w2, LANES, LANES), pad2(b2[None], 1, LANES),
      pad2(w3, LANES, LANES), pad2(b3[None], 1, LANES), pad2(freq[None], 1, LANES))

    w4p = pad2(w4, LANES, w4.shape[1])
    nt = c2 // tc
    nct = n_ch // tc
    slab = w_cast.shape[0] // nt
    assert slab * nt == w_cast.shape[0] and slab % BF16_ROWS == 0
    cast_spec = pl.BlockSpec((slab, w_cast.shape[1]), lambda j: (j, 0))
    sp, kq, w_bf16 = pl.pallas_call(
        _filter_spec_kernel,
        grid=(nt,),
        in_specs=[
            _resident((RADIX, Q, LANES), lambda j: (0, 0, 0)),
            pl.BlockSpec((LANES, tc), lambda j: (0, j)),
            pl.BlockSpec((LANES, tc), lambda j: (0, nt + j)),
            _resident((RADIX, Q, 1), lambda j: (0, 0, 0)),
            pl.BlockSpec((1, tc), lambda j: (0, j % nct)),
            pl.BlockSpec((1, tc), lambda j: (0, j)),
            _resident((Q, 1), lambda j: (0, 0)),
            _resident((2 * RADIX, Q, Q), lambda j: (0, 0, 0)),
            cast_spec,
        ],
        out_specs=[pl.BlockSpec((2 * RADIX, Q, tc), lambda j: (0, 0, j)),
                   pl.BlockSpec((RADIX, tc), lambda j: (0, j)), cast_spec],
        out_shape=[jax.ShapeDtypeStruct((2 * RADIX, Q, c2), F32),
                   jax.ShapeDtypeStruct((RADIX, c2), F32),
                   jax.ShapeDtypeStruct(w_cast.shape, BF16)],
        name="filter_spec",
        compiler_params=_params(58, 1),
    )(feat.reshape(RADIX, Q, LANES), w4p, w4p, t_col.reshape(RADIX, Q, 1), absdelta,
      hbias.reshape(1, c2).astype(F32), alt, tabs, w_cast)
    return (sp, kq), w_bf16


def _proj_kernel(x_ref, g_ref, w_ref, o_ref, h_ref, *, n_mix):
    j = pl.program_id(1)
    tn = o_ref.shape[1]
    has_gates = (j + 1) * tn > n_mix

    def with_gates(acc):
        col = j * tn + lax.broadcasted_iota(jnp.int32, acc.shape, 1)
        return jnp.where(col >= n_mix, _sigmoid(acc), acc)

    @pl.when(j == 0)
    def _():
        h = _rms(x_ref[...], g_ref[...]).astype(BF16)
        h_ref[...] = h
        acc = _dot(h, w_ref[...])
        o_ref[...] = (with_gates(acc) if tn > n_mix else acc).astype(BF16)

    @pl.when((j > 0) & jnp.logical_not(has_gates))
    def _():
        o_ref[...] = _dot(h_ref[...], w_ref[...]).astype(BF16)

    @pl.when((j > 0) & has_gates)
    def _():
        o_ref[...] = with_gates(_dot(h_ref[...], w_ref[...])).astype(BF16)


def _proj(x2, g, w, n_mix, tm, tn):
    M, D = x2.shape
    E = w.shape[1]
    return pl.pallas_call(
        functools.partial(_proj_kernel, n_mix=n_mix),
        grid=(M // tm, E // tn),
        in_specs=[
            pl.BlockSpec((tm, D), lambda i, j: (i, 0)),
            _resident((1, D), lambda i, j: (0, 0)),
            pl.BlockSpec((D, tn), lambda i, j: (0, j)),
        ],
        out_specs=pl.BlockSpec((tm, tn), lambda i, j: (i, j)),
        out_shape=jax.ShapeDtypeStruct((M, E), BF16),
        scratch_shapes=[pltpu.VMEM((tm, D), BF16)],
        name="proj",
        compiler_params=_params(56, 2),
    )(x2, g, w)


def _short_conv_rows(p_ref, il_ref, w_ref, b_ref, r0, rc):
    L = p_ref.shape[1]
    t0, t1 = RADIX * r0, RADIX * (r0 + rc)
    parts = [[] for _ in range(RADIX)]
    for c in range(il_ref.shape[0]):
        il_ref[c] = p_ref[0, t0:t1, c * LANES:(c + 1) * LANES].astype(F32)
        for r in range(RADIX):
            parts[r].append(il_ref[c, pl.ds(r, rc, stride=RADIX), :])
    p = [jnp.concatenate(x, axis=1) for x in parts]
    row = lax.broadcasted_iota(jnp.int32, p[0].shape, 0)
    if t0 == 0:
        before = 0.0
    else:
        before = p_ref[0, t0 - BF16_ROWS:t0, :].astype(F32)[BF16_ROWS - 1:BF16_ROWS, :]
    if t1 == L:
        after = 0.0
    else:
        after = p_ref[0, t1:t1 + BF16_ROWS, :].astype(F32)[0:1, :]
    last_prev = jnp.where(row == 0, before, pltpu.roll(p[RADIX - 1], 1, axis=0))
    first_next = jnp.where(row == rc - 1, after, pltpu.roll(p[0], rc - 1, axis=0))
    prev = [last_prev] + p[:-1]
    nxt = p[1:] + [first_next]
    w0, w1, w2, b = w_ref[0:1, :], w_ref[1:2, :], w_ref[2:3, :], b_ref[...]
    return [prev[r] * w0 + p[r] * w1 + nxt[r] * w2 + b for r in range(RADIX)]


def _hyena_kernel(v_ref, x1_ref, x2_ref, wv_ref, w1_ref, w2_ref, bv_ref, b1_ref, b2_ref,
                  sp0_ref, kq0_ref, sp1_ref, kq1_ref, alt_ref, tab_ref, *refs, rc, n_cast):
    o_ref = refs[n_cast]
    zb_ref, p_ref, il_ref = refs[2 * n_cast + 1:]
    for src, dst in zip(refs[:n_cast], refs[n_cast + 1:2 * n_cast + 1]):
        dst[...] = src[...].astype(dst.dtype)
    Q = tab_ref.shape[1]
    R = RADIX
    chunks = [(r0, slice(r0, r0 + rc)) for r0 in range(0, Q, rc)]

    def put_z(zs, rows):
        alt = alt_ref[rows, :]
        for r in range(R):
            zb_ref[r, rows, :] = zs[r].astype(BF16)
        return [_colsum(z * alt) for z in zs]

    def spectrum_product(sp_ref):
        zb = [zb_ref[r] for r in range(R)]
        for _, rows in chunks:
            a = [_dot(tab_ref[r, rows, :], zb[r]) for r in range(R)]
            b = [_dot(tab_ref[R + r, rows, :], zb[r]) for r in range(R)]
            e, f = a[0] + a[2], a[0] - a[2]
            g, h = a[1] + a[3], a[1] - a[3]
            p, q = b[0] + b[2], b[0] - b[2]
            u, v = b[1] + b[3], b[1] - b[3]
            xr = [e + g, e - g, f + v, f - v]
            xi = [p + u, u - p, h - q, q + h]
            pr, pi = zip(*[_cmul(xr[s], xi[s], sp_ref[s, rows, :], sp_ref[R + s, rows, :])
                           for s in range(R)])
            s01p, s01m, s23p, s23m = pr[0] + pr[1], pr[0] - pr[1], pr[2] + pr[3], pr[2] - pr[3]
            t01p, t01m, t23p, t23m = pi[0] + pi[1], pi[0] - pi[1], pi[2] + pi[3], pi[2] - pi[3]
            for r, val in enumerate([s01p + s23p, s01m + t23p, s01p - s23p, s01m - t23p,
                                     t01m - t23m, t01p + s23m, t01m + t23m, t01p - s23m]):
                p_ref[r, rows, :] = val.astype(BF16)

    def conv_rows(rows, t, kq_ref):
        (xrq, xiq), (xr3, xi3) = _special_bins(t)
        prq, piq = _cmul(xrq, xiq, kq_ref[0:1, :], kq_ref[1:2, :])
        pr3, pi3 = _cmul(xr3, xi3, kq_ref[2:3, :], kq_ref[3:4, :])
        sp = [prq + pr3, SQH * (prq + piq + pi3 - pr3), piq - pi3, SQH * (piq - prq + pr3 + pi3)]
        alt = alt_ref[rows, :]
        return [_dot(tab_ref[2 * R + r, rows, :], p_ref[r]) + _dot(tab_ref[3 * R + r, rows, :], p_ref[R + r])
                + alt * sp[r] for r in range(R)]

    def add(acc, new):
        return new if acc is None else [x + y for x, y in zip(acc, new)]

    t = None
    for r0, rows in chunks:
        t = add(t, put_z(_short_conv_rows(v_ref, il_ref, wv_ref, bv_ref, r0, rc), rows))
    spectrum_product(sp0_ref)
    t1 = None
    for r0, rows in chunks:
        y = conv_rows(rows, t, kq0_ref)
        gate = _short_conv_rows(x1_ref, il_ref, w1_ref, b1_ref, r0, rc)
        t1 = add(t1, put_z([gate[r] * y[r] for r in range(R)], rows))
    spectrum_product(sp1_ref)
    for r0, rows in chunks:
        y = conv_rows(rows, t1, kq1_ref)
        gate = _short_conv_rows(x2_ref, il_ref, w2_ref, b2_ref, r0, rc)
        out = [gate[r] * y[r] for r in range(R)]
        for c in range(il_ref.shape[0]):
            lanes = slice(c * LANES, (c + 1) * LANES)
            for r in range(R):
                il_ref[c, pl.ds(r, rc, stride=R), :] = out[r][:, lanes]
            o_ref[0, R * r0:R * (r0 + rc), lanes] = il_ref[c].astype(o_ref.dtype)


def _hyena(p3, conv_w, conv_b, spectra, alt, tabs, n_ch, ct, to_cast):
    B, L, _ = p3.shape
    Q = L // RADIX
    nct = n_ch // ct
    rc = _pick(Q, 512)
    sp, kq = spectra
    cb = conv_b.reshape(1, -1).astype(F32)
    cw = conv_w.astype(F32)

    def pspec(g):
        return pl.BlockSpec((1, L, ct), lambda j, b, g=g: (b, 0, g * nct + j))

    def rowspec(rows, g):
        return pl.BlockSpec((rows, ct), lambda j, b, g=g: (0, g * nct + j))

    def specspec(g):
        return _resident((2 * RADIX, Q, ct), lambda j, b, g=g: (0, 0, g * nct + j))

    steps = nct * B
    cast_specs = []
    for w in to_cast:
        slab = w.shape[0] // steps
        assert slab * steps == w.shape[0] and slab % BF16_ROWS == 0, w.shape
        cast_specs.append(pl.BlockSpec((slab, w.shape[1]), lambda j, b: (j * B + b, 0)))

    return pl.pallas_call(
        functools.partial(_hyena_kernel, rc=rc, n_cast=len(to_cast)),
        grid=(nct, B),
        in_specs=[
            pspec(0), pspec(1), pspec(2),
            rowspec(3, 0), rowspec(3, 1), rowspec(3, 2),
            rowspec(1, 0), rowspec(1, 1), rowspec(1, 2),
            specspec(0), rowspec(RADIX, 0), specspec(1), rowspec(RADIX, 1),
            _resident((Q, 1), lambda j, b: (0, 0)),
            _resident(tabs.shape, lambda j, b: (0, 0, 0)),
        ] + cast_specs,
        out_specs=[pl.BlockSpec((1, L, ct), lambda j, b: (b, 0, j))] + cast_specs,
        out_shape=[jax.ShapeDtypeStruct((B, L, n_ch), BF16)]
                  + [jax.ShapeDtypeStruct(w.shape, BF16) for w in to_cast],
        scratch_shapes=[pltpu.VMEM((RADIX, Q, ct), BF16),
                        pltpu.VMEM((2 * RADIX, Q, ct), BF16),
                        pltpu.VMEM((ct // LANES, RADIX * rc, LANES), F32)],
        name="hyena",
        compiler_params=_params(58, 2),
    )(p3, p3, p3, cw, cw, cw, cb, cb, cb, sp, kq, sp, kq, alt, tabs, *to_cast)


def _gelu(x):
    return 0.5 * x * (1.0 + lax.erf(x * math.sqrt(0.5)))


def _spatial_gating(u_ref, v_ref, g_ref, b_ref, ws_ref, bs_ref, o_ref, heads, chunk):
    u = _gelu(u_ref[...].astype(F32))
    v = _gelu(v_ref[...].astype(F32))
    mu = jnp.mean(v, axis=-1, keepdims=True)
    vc = v - mu
    var = jnp.mean(vc * vc, axis=-1, keepdims=True)
    vn = (vc * lax.rsqrt(var + EPS) * g_ref[...] + b_ref[...]).astype(BF16)
    tm, ds = vn.shape
    hd = ds // heads
    for n in range(tm // chunk):
        rows = slice(n * chunk, (n + 1) * chunk)
        for h in range(heads):
            cols = slice(h * hd, (h + 1) * hd)
            s = _dot(ws_ref[h], vn[rows, cols]) + bs_ref[:, h:h + 1]
            o_ref[rows, cols] = (u[rows, cols] * s).astype(o_ref.dtype)


def _merge_kernel(x_ref, yh_ref, u_ref, v_ref, gh0_ref, gh1_ref, gs0_ref, gs1_ref,
                  lng_ref, lnb_ref, ws_ref, bs_ref, wbh_ref, wbs_ref, wo_ref, o_ref, ys_ref, *, heads, chunk):
    _spatial_gating(u_ref, v_ref, lng_ref, lnb_ref, ws_ref, bs_ref, ys_ref, heads, chunk)
    gh = jnp.concatenate([gh0_ref[...], gh1_ref[...]], axis=1).astype(F32)
    gs = jnp.concatenate([gs0_ref[...], gs1_ref[...]], axis=1).astype(F32)
    a = _dot(yh_ref[...], wbh_ref[...]) * gh + _dot(ys_ref[...], wbs_ref[...]) * gs
    o_ref[...] = x_ref[...] + _dot(a.astype(BF16), wo_ref[...])


def _merge(x2, yh, proj, sgu_col, gate_col, ln_g, ln_b, w_s, b_s, wbh, wbs, wo, tm):
    M, D = x2.shape
    ds = wbs.shape[0]
    heads, chunk, _ = w_s.shape
    gw = D // 2
    gb = gate_col // gw
    sb = sgu_col // ds
    assert gate_col % gw == 0 and sgu_col % ds == 0 and tm % chunk == 0
    full = lambda i: (0, 0)

    def gate(c):
        return pl.BlockSpec((tm, gw), lambda i, c=c: (i, gb + c))

    return pl.pallas_call(
        functools.partial(_merge_kernel, heads=heads, chunk=chunk),
        grid=(M // tm,),
        in_specs=[
            pl.BlockSpec((tm, D), lambda i: (i, 0)),
            pl.BlockSpec((tm, yh.shape[1]), lambda i: (i, 0)),
            pl.BlockSpec((tm, ds), lambda i: (i, sb)),
            pl.BlockSpec((tm, ds), lambda i: (i, sb + 1)),
            gate(0), gate(1), gate(2), gate(3),
            _resident((1, ds), full),
            _resident((1, ds), full),
            _resident((heads, chunk, chunk), lambda i: (0, 0, 0)),
            _resident((chunk, heads), full),
            _resident(wbh.shape, full),
            _resident(wbs.shape, full),
            _resident(wo.shape, full),
        ],
        out_specs=pl.BlockSpec((tm, D), lambda i: (i, 0)),
        out_shape=jax.ShapeDtypeStruct((M, D), F32),
        scratch_shapes=[pltpu.VMEM((tm, ds), BF16)],
        name="merge",
        compiler_params=_params(60, 1),
    )(x2, yh, proj, proj, proj, proj, proj, proj,
      ln_g.reshape(1, ds).astype(F32), ln_b.reshape(1, ds).astype(F32),
      w_s.astype(BF16), b_s.T.astype(F32), wbh, wbs, wo)


def _ffn_kernel(x_ref, g_ref, wg_ref, wu_ref, wo_ref, gf_ref, o_ref, h_ref, *, final_norm):
    f = pl.program_id(1)

    def tile(h):
        gate = _dot(h, wg_ref[...])
        up = _dot(h, wu_ref[...])
        a = (gate * _sigmoid(gate) * up).astype(BF16)
        return _dot(a, wo_ref[...])

    @pl.when(f == 0)
    def _():
        x = x_ref[...]
        h = _rms(x, g_ref[...]).astype(BF16)
        h_ref[...] = h
        o_ref[...] = x + tile(h)

    @pl.when(f > 0)
    def _():
        o_ref[...] += tile(h_ref[...])

    if final_norm:
        @pl.when(f == pl.num_programs(1) - 1)
        def _():
            o_ref[...] = _rms(o_ref[...], gf_ref[...])


def _ffn(x2, g, w_in, w_out, g_final, final_norm, tm, tf):
    M, D = x2.shape
    FF = w_out.shape[0]
    nf = FF // tf
    full = lambda i, f: (0, 0)
    return pl.pallas_call(
        functools.partial(_ffn_kernel, final_norm=final_norm),
        grid=(M // tm, nf),
        in_specs=[
            pl.BlockSpec((tm, D), lambda i, f: (i, 0)),
            _resident((1, D), full),
            pl.BlockSpec((D, tf), lambda i, f: (0, f)),
            pl.BlockSpec((D, tf), lambda i, f: (0, nf + f)),
            pl.BlockSpec((tf, D), lambda i, f: (f, 0)),
            _resident((1, D), full),
        ],
        out_specs=pl.BlockSpec((tm, D), lambda i, f: (i, 0)),
        out_shape=jax.ShapeDtypeStruct((M, D), F32),
        scratch_shapes=[pltpu.VMEM((tm, D), BF16)],
        name="ffn",
        compiler_params=_params(60, 2),
    )(x2, g, w_in, w_in, w_out, g_final)


def _transform_tables(L):
    Q = L // RADIX
    k = np.arange(Q, dtype=np.int64)[:, None]
    m = np.arange(Q, dtype=np.int64)[None, :]

    def tab(fn, r):
        return fn(((k * (RADIX * m + r)) % (2 * L)) * (np.pi / L)).astype(np.float32)

    cs = [tab(np.cos, r) for r in range(RADIX)] + [tab(np.sin, r) for r in range(RADIX)]
    stack = np.stack(cs + [t.T for t in cs], axis=0)
    alt = (1.0 - 2.0 * (np.arange(Q) % 2)).astype(np.float32)[:, None]
    return jnp.asarray(stack).astype(BF16), jnp.asarray(alt)


def _position_features(L, emb):
    Q = L // RADIX
    row = jnp.arange(L, dtype=jnp.int32)
    n = ((row % Q) * RADIX + row // Q).astype(F32)[:, None]
    t = n / (L - 1)
    bands = (emb - 1) // 2
    w = 2.0 * math.pi * n / L
    f = jnp.linspace(1e-4, bands - 1, bands, dtype=F32)[None, :]
    return jnp.concatenate([t, jnp.cos(f * w), -jnp.sin(f * w)], axis=-1), t


def _pick(n, pref):
    t = min(n, pref)
    while n % t:
        t -= 1
    return t


def kernel(x, norm_mix_g, w_in, short_conv_w, short_conv_b, filt_w1, filt_b1, filt_w2, filt_b2, filt_w3, filt_b3, filt_freq, filt_w4, hyena_bias, sgu_ln_g, sgu_ln_b, sgu_w_s, sgu_b_s, w_branch_hyena, w_branch_sgu, w_out, norm_ffn_g, w_ffn_in, w_ffn_out, norm_final_g):
    B, L, D = x.shape
    depth = w_in.shape[0]
    n_ch = w_branch_hyena.shape[1]
    ds = w_branch_sgu.shape[1]
    assert n_ch == ds and 2 * n_ch == D and L % RADIX == 0 and depth >= 1
    n_mix = 3 * n_ch + 2 * ds
    M = B * L

    tabs, alt = _transform_tables(L)
    zfeat, t_col = _position_features(L, filt_w1.shape[1])
    max_decay = math.log(DECAY_TARGET) / FAST_DECAY
    min_decay = math.log(DECAY_TARGET) / SLOW_DECAY
    absdelta = jnp.abs(jnp.linspace(min_decay, max_decay, n_ch, dtype=F32))[None, :]

    ct = _pick(n_ch, 256)
    tm = _pick(M, 1024)
    tn = LANES * _pick(w_in.shape[2] // LANES, 2304 // LANES)

    x2 = x.reshape(M, D)
    for l in range(depth):
        spectra, w_in_bf16 = _hyena_filters(zfeat, t_col, absdelta, alt, tabs,
                                            filt_w1[l], filt_b1[l], filt_w2[l], filt_b2[l],
                                            filt_w3[l], filt_b3[l], filt_freq[l], filt_w4[l],
                                            hyena_bias[l], ct, w_in[l])
        proj = _proj(x2, norm_mix_g[l].reshape(1, D), w_in_bf16, n_mix, tm, tn)
        y_hy, wbh, wbs, wo, wfi, wfo = _hyena(
            proj.reshape(B, L, proj.shape[1]), short_conv_w[l], short_conv_b[l], spectra,
            alt, tabs, n_ch, ct,
            [w_branch_hyena[l], w_branch_sgu[l], w_out[l], w_ffn_in[l], w_ffn_out[l]])
        x2 = _merge(x2, y_hy.reshape(M, n_ch), proj, 3 * n_ch, n_mix,
                    sgu_ln_g[l], sgu_ln_b[l], sgu_w_s[l], sgu_b_s[l], wbh, wbs, wo, _pick(M, 512))
        last = l == depth - 1
        x2 = _ffn(x2, norm_ffn_g[l].reshape(1, D), wfi, wfo, norm_final_g.reshape(1, D), last,
                  _pick(M, 1024), _pick(w_ffn_out.shape[1], 512))
    return x2.reshape(B, L, D)
```

```python
import functools
import math

import numpy as np
import jax
import jax.numpy as jnp
from jax import lax
from jax.experimental import pallas as pl
from jax.experimental.pallas import tpu as pltpu

F32 = jnp.float32
BF16 = jnp.bfloat16

EPS = 1e-6
FAST_DECAY = 0.3
SLOW_DECAY = 1.5
DECAY_TARGET = 1e-2
MOD_SHIFT = 0.0
N_DIR = 2
HYENA_ORDER = 2
RADIX = 4
SQH = math.sqrt(0.5)
LANES = 128
BF16_ROWS = 16
MIB = 1024 * 1024


def _params(vmem_mib, n_grid):
    return pltpu.CompilerParams(
        dimension_semantics=("arbitrary",) * n_grid,
        vmem_limit_bytes=vmem_mib * MIB)


def _resident(block_shape, index_map):
    return pl.BlockSpec(block_shape, index_map, pipeline_mode=pl.Buffered(1))


def _rms(x, g):
    ms = jnp.mean(x * x, axis=-1, keepdims=True)
    return x * lax.rsqrt(ms + EPS) * g


def _dot(a, b):
    return jnp.dot(a, b, preferred_element_type=F32)


def _sigmoid(x):
    return 0.5 * jnp.tanh(0.5 * x) + 0.5


def _colsum(a):
    return jnp.sum(a, axis=0, keepdims=True)


def _cmul(xr, xi, kr, ki):
    return xr * kr - xi * ki, xr * ki + xi * kr


def _special_bins(t):
    d, s = SQH * (t[1] - t[3]), SQH * (t[1] + t[3])
    return (t[0] + d, t[2] + s), (t[0] - d, s - t[2])


def _filter_feat_kernel(z_ref, w1_ref, b1_ref, w2_ref, b2_ref, w3_ref, b3_ref, fr_ref, o_ref):
    hp = lax.Precision.HIGHEST
    fr = fr_ref[...]
    h = jnp.sin(fr * (jnp.dot(z_ref[...], w1_ref[...], precision=hp, preferred_element_type=F32) + b1_ref[...]))
    h = jnp.sin(fr * (jnp.dot(h, w2_ref[...], precision=hp, preferred_element_type=F32) + b2_ref[...]))
    h = jnp.sin(fr * (jnp.dot(h, w3_ref[...], precision=hp, preferred_element_type=F32) + b3_ref[...]))
    o_ref[...] = h


def _filter_spec_kernel(f_ref, w4f_ref, w4b_ref, t_ref, ad_ref, bias_ref, alt_ref, tab_ref, wsrc_ref,
                        sp_ref, kq_ref, wdst_ref):
    wdst_ref[...] = wsrc_ref[...].astype(wdst_ref.dtype)
    Q = f_ref.shape[1]
    inv_n = 1.0 / (2 * RADIX * Q)
    ad = ad_ref[...]
    row = lax.broadcasted_iota(jnp.int32, (Q, ad.shape[1]), 0)

    def split(x):
        hi = x.astype(BF16)
        return hi, (x - hi.astype(F32)).astype(BF16)

    def dot_split(a, b):
        return ((_dot(a[1], b[1]) + _dot(a[1], b[0])) + _dot(a[0], b[1])) + _dot(a[0], b[0])

    w4f, w4b = split(w4f_ref[...]), split(w4b_ref[...])
    hf, hb = [], []
    for r in range(RADIX):
        f = split(f_ref[r])
        decay = jnp.exp(-t_ref[r] * ad) + MOD_SHIFT
        hf.append(dot_split(f, w4f) * decay)
        hb.append(dot_split(f, w4b) * decay)
    hb[0] = jnp.where(row == 0, 0.0, hb[0])
    norm = sum(_colsum(jnp.abs(h)) for h in hf + hb)
    inv = 1.0 / norm
    ev = [(hf[r] + hb[r]) * inv for r in range(RADIX)]
    od = [(hf[r] - hb[r]) * inv for r in range(RADIX)]
    evb = [e.astype(BF16) for e in ev]
    odb = [o.astype(BF16) for o in od]

    def cos_t(r, zb):
        return _dot(tab_ref[r], zb[r])

    def sin_t(r, zb):
        return _dot(tab_ref[RADIX + r], zb[r])

    a0, a2 = cos_t(0, evb), cos_t(2, evb)
    e, f_ = a0 + a2, a0 - a2
    g = cos_t(1, evb) + cos_t(3, evb)
    v = sin_t(1, evb) - sin_t(3, evb)
    b0, b2 = sin_t(0, odb), sin_t(2, odb)
    p, q = b0 + b2, b0 - b2
    u = sin_t(1, odb) + sin_t(3, odb)
    h = cos_t(1, odb) - cos_t(3, odb)
    first = row[:, :1] == 0
    w_end = jnp.where(first, inv_n, 2.0 * inv_n)
    w_dup = jnp.where(first, 0.0, 2.0 * inv_n)
    w_mid = 2.0 * inv_n
    bias = bias_ref[...]
    for s, (kr, ki, w) in enumerate([(e + g, p + u, w_end), (e - g, u - p, w_end),
                                     (f_ + v, h - q, w_mid), (f_ - v, q + h, w_dup)]):
        sp_ref[s] = (kr + bias) * w
        sp_ref[RADIX + s] = ki * w
    alt = alt_ref[...]
    (krq, _), (kr3, _) = _special_bins([_colsum(x * alt) for x in ev])
    (_, kiq), (_, ki3) = _special_bins([_colsum(x * alt) for x in od])
    for i, kval in enumerate((krq + bias, kiq, kr3 + bias, ki3)):
        kq_ref[i:i + 1, :] = kval * w_mid


def _hyena_filters(zfeat, t_col, absdelta, alt, tabs, w1, b1, w2, b2, w3, b3, freq, w4, hbias, tc, w_cast):
    L = zfeat.shape[0]
    Q = L // RADIX
    c2 = w4.shape[1] // N_DIR
    n_ch = c2 // HYENA_ORDER

    def pad2(a, r, c):
        return jnp.pad(a.astype(F32), ((0, r - a.shape[0]), (0, c - a.shape[1])))

    feat = pl.pallas_call(
        _filter_feat_kernel,
        out_shape=jax.ShapeDtypeStruct((L, LANES), F32),
        name="filter_feat",
        compiler_params=_params(32, 0),
    )(pad2(zfeat, L, LANES),
      pad2(w1, LANES, LANES), pad2(b1[None], 1, LANES),
      pad2(w2, LANES, LANES), pad2(b2[None], 1, LANES),
      pad2(w3, LANES, LANES), pad2(b3[None], 1, LANES), pad2(freq[None], 1, LANES))

    w4p = pad2(w4, LANES, w4.shape[1])
    nt = c2 // tc
    nct = n_ch // tc
    slab = w_cast.shape[0] // nt
    assert slab * nt == w_cast.shape[0] and slab % BF16_ROWS == 0
    cast_spec = pl.BlockSpec((slab, w_cast.shape[1]), lambda j: (j, 0))
    sp, kq, w_bf16 = pl.pallas_call(
        _filter_spec_kernel,
        grid=(nt,),
        in_specs=[
            _resident((RADIX, Q, LANES), lambda j: (0, 0, 0)),
            pl.BlockSpec((LANES, tc), lambda j: (0, j)),
            pl.BlockSpec((LANES, tc), lambda j: (0, nt + j)),
            _resident((RADIX, Q, 1), lambda j: (0, 0, 0)),
            pl.BlockSpec((1, tc), lambda j: (0, j % nct)),
            pl.BlockSpec((1, tc), lambda j: (0, j)),
            _resident((Q, 1), lambda j: (0, 0)),
            _resident((2 * RADIX, Q, Q), lambda j: (0, 0, 0)),
            cast_spec,
        ],
        out_specs=[pl.BlockSpec((2 * RADIX, Q, tc), lambda j: (0, 0, j)),
                   pl.BlockSpec((RADIX, tc), lambda j: (0, j)), cast_spec],
        out_shape=[jax.ShapeDtypeStruct((2 * RADIX, Q, c2), F32),
                   jax.ShapeDtypeStruct((RADIX, c2), F32),
                   jax.ShapeDtypeStruct(w_cast.shape, BF16)],
        name="filter_spec",
        compiler_params=_params(58, 1),
    )(feat.reshape(RADIX, Q, LANES), w4p, w4p, t_col.reshape(RADIX, Q, 1), absdelta,
      hbias.reshape(1, c2).astype(F32), alt, tabs, w_cast)
    return (sp, kq), w_bf16


def _proj_kernel(x_ref, g_ref, w_ref, o_ref, h_ref, *, n_mix):
    j = pl.program_id(1)
    tn = o_ref.shape[1]
    has_gates = (j + 1) * tn > n_mix

    def with_gates(acc):
        col = j * tn + lax.broadcasted_iota(jnp.int32, acc.shape, 1)
        return jnp.where(col >= n_mix, _sigmoid(acc), acc)

    @pl.when(j == 0)
    def _():
        h = _rms(x_ref[...], g_ref[...]).astype(BF16)
        h_ref[...] = h
        acc = _dot(h, w_ref[...])
        o_ref[...] = (with_gates(acc) if tn > n_mix else acc).astype(BF16)

    @pl.when((j > 0) & jnp.logical_not(has_gates))
    def _():
        o_ref[...] = _dot(h_ref[...], w_ref[...]).astype(BF16)

    @pl.when((j > 0) & has_gates)
    def _():
        o_ref[...] = with_gates(_dot(h_ref[...], w_ref[...])).astype(BF16)


def _proj(x2, g, w, n_mix, tm, tn):
    M, D = x2.shape
    E = w.shape[1]
    return pl.pallas_call(
        functools.partial(_proj_kernel, n_mix=n_mix),
        grid=(M // tm, E // tn),
        in_specs=[
            pl.BlockSpec((tm, D), lambda i, j: (i, 0)),
            _resident((1, D), lambda i, j: (0, 0)),
            pl.BlockSpec((D, tn), lambda i, j: (0, j)),
        ],
        out_specs=pl.BlockSpec((tm, tn), lambda i, j: (i, j)),
        out_shape=jax.ShapeDtypeStruct((M, E), BF16),
        scratch_shapes=[pltpu.VMEM((tm, D), BF16)],
        name="proj",
        compiler_params=_params(56, 2),
    )(x2, g, w)


def _short_conv_rows(p_ref, il_ref, w_ref, b_ref, r0, rc):
    L = p_ref.shape[1]
    t0, t1 = RADIX * r0, RADIX * (r0 + rc)
    parts = [[] for _ in range(RADIX)]
    for c in range(il_ref.shape[0]):
        il_ref[c] = p_ref[0, t0:t1, c * LANES:(c + 1) * LANES].astype(F32)
        for r in range(RADIX):
            parts[r].append(il_ref[c, pl.ds(r, rc, stride=RADIX), :])
    p = [jnp.concatenate(x, axis=1) for x in parts]
    row = lax.broadcasted_iota(jnp.int32, p[0].shape, 0)
    if t0 == 0:
        before = 0.0
    else:
        before = p_ref[0, t0 - BF16_ROWS:t0, :].astype(F32)[BF16_ROWS - 1:BF16_ROWS, :]
    if t1 == L:
        after = 0.0
    else:
        after = p_ref[0, t1:t1 + BF16_ROWS, :].astype(F32)[0:1, :]
    last_prev = jnp.where(row == 0, before, pltpu.roll(p[RADIX - 1], 1, axis=0))
    first_next = jnp.where(row == rc - 1, after, pltpu.roll(p[0], rc - 1, axis=0))
    prev = [last_prev] + p[:-1]
    nxt = p[1:] + [first_next]
    w0, w1, w2, b = w_ref[0:1, :], w_ref[1:2, :], w_ref[2:3, :], b_ref[...]
    return [prev[r] * w0 + p[r] * w1 + nxt[r] * w2 + b for r in range(RADIX)]


def _hyena_kernel(v_ref, x1_ref, x2_ref, wv_ref, w1_ref, w2_ref, bv_ref, b1_ref, b2_ref,
                  sp0_ref, kq0_ref, sp1_ref, kq1_ref, alt_ref, tab_ref, *refs, rc, n_cast):
    o_ref = refs[n_cast]
    zb_ref, p_ref, il_ref = refs[2 * n_cast + 1:]
    for src, dst in zip(refs[:n_cast], refs[n_cast + 1:2 * n_cast + 1]):
        dst[...] = src[...].astype(dst.dtype)
    Q = tab_ref.shape[1]
    R = RADIX
    chunks = [(r0, slice(r0, r0 + rc)) for r0 in range(0, Q, rc)]

    def put_z(zs, rows):
        alt = alt_ref[rows, :]
        for r in range(R):
            zb_ref[r, rows, :] = zs[r].astype(BF16)
        return [_colsum(z * alt) for z in zs]

    def spectrum_product(sp_ref):
        zb = [zb_ref[r] for r in range(R)]
        for _, rows in chunks:
            a = [_dot(tab_ref[r, rows, :], zb[r]) for r in range(R)]
            b = [_dot(tab_ref[R + r, rows, :], zb[r]) for r in range(R)]
            e, f = a[0] + a[2], a[0] - a[2]
            g, h = a[1] + a[3], a[1] - a[3]
            p, q = b[0] + b[2], b[0] - b[2]
            u, v = b[1] + b[3], b[1] - b[3]
            xr = [e + g, e - g, f + v, f - v]
            xi = [p + u, u - p, h - q, q + h]
            pr, pi = zip(*[_cmul(xr[s], xi[s], sp_ref[s, rows, :], sp_ref[R + s, rows, :])
                           for s in range(R)])
            s01p, s01m, s23p, s23m = pr[0] + pr[1], pr[0] - pr[1], pr[2] + pr[3], pr[2] - pr[3]
            t01p, t01m, t23p, t23m = pi[0] + pi[1], pi[0] - pi[1], pi[2] + pi[3], pi[2] - pi[3]
            for r, val in enumerate([s01p + s23p, s01m + t23p, s01p - s23p, s01m - t23p,
                                     t01m - t23m, t01p + s23m, t01m + t23m, t01p - s23m]):
                p_ref[r, rows, :] = val.astype(BF16)

    def conv_rows(rows, t, kq_ref):
        (xrq, xiq), (xr3, xi3) = _special_bins(t)
        prq, piq = _cmul(xrq, xiq, kq_ref[0:1, :], kq_ref[1:2, :])
        pr3, pi3 = _cmul(xr3, xi3, kq_ref[2:3, :], kq_ref[3:4, :])
        sp = [prq + pr3, SQH * (prq + piq + pi3 - pr3), piq - pi3, SQH * (piq - prq + pr3 + pi3)]
        alt = alt_ref[rows, :]
        return [_dot(tab_ref[2 * R + r, rows, :], p_ref[r]) + _dot(tab_ref[3 * R + r, rows, :], p_ref[R + r])
                + alt * sp[r] for r in range(R)]

    def add(acc, new):
        return new if acc is None else [x + y for x, y in zip(acc, new)]

    t = None
    for r0, rows in chunks:
        t = add(t, put_z(_short_conv_rows(v_ref, il_ref, wv_ref, bv_ref, r0, rc), rows))
    spectrum_product(sp0_ref)
    t1 = None
    for r0, rows in chunks:
        y = conv_rows(rows, t, kq0_ref)
        gate = _short_conv_rows(x1_ref, il_ref, w1_ref, b1_ref, r0, rc)
        t1 = add(t1, put_z([gate[r] * y[r] for r in range(R)], rows))
    spectrum_product(sp1_ref)
    for r0, rows in chunks:
        y = conv_rows(rows, t1, kq1_ref)
        gate = _short_conv_rows(x2_ref, il_ref, w2_ref, b2_ref, r0, rc)
        out = [gate[r] * y[r] for r in range(R)]
        for c in range(il_ref.shape[0]):
            lanes = slice(c * LANES, (c + 1) * LANES)
            for r in range(R):
                il_ref[c, pl.ds(r, rc, stride=R), :] = out[r][:, lanes]
            o_ref[0, R * r0:R * (r0 + rc), lanes] = il_ref[c].astype(o_ref.dtype)


def _hyena(p3, conv_w, conv_b, spectra, alt, tabs, n_ch, ct, to_cast):
    B, L, _ = p3.shape
    Q = L // RADIX
    nct = n_ch // ct
    rc = _pick(Q, 512)
    sp, kq = spectra
    cb = conv_b.reshape(1, -1).astype(F32)
    cw = conv_w.astype(F32)

    def pspec(g):
        return pl.BlockSpec((1, L, ct), lambda j, b, g=g: (b, 0, g * nct + j))

    def rowspec(rows, g):
        return pl.BlockSpec((rows, ct), lambda j, b, g=g: (0, g * nct + j))

    def specspec(g):
        return _resident((2 * RADIX, Q, ct), lambda j, b, g=g: (0, 0, g * nct + j))

    steps = nct * B
    cast_specs = []
    for w in to_cast:
        slab = w.shape[0] // steps
        assert slab * steps == w.shape[0] and slab % BF16_ROWS == 0, w.shape
        cast_specs.append(pl.BlockSpec((slab, w.shape[1]), lambda j, b: (j * B + b, 0)))

    return pl.pallas_call(
        functools.partial(_hyena_kernel, rc=rc, n_cast=len(to_cast)),
        grid=(nct, B),
        in_specs=[
            pspec(0), pspec(1), pspec(2),
            rowspec(3, 0), rowspec(3, 1), rowspec(3, 2),
            rowspec(1, 0), rowspec(1, 1), rowspec(1, 2),
            specspec(0), rowspec(RADIX, 0), specspec(1), rowspec(RADIX, 1),
            _resident((Q, 1), lambda j, b: (0, 0)),
            _resident(tabs.shape, lambda j, b: (0, 0, 0)),
        ] + cast_specs,
        out_specs=[pl.BlockSpec((1, L, ct), lambda j, b: (b, 0, j))] + cast_specs,
        out_shape=[jax.ShapeDtypeStruct((B, L, n_ch), BF16)]
                  + [jax.ShapeDtypeStruct(w.shape, BF16) for w in to_cast],
        scratch_shapes=[pltpu.VMEM((RADIX, Q, ct), BF16),
                        pltpu.VMEM((2 * RADIX, Q, ct), BF16),
                        pltpu.VMEM((ct // LANES, RADIX * rc, LANES), F32)],
        name="hyena",
        compiler_params=_params(58, 2),
    )(p3, p3, p3, cw, cw, cw, cb, cb, cb, sp, kq, sp, kq, alt, tabs, *to_cast)


def _gelu(x):
    return 0.5 * x * (1.0 + lax.erf(x * math.sqrt(0.5)))


def _spatial_gating(u_ref, v_ref, g_ref, b_ref, ws_ref, bs_ref, o_ref, heads, chunk):
    u = _gelu(u_ref[...].astype(F32))
    v = _gelu(v_ref[...].astype(F32))
    mu = jnp.mean(v, axis=-1, keepdims=True)
    vc = v - mu
    var = jnp.mean(vc * vc, axis=-1, keepdims=True)
    vn = (vc * lax.rsqrt(var + EPS) * g_ref[...] + b_ref[...]).astype(BF16)
    tm, ds = vn.shape
    hd = ds // heads
    for n in range(tm // chunk):
        rows = slice(n * chunk, (n + 1) * chunk)
        for h in range(heads):
            cols = slice(h * hd, (h + 1) * hd)
            s = _dot(ws_ref[h], vn[rows, cols]) + bs_ref[:, h:h + 1]
            o_ref[rows, cols] = (u[rows, cols] * s).astype(o_ref.dtype)


def _merge_kernel(x_ref, yh_ref, u_ref, v_ref, gh0_ref, gh1_ref, gs0_ref, gs1_ref,
                  lng_ref, lnb_ref, ws_ref, bs_ref, wbh_ref, wbs_ref, wo_ref, o_ref, ys_ref, *, heads, chunk):
    gh = jnp.concatenate([gh0_ref[...], gh1_ref[...]], axis=1).astype(F32)
    ah = _dot(yh_ref[...], wbh_ref[...]) * gh
    _spatial_gating(u_ref, v_ref, lng_ref, lnb_ref, ws_ref, bs_ref, ys_ref, heads, chunk)
    gs = jnp.concatenate([gs0_ref[...], gs1_ref[...]], axis=1).astype(F32)
    a = ah + _dot(ys_ref[...], wbs_ref[...]) * gs
    o_ref[...] = x_ref[...] + _dot(a.astype(BF16), wo_ref[...])


def _merge(x2, yh, proj, sgu_col, gate_col, ln_g, ln_b, w_s, b_s, wbh, wbs, wo, tm):
    M, D = x2.shape
    ds = wbs.shape[0]
    heads, chunk, _ = w_s.shape
    gw = D // 2
    gb = gate_col // gw
    sb = sgu_col // ds
    assert gate_col % gw == 0 and sgu_col % ds == 0 and tm % chunk == 0
    full = lambda i: (0, 0)

    def gate(c):
        return pl.BlockSpec((tm, gw), lambda i, c=c: (i, gb + c))

    return pl.pallas_call(
        functools.partial(_merge_kernel, heads=heads, chunk=chunk),
        grid=(M // tm,),
        in_specs=[
            pl.BlockSpec((tm, D), lambda i: (i, 0)),
            pl.BlockSpec((tm, yh.shape[1]), lambda i: (i, 0)),
            pl.BlockSpec((tm, ds), lambda i: (i, sb)),
            pl.BlockSpec((tm, ds), lambda i: (i, sb + 1)),
            gate(0), gate(1), gate(2), gate(3),
            _resident((1, ds), full),
            _resident((1, ds), full),
            _resident((heads, chunk, chunk), lambda i: (0, 0, 0)),
            _resident((chunk, heads), full),
            _resident(wbh.shape, full),
            _resident(wbs.shape, full),
            _resident(wo.shape, full),
        ],
        out_specs=pl.BlockSpec((tm, D), lambda i: (i, 0)),
        out_shape=jax.ShapeDtypeStruct((M, D), F32),
        scratch_shapes=[pltpu.VMEM((tm, ds), BF16)],
        name="merge",
        compiler_params=_params(60, 1),
    )(x2, yh, proj, proj, proj, proj, proj, proj,
      ln_g.reshape(1, ds).astype(F32), ln_b.reshape(1, ds).astype(F32),
      w_s.astype(BF16), b_s.T.astype(F32), wbh, wbs, wo)


def _ffn_kernel(x_ref, g_ref, wg_ref, wu_ref, wo_ref, gf_ref, o_ref, h_ref, *, final_norm):
    f = pl.program_id(1)

    def tile(h):
        gate = _dot(h, wg_ref[...])
        up = _dot(h, wu_ref[...])
        a = (gate * _sigmoid(gate) * up).astype(BF16)
        return _dot(a, wo_ref[...])

    @pl.when(f == 0)
    def _():
        x = x_ref[...]
        h = _rms(x, g_ref[...]).astype(BF16)
        h_ref[...] = h
        o_ref[...] = x + tile(h)

    @pl.when(f > 0)
    def _():
        o_ref[...] += tile(h_ref[...])

    if final_norm:
        @pl.when(f == pl.num_programs(1) - 1)
        def _():
            o_ref[...] = _rms(o_ref[...], gf_ref[...])


def _ffn(x2, g, w_in, w_out, g_final, final_norm, tm, tf):
    M, D = x2.shape
    FF = w_out.shape[0]
    nf = FF // tf
    full = lambda i, f: (0, 0)
    return pl.pallas_call(
        functools.partial(_ffn_kernel, final_norm=final_norm),
        grid=(M // tm, nf),
        in_specs=[
            pl.BlockSpec((tm, D), lambda i, f: (i, 0)),
            _resident((1, D), full),
            pl.BlockSpec((D, tf), lambda i, f: (0, f)),
            pl.BlockSpec((D, tf), lambda i, f: (0, nf + f)),
            pl.BlockSpec((tf, D), lambda i, f: (f, 0)),
            _resident((1, D), full),
        ],
        out_specs=pl.BlockSpec((tm, D), lambda i, f: (i, 0)),
        out_shape=jax.ShapeDtypeStruct((M, D), F32),
        scratch_shapes=[pltpu.VMEM((tm, D), BF16)],
        name="ffn",
        compiler_params=_params(60, 2),
    )(x2, g, w_in, w_in, w_out, g_final)


def _transform_tables(L):
    Q = L // RADIX
    k = np.arange(Q, dtype=np.int64)[:, None]
    m = np.arange(Q, dtype=np.int64)[None, :]

    def tab(fn, r):
        return fn(((k * (RADIX * m + r)) % (2 * L)) * (np.pi / L)).astype(np.float32)

    cs = [tab(np.cos, r) for r in range(RADIX)] + [tab(np.sin, r) for r in range(RADIX)]
    stack = np.stack(cs + [t.T for t in cs], axis=0)
    alt = (1.0 - 2.0 * (np.arange(Q) % 2)).astype(np.float32)[:, None]
    return jnp.asarray(stack).astype(BF16), jnp.asarray(alt)


def _position_features(L, emb):
    Q = L // RADIX
    row = jnp.arange(L, dtype=jnp.int32)
    n = ((row % Q) * RADIX + row // Q).astype(F32)[:, None]
    t = n / (L - 1)
    bands = (emb - 1) // 2
    w = 2.0 * math.pi * n / L
    f = jnp.linspace(1e-4, bands - 1, bands, dtype=F32)[None, :]
    return jnp.concatenate([t, jnp.cos(f * w), -jnp.sin(f * w)], axis=-1), t


def _pick(n, pref):
    t = min(n, pref)
    while n % t:
        t -= 1
    return t


def kernel(x, norm_mix_g, w_in, short_conv_w, short_conv_b, filt_w1, filt_b1, filt_w2, filt_b2, filt_w3, filt_b3, filt_freq, filt_w4, hyena_bias, sgu_ln_g, sgu_ln_b, sgu_w_s, sgu_b_s, w_branch_hyena, w_branch_sgu, w_out, norm_ffn_g, w_ffn_in, w_ffn_out, norm_final_g):
    B, L, D = x.shape
    depth = w_in.shape[0]
    n_ch = w_branch_hyena.shape[1]
    ds = w_branch_sgu.shape[1]
    assert n_ch == ds and 2 * n_ch == D and L % RADIX == 0 and depth >= 1
    n_mix = 3 * n_ch + 2 * ds
    M = B * L

    tabs, alt = _transform_tables(L)
    zfeat, t_col = _position_features(L, filt_w1.shape[1])
    max_decay = math.log(DECAY_TARGET) / FAST_DECAY
    min_decay = math.log(DECAY_TARGET) / SLOW_DECAY
    absdelta = jnp.abs(jnp.linspace(min_decay, max_decay, n_ch, dtype=F32))[None, :]

    ct = _pick(n_ch, 256)
    tm = _pick(M, 1024)
    tn = LANES * _pick(w_in.shape[2] // LANES, 2304 // LANES)

    x2 = x.reshape(M, D)
    for l in range(depth):
        spectra, w_in_bf16 = _hyena_filters(zfeat, t_col, absdelta, alt, tabs,
                                            filt_w1[l], filt_b1[l], filt_w2[l], filt_b2[l],
                                            filt_w3[l], filt_b3[l], filt_freq[l], filt_w4[l],
                                            hyena_bias[l], ct, w_in[l])
        proj = _proj(x2, norm_mix_g[l].reshape(1, D), w_in_bf16, n_mix, tm, tn)
        y_hy, wbh, wbs, wo, wfi, wfo = _hyena(
            proj.reshape(B, L, proj.shape[1]), short_conv_w[l], short_conv_b[l], spectra,
            alt, tabs, n_ch, ct,
            [w_branch_hyena[l], w_branch_sgu[l], w_out[l], w_ffn_in[l], w_ffn_out[l]])
        x2 = _merge(x2, y_hy.reshape(M, n_ch), proj, 3 * n_ch, n_mix,
                    sgu_ln_g[l], sgu_ln_b[l], sgu_w_s[l], sgu_b_s[l], wbh, wbs, wo, _pick(M, 512))
        last = l == depth - 1
        x2 = _ffn(x2, norm_ffn_g[l].reshape(1, D), wfi, wfo, norm_final_g.reshape(1, D), last,
                  _pick(M, 1024), _pick(w_ffn_out.shape[1], 512))
    return x2.reshape(B, L, D)
```

```python
import functools
import math

import numpy as np
import jax
import jax.numpy as jnp
from jax import lax
from jax.experimental import pallas as pl
from jax.experimental.pallas import tpu as pltpu

F32 = jnp.float32
BF16 = jnp.bfloat16

EPS = 1e-6
FAST_DECAY = 0.3
SLOW_DECAY = 1.5
DECAY_TARGET = 1e-2
MOD_SHIFT = 0.0
N_DIR = 2
HYENA_ORDER = 2
RADIX = 4
SQH = math.sqrt(0.5)
LANES = 128
BF16_ROWS = 16
MIB = 1024 * 1024


def _params(vmem_mib, n_grid):
    return pltpu.CompilerParams(
        dimension_semantics=("arbitrary",) * n_grid,
        vmem_limit_bytes=vmem_mib * MIB)


def _resident(block_shape, index_map):
    return pl.BlockSpec(block_shape, index_map, pipeline_mode=pl.Buffered(1))


def _rms(x, g):
    ms = jnp.mean(x * x, axis=-1, keepdims=True)
    return x * lax.rsqrt(ms + EPS) * g


def _dot(a, b):
    return jnp.dot(a, b, preferred_element_type=F32)


def _sigmoid(x):
    return 0.5 * jnp.tanh(0.5 * x) + 0.5


def _colsum(a):
    return jnp.sum(a, axis=0, keepdims=True)


def _cmul(xr, xi, kr, ki):
    return xr * kr - xi * ki, xr * ki + xi * kr


def _special_bins(t):
    d, s = SQH * (t[1] - t[3]), SQH * (t[1] + t[3])
    return (t[0] + d, t[2] + s), (t[0] - d, s - t[2])


def _filter_feat_kernel(z_ref, w1_ref, b1_ref, w2_ref, b2_ref, w3_ref, b3_ref, fr_ref, o_ref):
    hp = lax.Precision.HIGHEST
    fr = fr_ref[...]
    h = jnp.sin(fr * (jnp.dot(z_ref[...], w1_ref[...], precision=hp, preferred_element_type=F32) + b1_ref[...]))
    h = jnp.sin(fr * (jnp.dot(h, w2_ref[...], precision=hp, preferred_element_type=F32) + b2_ref[...]))
    h = jnp.sin(fr * (jnp.dot(h, w3_ref[...], precision=hp, preferred_element_type=F32) + b3_ref[...]))
    o_ref[...] = h


def _filter_spec_kernel(f_ref, w4f_ref, w4b_ref, t_ref, ad_ref, bias_ref, alt_ref, tab_ref, wsrc_ref,
                        sp_ref, kq_ref, wdst_ref):
    wdst_ref[...] = wsrc_ref[...].astype(wdst_ref.dtype)
    Q = f_ref.shape[1]
    inv_n = 1.0 / (2 * RADIX * Q)
    ad = ad_ref[...]
    row = lax.broadcasted_iota(jnp.int32, (Q, ad.shape[1]), 0)

    def split(x):
        hi = x.astype(BF16)
        return hi, (x - hi.astype(F32)).astype(BF16)

    def dot_split(a, b):
        return ((_dot(a[1], b[1]) + _dot(a[1], b[0])) + _dot(a[0], b[1])) + _dot(a[0], b[0])

    w4f, w4b = split(w4f_ref[...]), split(w4b_ref[...])
    hf, hb = [], []
    for r in range(RADIX):
        f = split(f_ref[r][:, 0:w4f_ref.shape[0]])
        decay = jnp.exp(-t_ref[r] * ad) + MOD_SHIFT
        hf.append(dot_split(f, w4f) * decay)
        hb.append(dot_split(f, w4b) * decay)
    hb[0] = jnp.where(row == 0, 0.0, hb[0])
    norm = sum(_colsum(jnp.abs(h)) for h in hf + hb)
    inv = 1.0 / norm
    ev = [(hf[r] + hb[r]) * inv for r in range(RADIX)]
    od = [(hf[r] - hb[r]) * inv for r in range(RADIX)]
    evb = [e.astype(BF16) for e in ev]
    odb = [o.astype(BF16) for o in od]

    def cos_t(r, zb):
        return _dot(tab_ref[r], zb[r])

    def sin_t(r, zb):
        return _dot(tab_ref[RADIX + r], zb[r])

    a0, a2 = cos_t(0, evb), cos_t(2, evb)
    e, f_ = a0 + a2, a0 - a2
    g = cos_t(1, evb) + cos_t(3, evb)
    v = sin_t(1, evb) - sin_t(3, evb)
    b0, b2 = sin_t(0, odb), sin_t(2, odb)
    p, q = b0 + b2, b0 - b2
    u = sin_t(1, odb) + sin_t(3, odb)
    h = cos_t(1, odb) - cos_t(3, odb)
    first = row[:, :1] == 0
    w_end = jnp.where(first, inv_n, 2.0 * inv_n)
    w_dup = jnp.where(first, 0.0, 2.0 * inv_n)
    w_mid = 2.0 * inv_n
    bias = bias_ref[...]
    for s, (kr, ki, w) in enumerate([(e + g, p + u, w_end), (e - g, u - p, w_end),
                                     (f_ + v, h - q, w_mid), (f_ - v, q + h, w_dup)]):
        sp_ref[s] = (kr + bias) * w
        sp_ref[RADIX + s] = ki * w
    alt = alt_ref[...]
    (krq, _), (kr3, _) = _special_bins([_colsum(x * alt) for x in ev])
    (_, kiq), (_, ki3) = _special_bins([_colsum(x * alt) for x in od])
    for i, kval in enumerate((krq + bias, kiq, kr3 + bias, ki3)):
        kq_ref[i:i + 1, :] = kval * w_mid


def _hyena_filters(zfeat, t_col, absdelta, alt, tabs, w1, b1, w2, b2, w3, b3, freq, w4, hbias, tc, w_cast):
    L = zfeat.shape[0]
    Q = L // RADIX
    c2 = w4.shape[1] // N_DIR
    n_ch = c2 // HYENA_ORDER

    def pad2(a, r, c):
        return jnp.pad(a.astype(F32), ((0, r - a.shape[0]), (0, c - a.shape[1])))

    feat = pl.pallas_call(
        _filter_feat_kernel,
        out_shape=jax.ShapeDtypeStruct((L, LANES), F32),
        name="filter_feat",
        compiler_params=_params(32, 0),
    )(pad2(zfeat, L, LANES),
      pad2(w1, LANES, LANES), pad2(b1[None], 1, LANES),
      pad2(w2, LANES, LANES), pad2(b2[None], 1, LANES),
      pad2(w3, LANES, LANES), pad2(b3[None], 1, LANES), pad2(freq[None], 1, LANES))

    w4p = w4.astype(F32)
    fo = w4.shape[0]
    nt = c2 // tc
    nct = n_ch // tc
    slab = w_cast.shape[0] // nt
    assert slab * nt == w_cast.shape[0] and slab % BF16_ROWS == 0
    cast_spec = pl.BlockSpec((slab, w_cast.shape[1]), lambda j: (j, 0))
    sp, kq, w_bf16 = pl.pallas_call(
        _filter_spec_kernel,
        grid=(nt,),
        in_specs=[
            _resident((RADIX, Q, LANES), lambda j: (0, 0, 0)),
            pl.BlockSpec((fo, tc), lambda j: (0, j)),
            pl.BlockSpec((fo, tc), lambda j: (0, nt + j)),
            _resident((RADIX, Q, 1), lambda j: (0, 0, 0)),
            pl.BlockSpec((1, tc), lambda j: (0, j % nct)),
            pl.BlockSpec((1, tc), lambda j: (0, j)),
            _resident((Q, 1), lambda j: (0, 0)),
            _resident((2 * RADIX, Q, Q), lambda j: (0, 0, 0)),
            cast_spec,
        ],
        out_specs=[pl.BlockSpec((2 * RADIX, Q, tc), lambda j: (0, 0, j)),
                   pl.BlockSpec((RADIX, tc), lambda j: (0, j)), cast_spec],
        out_shape=[jax.ShapeDtypeStruct((2 * RADIX, Q, c2), F32),
                   jax.ShapeDtypeStruct((RADIX, c2), F32),
                   jax.ShapeDtypeStruct(w_cast.shape, BF16)],
        name="filter_spec",
        compiler_params=_params(58, 1),
    )(feat.reshape(RADIX, Q, LANES), w4p, w4p, t_col.reshape(RADIX, Q, 1), absdelta,
      hbias.reshape(1, c2).astype(F32), alt, tabs, w_cast)
    return (sp, kq), w_bf16


def _proj_kernel(x_ref, g_ref, w_ref, o_ref, h_ref, *, n_mix):
    j = pl.program_id(1)
    tn = o_ref.shape[1]
    has_gates = (j + 1) * tn > n_mix

    def with_gates(acc):
        col = j * tn + lax.broadcasted_iota(jnp.int32, acc.shape, 1)
        return jnp.where(col >= n_mix, _sigmoid(acc), acc)

    @pl.when(j == 0)
    def _():
        h = _rms(x_ref[...], g_ref[...]).astype(BF16)
        h_ref[...] = h
        acc = _dot(h, w_ref[...])
        o_ref[...] = (with_gates(acc) if tn > n_mix else acc).astype(BF16)

    @pl.when((j > 0) & jnp.logical_not(has_gates))
    def _():
        o_ref[...] = _dot(h_ref[...], w_ref[...]).astype(BF16)

    @pl.when((j > 0) & has_gates)
    def _():
        o_ref[...] = with_gates(_dot(h_ref[...], w_ref[...])).astype(BF16)


def _proj(x2, g, w, n_mix, tm, tn):
    M, D = x2.shape
    E = w.shape[1]
    return pl.pallas_call(
        functools.partial(_proj_kernel, n_mix=n_mix),
        grid=(M // tm, E // tn),
        in_specs=[
            pl.BlockSpec((tm, D), lambda i, j: (i, 0)),
            _resident((1, D), lambda i, j: (0, 0)),
            pl.BlockSpec((D, tn), lambda i, j: (0, j)),
        ],
        out_specs=pl.BlockSpec((tm, tn), lambda i, j: (i, j)),
        out_shape=jax.ShapeDtypeStruct((M, E), BF16),
        scratch_shapes=[pltpu.VMEM((tm, D), BF16)],
        name="proj",
        compiler_params=_params(56, 2),
    )(x2, g, w)


def _short_conv_rows(p_ref, il_ref, w_ref, b_ref, r0, rc):
    L = p_ref.shape[1]
    t0, t1 = RADIX * r0, RADIX * (r0 + rc)
    parts = [[] for _ in range(RADIX)]
    for c in range(il_ref.shape[0]):
        il_ref[c] = p_ref[0, t0:t1, c * LANES:(c + 1) * LANES].astype(F32)
        for r in range(RADIX):
            parts[r].append(il_ref[c, pl.ds(r, rc, stride=RADIX), :])
    p = [jnp.concatenate(x, axis=1) for x in parts]
    row = lax.broadcasted_iota(jnp.int32, p[0].shape, 0)
    if t0 == 0:
        before = 0.0
    else:
        before = p_ref[0, t0 - BF16_ROWS:t0, :].astype(F32)[BF16_ROWS - 1:BF16_ROWS, :]
    if t1 == L:
        after = 0.0
    else:
        after = p_ref[0, t1:t1 + BF16_ROWS, :].astype(F32)[0:1, :]
    last_prev = jnp.where(row == 0, before, pltpu.roll(p[RADIX - 1], 1, axis=0))
    first_next = jnp.where(row == rc - 1, after, pltpu.roll(p[0], rc - 1, axis=0))
    prev = [last_prev] + p[:-1]
    nxt = p[1:] + [first_next]
    w0, w1, w2, b = w_ref[0:1, :], w_ref[1:2, :], w_ref[2:3, :], b_ref[...]
    return [prev[r] * w0 + p[r] * w1 + nxt[r] * w2 + b for r in range(RADIX)]


def _hyena_kernel(v_ref, x1_ref, x2_ref, wv_ref, w1_ref, w2_ref, bv_ref, b1_ref, b2_ref,
                  sp0_ref, kq0_ref, sp1_ref, kq1_ref, alt_ref, tab_ref, *refs, rc, n_cast):
    o_ref = refs[n_cast]
    zb_ref, p_ref, il_ref = refs[2 * n_cast + 1:]
    for src, dst in zip(refs[:n_cast], refs[n_cast + 1:2 * n_cast + 1]):
        dst[...] = src[...].astype(dst.dtype)
    Q = tab_ref.shape[1]
    R = RADIX
    chunks = [(r0, slice(r0, r0 + rc)) for r0 in range(0, Q, rc)]

    def put_z(zs, rows):
        alt = alt_ref[rows, :]
        for r in range(R):
            zb_ref[r, rows, :] = zs[r].astype(BF16)
        return [_colsum(z * alt) for z in zs]

    def spectrum_product(sp_ref):
        zb = [zb_ref[r] for r in range(R)]
        for _, rows in chunks:
            a = [_dot(tab_ref[r, rows, :], zb[r]) for r in range(R)]
            b = [_dot(tab_ref[R + r, rows, :], zb[r]) for r in range(R)]
            e, f = a[0] + a[2], a[0] - a[2]
            g, h = a[1] + a[3], a[1] - a[3]
            p, q = b[0] + b[2], b[0] - b[2]
            u, v = b[1] + b[3], b[1] - b[3]
            xr = [e + g, e - g, f + v, f - v]
            xi = [p + u, u - p, h - q, q + h]
            pr, pi = zip(*[_cmul(xr[s], xi[s], sp_ref[s, rows, :], sp_ref[R + s, rows, :])
                           for s in range(R)])
            s01p, s01m, s23p, s23m = pr[0] + pr[1], pr[0] - pr[1], pr[2] + pr[3], pr[2] - pr[3]
            t01p, t01m, t23p, t23m = pi[0] + pi[1], pi[0] - pi[1], pi[2] + pi[3], pi[2] - pi[3]
            for r, val in enumerate([s01p + s23p, s01m + t23p, s01p - s23p, s01m - t23p,
                                     t01m - t23m, t01p + s23m, t01m + t23m, t01p - s23m]):
                p_ref[r, rows, :] = val.astype(BF16)

    def conv_rows(rows, t, kq_ref):
        (xrq, xiq), (xr3, xi3) = _special_bins(t)
        prq, piq = _cmul(xrq, xiq, kq_ref[0:1, :], kq_ref[1:2, :])
        pr3, pi3 = _cmul(xr3, xi3, kq_ref[2:3, :], kq_ref[3:4, :])
        sp = [prq + pr3, SQH * (prq + piq + pi3 - pr3), piq - pi3, SQH * (piq - prq + pr3 + pi3)]
        alt = alt_ref[rows, :]
        return [_dot(tab_ref[2 * R + r, rows, :], p_ref[r]) + _dot(tab_ref[3 * R + r, rows, :], p_ref[R + r])
                + alt * sp[r] for r in range(R)]

    def add(acc, new):
        return new if acc is None else [x + y for x, y in zip(acc, new)]

    t = None
    for r0, rows in chunks:
        t = add(t, put_z(_short_conv_rows(v_ref, il_ref, wv_ref, bv_ref, r0, rc), rows))
    spectrum_product(sp0_ref)
    t1 = None
    for r0, rows in chunks:
        y = conv_rows(rows, t, kq0_ref)
        gate = _short_conv_rows(x1_ref, il_ref, w1_ref, b1_ref, r0, rc)
        t1 = add(t1, put_z([gate[r] * y[r] for r in range(R)], rows))
    spectrum_product(sp1_ref)
    for r0, rows in chunks:
        y = conv_rows(rows, t1, kq1_ref)
        gate = _short_conv_rows(x2_ref, il_ref, w2_ref, b2_ref, r0, rc)
        out = [gate[r] * y[r] for r in range(R)]
        for c in range(il_ref.shape[0]):
            lanes = slice(c * LANES, (c + 1) * LANES)
            for r in range(R):
                il_ref[c, pl.ds(r, rc, stride=R), :] = out[r][:, lanes]
            o_ref[0, R * r0:R * (r0 + rc), lanes] = il_ref[c].astype(o_ref.dtype)


def _hyena(p3, conv_w, conv_b, spectra, alt, tabs, n_ch, ct, to_cast):
    B, L, _ = p3.shape
    Q = L // RADIX
    nct = n_ch // ct
    rc = _pick(Q, 512)
    sp, kq = spectra
    cb = conv_b.reshape(1, -1).astype(F32)
    cw = conv_w.astype(F32)

    def pspec(g):
        return pl.BlockSpec((1, L, ct), lambda j, b, g=g: (b, 0, g * nct + j))

    def rowspec(rows, g):
        return pl.BlockSpec((rows, ct), lambda j, b, g=g: (0, g * nct + j))

    def specspec(g):
        return _resident((2 * RADIX, Q, ct), lambda j, b, g=g: (0, 0, g * nct + j))

    steps = nct * B
    cast_specs = []
    for w in to_cast:
        slab = w.shape[0] // steps
        assert slab * steps == w.shape[0] and slab % BF16_ROWS == 0, w.shape
        cast_specs.append(pl.BlockSpec((slab, w.shape[1]), lambda j, b: (j * B + b, 0)))

    return pl.pallas_call(
        functools.partial(_hyena_kernel, rc=rc, n_cast=len(to_cast)),
        grid=(nct, B),
        in_specs=[
            pspec(0), pspec(1), pspec(2),
            rowspec(3, 0), rowspec(3, 1), rowspec(3, 2),
            rowspec(1, 0), rowspec(1, 1), rowspec(1, 2),
            specspec(0), rowspec(RADIX, 0), specspec(1), rowspec(RADIX, 1),
            _resident((Q, 1), lambda j, b: (0, 0)),
            _resident(tabs.shape, lambda j, b: (0, 0, 0)),
        ] + cast_specs,
        out_specs=[pl.BlockSpec((1, L, ct), lambda j, b: (b, 0, j))] + cast_specs,
        out_shape=[jax.ShapeDtypeStruct((B, L, n_ch), BF16)]
                  + [jax.ShapeDtypeStruct(w.shape, BF16) for w in to_cast],
        scratch_shapes=[pltpu.VMEM((RADIX, Q, ct), BF16),
                        pltpu.VMEM((2 * RADIX, Q, ct), BF16),
                        pltpu.VMEM((ct // LANES, RADIX * rc, LANES), F32)],
        name="hyena",
        compiler_params=_params(58, 2),
    )(p3, p3, p3, cw, cw, cw, cb, cb, cb, sp, kq, sp, kq, alt, tabs, *to_cast)


def _gelu(x):
    return 0.5 * x * (1.0 + lax.erf(x * math.sqrt(0.5)))


def _spatial_gating(u_ref, v_ref, g_ref, b_ref, ws_ref, bs_ref, o_ref, heads, chunk):
    u = _gelu(u_ref[...].astype(F32))
    v = _gelu(v_ref[...].astype(F32))
    mu = jnp.mean(v, axis=-1, keepdims=True)
    vc = v - mu
    var = jnp.mean(vc * vc, axis=-1, keepdims=True)
    vn = (vc * lax.rsqrt(var + EPS) * g_ref[...] + b_ref[...]).astype(BF16)
    tm, ds = vn.shape
    hd = ds // heads
    for n in range(tm // chunk):
        rows = slice(n * chunk, (n + 1) * chunk)
        for h in range(heads):
            cols = slice(h * hd, (h + 1) * hd)
            s = _dot(ws_ref[h], vn[rows, cols]) + bs_ref[:, h:h + 1]
            o_ref[rows, cols] = (u[rows, cols] * s).astype(o_ref.dtype)


def _merge_kernel(x_ref, yh_ref, u_ref, v_ref, gh0_ref, gh1_ref, gs0_ref, gs1_ref,
                  lng_ref, lnb_ref, ws_ref, bs_ref, wbh_ref, wbs_ref, wo_ref, o_ref, ys_ref, *, heads, chunk):
    _spatial_gating(u_ref, v_ref, lng_ref, lnb_ref, ws_ref, bs_ref, ys_ref, heads, chunk)
    gh = jnp.concatenate([gh0_ref[...], gh1_ref[...]], axis=1).astype(F32)
    gs = jnp.concatenate([gs0_ref[...], gs1_ref[...]], axis=1).astype(F32)
    a = _dot(yh_ref[...], wbh_ref[...]) * gh + _dot(ys_ref[...], wbs_ref[...]) * gs
    o_ref[...] = x_ref[...] + _dot(a.astype(BF16), wo_ref[...])


def _merge(x2, yh, proj, sgu_col, gate_col, ln_g, ln_b, w_s, b_s, wbh, wbs, wo, tm):
    M, D = x2.shape
    ds = wbs.shape[0]
    heads, chunk, _ = w_s.shape
    gw = D // 2
    gb = gate_col // gw
    sb = sgu_col // ds
    assert gate_col % gw == 0 and sgu_col % ds == 0 and tm % chunk == 0
    full = lambda i: (0, 0)

    def gate(c):
        return pl.BlockSpec((tm, gw), lambda i, c=c: (i, gb + c))

    return pl.pallas_call(
        functools.partial(_merge_kernel, heads=heads, chunk=chunk),
        grid=(M // tm,),
        in_specs=[
            pl.BlockSpec((tm, D), lambda i: (i, 0)),
            pl.BlockSpec((tm, yh.shape[1]), lambda i: (i, 0)),
            pl.BlockSpec((tm, ds), lambda i: (i, sb)),
            pl.BlockSpec((tm, ds), lambda i: (i, sb + 1)),
            gate(0), gate(1), gate(2), gate(3),
            _resident((1, ds), full),
            _resident((1, ds), full),
            _resident((heads, chunk, chunk), lambda i: (0, 0, 0)),
            _resident((chunk, heads), full),
            _resident(wbh.shape, full),
            _resident(wbs.shape, full),
            _resident(wo.shape, full),
        ],
        out_specs=pl.BlockSpec((tm, D), lambda i: (i, 0)),
        out_shape=jax.ShapeDtypeStruct((M, D), F32),
        scratch_shapes=[pltpu.VMEM((tm, ds), BF16)],
        name="merge",
        compiler_params=_params(60, 1),
    )(x2, yh, proj, proj, proj, proj, proj, proj,
      ln_g.reshape(1, ds).astype(F32), ln_b.reshape(1, ds).astype(F32),
      w_s.astype(BF16), b_s.T.astype(F32), wbh, wbs, wo)


def _ffn_kernel(x_ref, g_ref, wg_ref, wu_ref, wo_ref, gf_ref, o_ref, h_ref, *, final_norm):
    f = pl.program_id(1)

    def tile(h):
        gate = _dot(h, wg_ref[...])
        up = _dot(h, wu_ref[...])
        a = (gate * _sigmoid(gate) * up).astype(BF16)
        return _dot(a, wo_ref[...])

    @pl.when(f == 0)
    def _():
        x = x_ref[...]
        h = _rms(x, g_ref[...]).astype(BF16)
        h_ref[...] = h
        o_ref[...] = x + tile(h)

    @pl.when(f > 0)
    def _():
        o_ref[...] += tile(h_ref[...])

    if final_norm:
        @pl.when(f == pl.num_programs(1) - 1)
        def _():
            o_ref[...] = _rms(o_ref[...], gf_ref[...])


def _ffn(x2, g, w_in, w_out, g_final, final_norm, tm, tf):
    M, D = x2.shape
    FF = w_out.shape[0]
    nf = FF // tf
    full = lambda i, f: (0, 0)
    return pl.pallas_call(
        functools.partial(_ffn_kernel, final_norm=final_norm),
        grid=(M // tm, nf),
        in_specs=[
            pl.BlockSpec((tm, D), lambda i, f: (i, 0)),
            _resident((1, D), full),
            pl.BlockSpec((D, tf), lambda i, f: (0, f)),
            pl.BlockSpec((D, tf), lambda i, f: (0, nf + f)),
            pl.BlockSpec((tf, D), lambda i, f: (f, 0)),
            _resident((1, D), full),
        ],
        out_specs=pl.BlockSpec((tm, D), lambda i, f: (i, 0)),
        out_shape=jax.ShapeDtypeStruct((M, D), F32),
        scratch_shapes=[pltpu.VMEM((tm, D), BF16)],
        name="ffn",
        compiler_params=_params(60, 2),
    )(x2, g, w_in, w_in, w_out, g_final)


def _transform_tables(L):
    Q = L // RADIX
    k = np.arange(Q, dtype=np.int64)[:, None]
    m = np.arange(Q, dtype=np.int64)[None, :]

    def tab(fn, r):
        return fn(((k * (RADIX * m + r)) % (2 * L)) * (np.pi / L)).astype(np.float32)

    cs = [tab(np.cos, r) for r in range(RADIX)] + [tab(np.sin, r) for r in range(RADIX)]
    stack = np.stack(cs + [t.T for t in cs], axis=0)
    alt = (1.0 - 2.0 * (np.arange(Q) % 2)).astype(np.float32)[:, None]
    return jnp.asarray(stack).astype(BF16), jnp.asarray(alt)


def _position_features(L, emb):
    Q = L // RADIX
    row = np.arange(L)
    n = ((row % Q) * RADIX + row // Q).astype(np.float64)[:, None]
    t = n / (L - 1)
    bands = (emb - 1) // 2
    w = 2.0 * np.pi * n / L
    f = np.linspace(1e-4, bands - 1, bands)[None, :]
    z = np.concatenate([t, np.cos(f * w), -np.sin(f * w)], axis=-1)
    return jnp.asarray(z.astype(np.float32)), jnp.asarray(t.astype(np.float32))


def _pick(n, pref):
    t = min(n, pref)
    while n % t:
        t -= 1
    return t


def kernel(x, norm_mix_g, w_in, short_conv_w, short_conv_b, filt_w1, filt_b1, filt_w2, filt_b2, filt_w3, filt_b3, filt_freq, filt_w4, hyena_bias, sgu_ln_g, sgu_ln_b, sgu_w_s, sgu_b_s, w_branch_hyena, w_branch_sgu, w_out, norm_ffn_g, w_ffn_in, w_ffn_out, norm_final_g):
    B, L, D = x.shape
    depth = w_in.shape[0]
    n_ch = w_branch_hyena.shape[1]
    ds = w_branch_sgu.shape[1]
    assert n_ch == ds and 2 * n_ch == D and L % RADIX == 0 and depth >= 1
    n_mix = 3 * n_ch + 2 * ds
    M = B * L

    tabs, alt = _transform_tables(L)
    zfeat, t_col = _position_features(L, filt_w1.shape[1])
    max_decay = math.log(DECAY_TARGET) / FAST_DECAY
    min_decay = math.log(DECAY_TARGET) / SLOW_DECAY
    absdelta = jnp.asarray(np.abs(np.linspace(min_decay, max_decay, n_ch)).astype(np.float32)[None, :])

    ct = _pick(n_ch, 256)
    tm = _pick(M, 1024)
    tn = LANES * _pick(w_in.shape[2] // LANES, 2304 // LANES)

    x2 = x.reshape(M, D)
    for l in range(depth):
        spectra, w_in_bf16 = _hyena_filters(zfeat, t_col, absdelta, alt, tabs,
                                            filt_w1[l], filt_b1[l], filt_w2[l], filt_b2[l],
                                            filt_w3[l], filt_b3[l], filt_freq[l], filt_w4[l],
                                            hyena_bias[l], ct, w_in[l])
        proj = _proj(x2, norm_mix_g[l].reshape(1, D), w_in_bf16, n_mix, tm, tn)
        y_hy, wbh, wbs, wo, wfi, wfo = _hyena(
            proj.reshape(B, L, proj.shape[1]), short_conv_w[l], short_conv_b[l], spectra,
            alt, tabs, n_ch, ct,
            [w_branch_hyena[l], w_branch_sgu[l], w_out[l], w_ffn_in[l], w_ffn_out[l]])
        x2 = _merge(x2, y_hy.reshape(M, n_ch), proj, 3 * n_ch, n_mix,
                    sgu_ln_g[l], sgu_ln_b[l], sgu_w_s[l], sgu_b_s[l], wbh, wbs, wo, _pick(M, 512))
        last = l == depth - 1
        x2 = _ffn(x2, norm_ffn_g[l].reshape(1, D), wfi, wfo, norm_final_g.reshape(1, D), last,
                  _pick(M, 1024), _pick(w_ffn_out.shape[1], 512))
    return x2.reshape(B, L, D)
```

```python
import functools
import math

import numpy as np
import jax
import jax.numpy as jnp
from jax import lax
from jax.experimental import pallas as pl
from jax.experimental.pallas import tpu as pltpu

F32 = jnp.float32
BF16 = jnp.bfloat16

EPS = 1e-6
FAST_DECAY = 0.3
SLOW_DECAY = 1.5
DECAY_TARGET = 1e-2
MOD_SHIFT = 0.0
N_DIR = 2
HYENA_ORDER = 2
RADIX = 4
SQH = math.sqrt(0.5)
LANES = 128
BF16_ROWS = 16
MIB = 1024 * 1024


def _params(vmem_mib, n_grid):
    return pltpu.CompilerParams(
        dimension_semantics=("arbitrary",) * n_grid,
        vmem_limit_bytes=vmem_mib * MIB)


def _resident(block_shape, index_map):
    return pl.BlockSpec(block_shape, index_map, pipeline_mode=pl.Buffered(1))


def _rms(x, g):
    ms = jnp.mean(x * x, axis=-1, keepdims=True)
    return x * lax.rsqrt(ms + EPS) * g


def _dot(a, b):
    return jnp.dot(a, b, preferred_element_type=F32)


def _sigmoid(x):
    return 0.5 * jnp.tanh(0.5 * x) + 0.5


def _colsum(a):
    return jnp.sum(a, axis=0, keepdims=True)


def _cmul(xr, xi, kr, ki):
    return xr * kr - xi * ki, xr * ki + xi * kr


def _special_bins(t):
    d, s = SQH * (t[1] - t[3]), SQH * (t[1] + t[3])
    return (t[0] + d, t[2] + s), (t[0] - d, s - t[2])


def _filter_feat_kernel(z_ref, w1_ref, b1_ref, w2_ref, b2_ref, w3_ref, b3_ref, fr_ref, o_ref):
    hp = lax.Precision.HIGHEST
    fr = fr_ref[...]
    h = jnp.sin(fr * (jnp.dot(z_ref[...], w1_ref[...], precision=hp, preferred_element_type=F32) + b1_ref[...]))
    h = jnp.sin(fr * (jnp.dot(h, w2_ref[...], precision=hp, preferred_element_type=F32) + b2_ref[...]))
    h = jnp.sin(fr * (jnp.dot(h, w3_ref[...], precision=hp, preferred_element_type=F32) + b3_ref[...]))
    o_ref[...] = h


def _filter_spec_kernel(f_ref, w4f_ref, w4b_ref, t_ref, ad_ref, bias_ref, alt_ref, tab_ref, wsrc_ref,
                        sp_ref, kq_ref, wdst_ref):
    wdst_ref[...] = wsrc_ref[...].astype(wdst_ref.dtype)
    Q = f_ref.shape[1]
    inv_n = 1.0 / (2 * RADIX * Q)
    ad = ad_ref[...]
    row = lax.broadcasted_iota(jnp.int32, (Q, ad.shape[1]), 0)

    def split(x):
        hi = x.astype(BF16)
        return hi, (x - hi.astype(F32)).astype(BF16)

    def dot_split(a, b):
        return ((_dot(a[1], b[1]) + _dot(a[1], b[0])) + _dot(a[0], b[1])) + _dot(a[0], b[0])

    w4f, w4b = split(w4f_ref[...]), split(w4b_ref[...])
    hf, hb = [], []
    for r in range(RADIX):
        f = split(f_ref[r][:, 0:w4f_ref.shape[0]])
        decay = jnp.exp(-t_ref[r] * ad) + MOD_SHIFT
        hf.append(dot_split(f, w4f) * decay)
        hb.append(dot_split(f, w4b) * decay)
    hb[0] = jnp.where(row == 0, 0.0, hb[0])
    norm = sum(_colsum(jnp.abs(h)) for h in hf + hb)
    inv = 1.0 / norm
    ev = [(hf[r] + hb[r]) * inv for r in range(RADIX)]
    od = [(hf[r] - hb[r]) * inv for r in range(RADIX)]
    evb = [e.astype(BF16) for e in ev]
    odb = [o.astype(BF16) for o in od]

    def cos_t(r, zb):
        return _dot(tab_ref[r], zb[r])

    def sin_t(r, zb):
        return _dot(tab_ref[RADIX + r], zb[r])

    a0, a2 = cos_t(0, evb), cos_t(2, evb)
    e, f_ = a0 + a2, a0 - a2
    g = cos_t(1, evb) + cos_t(3, evb)
    v = sin_t(1, evb) - sin_t(3, evb)
    b0, b2 = sin_t(0, odb), sin_t(2, odb)
    p, q = b0 + b2, b0 - b2
    u = sin_t(1, odb) + sin_t(3, odb)
    h = cos_t(1, odb) - cos_t(3, odb)
    first = row[:, :1] == 0
    w_end = jnp.where(first, inv_n, 2.0 * inv_n)
    w_dup = jnp.where(first, 0.0, 2.0 * inv_n)
    w_mid = 2.0 * inv_n
    bias = bias_ref[...]
    for s, (kr, ki, w) in enumerate([(e + g, p + u, w_end), (e - g, u - p, w_end),
                                     (f_ + v, h - q, w_mid), (f_ - v, q + h, w_dup)]):
        sp_ref[s] = (kr + bias) * w
        sp_ref[RADIX + s] = ki * w
    alt = alt_ref[...]
    (krq, _), (kr3, _) = _special_bins([_colsum(x * alt) for x in ev])
    (_, kiq), (_, ki3) = _special_bins([_colsum(x * alt) for x in od])
    for i, kval in enumerate((krq + bias, kiq, kr3 + bias, ki3)):
        kq_ref[i:i + 1, :] = kval * w_mid


def _hyena_filters(zfeat, t_col, absdelta, alt, tabs, w1, b1, w2, b2, w3, b3, freq, w4, hbias, tc, w_cast):
    L = zfeat.shape[0]
    Q = L // RADIX
    c2 = w4.shape[1] // N_DIR
    n_ch = c2 // HYENA_ORDER

    def pad2(a, r, c):
        return jnp.pad(a.astype(F32), ((0, r - a.shape[0]), (0, c - a.shape[1])))

    feat = pl.pallas_call(
        _filter_feat_kernel,
        out_shape=jax.ShapeDtypeStruct((L, LANES), F32),
        name="filter_feat",
        compiler_params=_params(32, 0),
    )(pad2(zfeat, L, LANES),
      pad2(w1, LANES, LANES), pad2(b1[None], 1, LANES),
      pad2(w2, LANES, LANES), pad2(b2[None], 1, LANES),
      pad2(w3, LANES, LANES), pad2(b3[None], 1, LANES), pad2(freq[None], 1, LANES))

    w4p = w4.astype(F32)
    fo = w4.shape[0]
    nt = c2 // tc
    nct = n_ch // tc
    slab = w_cast.shape[0] // nt
    assert slab * nt == w_cast.shape[0] and slab % BF16_ROWS == 0
    cast_spec = pl.BlockSpec((slab, w_cast.shape[1]), lambda j: (j, 0))
    sp, kq, w_bf16 = pl.pallas_call(
        _filter_spec_kernel,
        grid=(nt,),
        in_specs=[
            _resident((RADIX, Q, LANES), lambda j: (0, 0, 0)),
            pl.BlockSpec((fo, tc), lambda j: (0, j)),
            pl.BlockSpec((fo, tc), lambda j: (0, nt + j)),
            _resident((RADIX, Q, 1), lambda j: (0, 0, 0)),
            pl.BlockSpec((1, tc), lambda j: (0, j % nct)),
            pl.BlockSpec((1, tc), lambda j: (0, j)),
            _resident((Q, 1), lambda j: (0, 0)),
            _resident((2 * RADIX, Q, Q), lambda j: (0, 0, 0)),
            cast_spec,
        ],
        out_specs=[pl.BlockSpec((2 * RADIX, Q, tc), lambda j: (0, 0, j)),
                   pl.BlockSpec((RADIX, tc), lambda j: (0, j)), cast_spec],
        out_shape=[jax.ShapeDtypeStruct((2 * RADIX, Q, c2), F32),
                   jax.ShapeDtypeStruct((RADIX, c2), F32),
                   jax.ShapeDtypeStruct(w_cast.shape, BF16)],
        name="filter_spec",
        compiler_params=_params(58, 1),
    )(feat.reshape(RADIX, Q, LANES), w4p, w4p, t_col.reshape(RADIX, Q, 1), absdelta,
      hbias.reshape(1, c2).astype(F32), alt, tabs, w_cast)
    return (sp, kq), w_bf16


def _proj_kernel(x_ref, g_ref, w_ref, o_ref, h_ref, *, n_mix):
    j = pl.program_id(1)
    tn = o_ref.shape[1]
    has_gates = (j + 1) * tn > n_mix

    def with_gates(acc):
        col = j * tn + lax.broadcasted_iota(jnp.int32, acc.shape, 1)
        return jnp.where(col >= n_mix, _sigmoid(acc), acc)

    @pl.when(j == 0)
    def _():
        h = _rms(x_ref[...], g_ref[...]).astype(BF16)
        h_ref[...] = h
        acc = _dot(h, w_ref[...])
        o_ref[...] = (with_gates(acc) if tn > n_mix else acc).astype(BF16)

    @pl.when((j > 0) & jnp.logical_not(has_gates))
    def _():
        o_ref[...] = _dot(h_ref[...], w_ref[...]).astype(BF16)

    @pl.when((j > 0) & has_gates)
    def _():
        o_ref[...] = with_gates(_dot(h_ref[...], w_ref[...])).astype(BF16)


def _proj(x2, g, w, n_mix, tm, tn):
    M, D = x2.shape
    E = w.shape[1]
    return pl.pallas_call(
        functools.partial(_proj_kernel, n_mix=n_mix),
        grid=(M // tm, E // tn),
        in_specs=[
            pl.BlockSpec((tm, D), lambda i, j: (i, 0)),
            _resident((1, D), lambda i, j: (0, 0)),
            pl.BlockSpec((D, tn), lambda i, j: (0, j)),
        ],
        out_specs=pl.BlockSpec((tm, tn), lambda i, j: (i, j)),
        out_shape=jax.ShapeDtypeStruct((M, E), BF16),
        scratch_shapes=[pltpu.VMEM((tm, D), BF16)],
        name="proj",
        compiler_params=_params(56, 2),
    )(x2, g, w)


def _short_conv_rows(p_ref, il_ref, w_ref, b_ref, r0, rc):
    L = p_ref.shape[1]
    t0, t1 = RADIX * r0, RADIX * (r0 + rc)
    parts = [[] for _ in range(RADIX)]
    for c in range(il_ref.shape[0]):
        il_ref[c] = p_ref[0, t0:t1, c * LANES:(c + 1) * LANES].astype(F32)
        for r in range(RADIX):
            parts[r].append(il_ref[c, pl.ds(r, rc, stride=RADIX), :])
    p = [jnp.concatenate(x, axis=1) for x in parts]
    row = lax.broadcasted_iota(jnp.int32, p[0].shape, 0)
    if t0 == 0:
        before = 0.0
    else:
        before = p_ref[0, t0 - BF16_ROWS:t0, :].astype(F32)[BF16_ROWS - 1:BF16_ROWS, :]
    if t1 == L:
        after = 0.0
    else:
        after = p_ref[0, t1:t1 + BF16_ROWS, :].astype(F32)[0:1, :]
    last_prev = jnp.where(row == 0, before, pltpu.roll(p[RADIX - 1], 1, axis=0))
    first_next = jnp.where(row == rc - 1, after, pltpu.roll(p[0], rc - 1, axis=0))
    prev = [last_prev] + p[:-1]
    nxt = p[1:] + [first_next]
    w0, w1, w2, b = w_ref[0:1, :], w_ref[1:2, :], w_ref[2:3, :], b_ref[...]
    return [prev[r] * w0 + p[r] * w1 + nxt[r] * w2 + b for r in range(RADIX)]


def _hyena_kernel(v_ref, x1_ref, x2_ref, wv_ref, w1_ref, w2_ref, bv_ref, b1_ref, b2_ref,
                  sp0_ref, kq0_ref, sp1_ref, kq1_ref, alt_ref, tab_ref, *refs, rc, n_cast):
    o_ref = refs[n_cast]
    zb_ref, p_ref, il_ref = refs[2 * n_cast + 1:]
    for src, dst in zip(refs[:n_cast], refs[n_cast + 1:2 * n_cast + 1]):
        dst[...] = src[...].astype(dst.dtype)
    Q = tab_ref.shape[1]
    R = RADIX
    chunks = [(r0, slice(r0, r0 + rc)) for r0 in range(0, Q, rc)]

    def put_z(zs, rows):
        alt = alt_ref[rows, :]
        for r in range(R):
            zb_ref[r, rows, :] = zs[r].astype(BF16)
        return [_colsum(z * alt) for z in zs]

    def spectrum_product(sp_ref):
        zb = [zb_ref[r] for r in range(R)]
        for _, rows in chunks:
            a = [_dot(tab_ref[r, rows, :], zb[r]) for r in range(R)]
            b = [_dot(tab_ref[R + r, rows, :], zb[r]) for r in range(R)]
            e, f = a[0] + a[2], a[0] - a[2]
            g, h = a[1] + a[3], a[1] - a[3]
            p, q = b[0] + b[2], b[0] - b[2]
            u, v = b[1] + b[3], b[1] - b[3]
            xr = [e + g, e - g, f + v, f - v]
            xi = [p + u, u - p, h - q, q + h]
            pr, pi = zip(*[_cmul(xr[s], xi[s], sp_ref[s, rows, :], sp_ref[R + s, rows, :])
                           for s in range(R)])
            s01p, s01m, s23p, s23m = pr[0] + pr[1], pr[0] - pr[1], pr[2] + pr[3], pr[2] - pr[3]
            t01p, t01m, t23p, t23m = pi[0] + pi[1], pi[0] - pi[1], pi[2] + pi[3], pi[2] - pi[3]
            for r, val in enumerate([s01p + s23p, s01m + t23p, s01p - s23p, s01m - t23p,
                                     t01m - t23m, t01p + s23m, t01m + t23m, t01p - s23m]):
                p_ref[r, rows, :] = val.astype(BF16)

    def conv_rows(rows, t, kq_ref):
        (xrq, xiq), (xr3, xi3) = _special_bins(t)
        prq, piq = _cmul(xrq, xiq, kq_ref[0:1, :], kq_ref[1:2, :])
        pr3, pi3 = _cmul(xr3, xi3, kq_ref[2:3, :], kq_ref[3:4, :])
        sp = [prq + pr3, SQH * (prq + piq + pi3 - pr3), piq - pi3, SQH * (piq - prq + pr3 + pi3)]
        alt = alt_ref[rows, :]
        return [_dot(tab_ref[2 * R + r, rows, :], p_ref[r]) + _dot(tab_ref[3 * R + r, rows, :], p_ref[R + r])
                + alt * sp[r] for r in range(R)]

    def add(acc, new):
        return new if acc is None else [x + y for x, y in zip(acc, new)]

    t = None
    for r0, rows in chunks:
        t = add(t, put_z(_short_conv_rows(v_ref, il_ref, wv_ref, bv_ref, r0, rc), rows))
    spectrum_product(sp0_ref)
    t1 = None
    for r0, rows in chunks:
        y = conv_rows(rows, t, kq0_ref)
        gate = _short_conv_rows(x1_ref, il_ref, w1_ref, b1_ref, r0, rc)
        t1 = add(t1, put_z([gate[r] * y[r] for r in range(R)], rows))
    spectrum_product(sp1_ref)
    for r0, rows in chunks:
        y = conv_rows(rows, t1, kq1_ref)
        gate = _short_conv_rows(x2_ref, il_ref, w2_ref, b2_ref, r0, rc)
        out = [gate[r] * y[r] for r in range(R)]
        for c in range(il_ref.shape[0]):
            lanes = slice(c * LANES, (c + 1) * LANES)
            for r in range(R):
                il_ref[c, pl.ds(r, rc, stride=R), :] = out[r][:, lanes]
            o_ref[0, R * r0:R * (r0 + rc), lanes] = il_ref[c].astype(o_ref.dtype)


def _hyena(p3, conv_w, conv_b, spectra, alt, tabs, n_ch, ct, to_cast):
    B, L, _ = p3.shape
    Q = L // RADIX
    nct = n_ch // ct
    rc = _pick(Q, 512)
    sp, kq = spectra
    cb = conv_b.reshape(1, -1).astype(F32)
    cw = conv_w.astype(F32)

    def pspec(g):
        return pl.BlockSpec((1, L, ct), lambda j, b, g=g: (b, 0, g * nct + j))

    def rowspec(rows, g):
        return pl.BlockSpec((rows, ct), lambda j, b, g=g: (0, g * nct + j))

    def specspec(g):
        return pl.BlockSpec((2 * RADIX, Q, ct), lambda j, b, g=g: (0, 0, g * nct + j))

    steps = nct * B
    cast_specs = []
    for w in to_cast:
        slab = w.shape[0] // steps
        assert slab * steps == w.shape[0] and slab % BF16_ROWS == 0, w.shape
        cast_specs.append(pl.BlockSpec((slab, w.shape[1]), lambda j, b: (j * B + b, 0)))

    return pl.pallas_call(
        functools.partial(_hyena_kernel, rc=rc, n_cast=len(to_cast)),
        grid=(nct, B),
        in_specs=[
            pspec(0), pspec(1), pspec(2),
            rowspec(3, 0), rowspec(3, 1), rowspec(3, 2),
            rowspec(1, 0), rowspec(1, 1), rowspec(1, 2),
            specspec(0), rowspec(RADIX, 0), specspec(1), rowspec(RADIX, 1),
            _resident((Q, 1), lambda j, b: (0, 0)),
            _resident(tabs.shape, lambda j, b: (0, 0, 0)),
        ] + cast_specs,
        out_specs=[pl.BlockSpec((1, L, ct), lambda j, b: (b, 0, j))] + cast_specs,
        out_shape=[jax.ShapeDtypeStruct((B, L, n_ch), BF16)]
                  + [jax.ShapeDtypeStruct(w.shape, BF16) for w in to_cast],
        scratch_shapes=[pltpu.VMEM((RADIX, Q, ct), BF16),
                        pltpu.VMEM((2 * RADIX, Q, ct), BF16),
                        pltpu.VMEM((ct // LANES, RADIX * rc, LANES), F32)],
        name="hyena",
        compiler_params=_params(60, 2),
    )(p3, p3, p3, cw, cw, cw, cb, cb, cb, sp, kq, sp, kq, alt, tabs, *to_cast)


def _gelu(x):
    return 0.5 * x * (1.0 + lax.erf(x * math.sqrt(0.5)))


def _spatial_gating(u_ref, v_ref, g_ref, b_ref, ws_ref, bs_ref, o_ref, heads, chunk):
    u = _gelu(u_ref[...].astype(F32))
    v = _gelu(v_ref[...].astype(F32))
    mu = jnp.mean(v, axis=-1, keepdims=True)
    vc = v - mu
    var = jnp.mean(vc * vc, axis=-1, keepdims=True)
    vn = (vc * lax.rsqrt(var + EPS) * g_ref[...] + b_ref[...]).astype(BF16)
    tm, ds = vn.shape
    hd = ds // heads
    for n in range(tm // chunk):
        rows = slice(n * chunk, (n + 1) * chunk)
        for h in range(heads):
            cols = slice(h * hd, (h + 1) * hd)
            s = _dot(ws_ref[h], vn[rows, cols]) + bs_ref[:, h:h + 1]
            o_ref[rows, cols] = (u[rows, cols] * s).astype(o_ref.dtype)


def _merge_kernel(x_ref, yh_ref, u_ref, v_ref, gh0_ref, gh1_ref, gs0_ref, gs1_ref,
                  lng_ref, lnb_ref, ws_ref, bs_ref, wbh_ref, wbs_ref, wo_ref, o_ref, ys_ref, *, heads, chunk):
    _spatial_gating(u_ref, v_ref, lng_ref, lnb_ref, ws_ref, bs_ref, ys_ref, heads, chunk)
    gh = jnp.concatenate([gh0_ref[...], gh1_ref[...]], axis=1).astype(F32)
    gs = jnp.concatenate([gs0_ref[...], gs1_ref[...]], axis=1).astype(F32)
    a = _dot(yh_ref[...], wbh_ref[...]) * gh + _dot(ys_ref[...], wbs_ref[...]) * gs
    o_ref[...] = x_ref[...] + _dot(a.astype(BF16), wo_ref[...])


def _merge(x2, yh, proj, sgu_col, gate_col, ln_g, ln_b, w_s, b_s, wbh, wbs, wo, tm):
    M, D = x2.shape
    ds = wbs.shape[0]
    heads, chunk, _ = w_s.shape
    gw = D // 2
    gb = gate_col // gw
    sb = sgu_col // ds
    assert gate_col % gw == 0 and sgu_col % ds == 0 and tm % chunk == 0
    full = lambda i: (0, 0)

    def gate(c):
        return pl.BlockSpec((tm, gw), lambda i, c=c: (i, gb + c))

    return pl.pallas_call(
        functools.partial(_merge_kernel, heads=heads, chunk=chunk),
        grid=(M // tm,),
        in_specs=[
            pl.BlockSpec((tm, D), lambda i: (i, 0)),
            pl.BlockSpec((tm, yh.shape[1]), lambda i: (i, 0)),
            pl.BlockSpec((tm, ds), lambda i: (i, sb)),
            pl.BlockSpec((tm, ds), lambda i: (i, sb + 1)),
            gate(0), gate(1), gate(2), gate(3),
            _resident((1, ds), full),
            _resident((1, ds), full),
            _resident((heads, chunk, chunk), lambda i: (0, 0, 0)),
            _resident((chunk, heads), full),
            _resident(wbh.shape, full),
            _resident(wbs.shape, full),
            _resident(wo.shape, full),
        ],
        out_specs=pl.BlockSpec((tm, D), lambda i: (i, 0)),
        out_shape=jax.ShapeDtypeStruct((M, D), F32),
        scratch_shapes=[pltpu.VMEM((tm, ds), BF16)],
        name="merge",
        compiler_params=_params(60, 1),
    )(x2, yh, proj, proj, proj, proj, proj, proj,
      ln_g.reshape(1, ds).astype(F32), ln_b.reshape(1, ds).astype(F32),
      w_s.astype(BF16), b_s.T.astype(F32), wbh, wbs, wo)


def _ffn_kernel(x_ref, g_ref, wg_ref, wu_ref, wo_ref, gf_ref, o_ref, h_ref, *, final_norm):
    f = pl.program_id(1)

    def tile(h):
        gate = _dot(h, wg_ref[...])
        up = _dot(h, wu_ref[...])
        a = (gate * _sigmoid(gate) * up).astype(BF16)
        return _dot(a, wo_ref[...])

    @pl.when(f == 0)
    def _():
        x = x_ref[...]
        h = _rms(x, g_ref[...]).astype(BF16)
        h_ref[...] = h
        o_ref[...] = x + tile(h)

    @pl.when(f > 0)
    def _():
        o_ref[...] += tile(h_ref[...])

    if final_norm:
        @pl.when(f == pl.num_programs(1) - 1)
        def _():
            o_ref[...] = _rms(o_ref[...], gf_ref[...])


def _ffn(x2, g, w_in, w_out, g_final, final_norm, tm, tf):
    M, D = x2.shape
    FF = w_out.shape[0]
    nf = FF // tf
    full = lambda i, f: (0, 0)
    return pl.pallas_call(
        functools.partial(_ffn_kernel, final_norm=final_norm),
        grid=(M // tm, nf),
        in_specs=[
            pl.BlockSpec((tm, D), lambda i, f: (i, 0)),
            _resident((1, D), full),
            pl.BlockSpec((D, tf), lambda i, f: (0, f)),
            pl.BlockSpec((D, tf), lambda i, f: (0, nf + f)),
            pl.BlockSpec((tf, D), lambda i, f: (f, 0)),
            _resident((1, D), full),
        ],
        out_specs=pl.BlockSpec((tm, D), lambda i, f: (i, 0)),
        out_shape=jax.ShapeDtypeStruct((M, D), F32),
        scratch_shapes=[pltpu.VMEM((tm, D), BF16)],
        name="ffn",
        compiler_params=_params(60, 2),
    )(x2, g, w_in, w_in, w_out, g_final)


def _transform_tables(L):
    Q = L // RADIX
    k = np.arange(Q, dtype=np.int64)[:, None]
    m = np.arange(Q, dtype=np.int64)[None, :]

    def tab(fn, r):
        return fn(((k * (RADIX * m + r)) % (2 * L)) * (np.pi / L)).astype(np.float32)

    cs = [tab(np.cos, r) for r in range(RADIX)] + [tab(np.sin, r) for r in range(RADIX)]
    stack = np.stack(cs + [t.T for t in cs], axis=0)
    alt = (1.0 - 2.0 * (np.arange(Q) % 2)).astype(np.float32)[:, None]
    return jnp.asarray(stack).astype(BF16), jnp.asarray(alt)


def _position_features(L, emb):
    Q = L // RADIX
    row = np.arange(L)
    n = ((row % Q) * RADIX + row // Q).astype(np.float64)[:, None]
    t = n / (L - 1)
    bands = (emb - 1) // 2
    w = 2.0 * np.pi * n / L
    f = np.linspace(1e-4, bands - 1, bands)[None, :]
    z = np.concatenate([t, np.cos(f * w), -np.sin(f * w)], axis=-1)
    return jnp.asarray(z.astype(np.float32)), jnp.asarray(t.astype(np.float32))


def _pick(n, pref):
    t = min(n, pref)
    while n % t:
        t -= 1
    return t


def kernel(x, norm_mix_g, w_in, short_conv_w, short_conv_b, filt_w1, filt_b1, filt_w2, filt_b2, filt_w3, filt_b3, filt_freq, filt_w4, hyena_bias, sgu_ln_g, sgu_ln_b, sgu_w_s, sgu_b_s, w_branch_hyena, w_branch_sgu, w_out, norm_ffn_g, w_ffn_in, w_ffn_out, norm_final_g):
    B, L, D = x.shape
    depth = w_in.shape[0]
    n_ch = w_branch_hyena.shape[1]
    ds = w_branch_sgu.shape[1]
    assert n_ch == ds and 2 * n_ch == D and L % RADIX == 0 and depth >= 1
    n_mix = 3 * n_ch + 2 * ds
    M = B * L

    tabs, alt = _transform_tables(L)
    zfeat, t_col = _position_features(L, filt_w1.shape[1])
    max_decay = math.log(DECAY_TARGET) / FAST_DECAY
    min_decay = math.log(DECAY_TARGET) / SLOW_DECAY
    absdelta = jnp.asarray(np.abs(np.linspace(min_decay, max_decay, n_ch)).astype(np.float32)[None, :])

    ct = _pick(n_ch, 256)
    tm = _pick(M, 1024)
    tn = LANES * _pick(w_in.shape[2] // LANES, 2304 // LANES)

    x2 = x.reshape(M, D)
    for l in range(depth):
        spectra, w_in_bf16 = _hyena_filters(zfeat, t_col, absdelta, alt, tabs,
                                            filt_w1[l], filt_b1[l], filt_w2[l], filt_b2[l],
                                            filt_w3[l], filt_b3[l], filt_freq[l], filt_w4[l],
                                            hyena_bias[l], ct, w_in[l])
        proj = _proj(x2, norm_mix_g[l].reshape(1, D), w_in_bf16, n_mix, tm, tn)
        y_hy, wbh, wbs, wo, wfi, wfo = _hyena(
            proj.reshape(B, L, proj.shape[1]), short_conv_w[l], short_conv_b[l], spectra,
            alt, tabs, n_ch, ct,
            [w_branch_hyena[l], w_branch_sgu[l], w_out[l], w_ffn_in[l], w_ffn_out[l]])
        x2 = _merge(x2, y_hy.reshape(M, n_ch), proj, 3 * n_ch, n_mix,
                    sgu_ln_g[l], sgu_ln_b[l], sgu_w_s[l], sgu_b_s[l], wbh, wbs, wo, _pick(M, 512))
        last = l == depth - 1
        x2 = _ffn(x2, norm_ffn_g[l].reshape(1, D), wfi, wfo, norm_final_g.reshape(1, D), last,
                  _pick(M, 1024), _pick(w_ffn_out.shape[1], 512))
    return x2.reshape(B, L, D)
```

```python
import functools
import math

import numpy as np
import jax
import jax.numpy as jnp
from jax import lax
from jax.experimental import pallas as pl
from jax.experimental.pallas import tpu as pltpu

F32 = jnp.float32
BF16 = jnp.bfloat16

EPS = 1e-6
FAST_DECAY = 0.3
SLOW_DECAY = 1.5
DECAY_TARGET = 1e-2
MOD_SHIFT = 0.0
N_DIR = 2
HYENA_ORDER = 2
RADIX = 4
SQH = math.sqrt(0.5)
LANES = 128
BF16_ROWS = 16
MIB = 1024 * 1024


def _params(vmem_mib, n_grid):
    return pltpu.CompilerParams(
        dimension_semantics=("arbitrary",) * n_grid,
        vmem_limit_bytes=vmem_mib * MIB)


def _resident(block_shape, index_map):
    return pl.BlockSpec(block_shape, index_map, pipeline_mode=pl.Buffered(1))


def _rms(x, g):
    ms = jnp.mean(x * x, axis=-1, keepdims=True)
    return x * lax.rsqrt(ms + EPS) * g


def _dot(a, b):
    return jnp.dot(a, b, preferred_element_type=F32)


def _sigmoid(x):
    return 0.5 * jnp.tanh(0.5 * x) + 0.5


def _colsum(a):
    return jnp.sum(a, axis=0, keepdims=True)


def _cmul(xr, xi, kr, ki):
    return xr * kr - xi * ki, xr * ki + xi * kr


def _special_bins(t):
    d, s = SQH * (t[1] - t[3]), SQH * (t[1] + t[3])
    return (t[0] + d, t[2] + s), (t[0] - d, s - t[2])


def _filter_feat_kernel(z_ref, w1_ref, b1_ref, w2_ref, b2_ref, w3_ref, b3_ref, fr_ref, o_ref):
    hp = lax.Precision.HIGHEST
    fr = fr_ref[...]
    h = jnp.sin(fr * (jnp.dot(z_ref[...], w1_ref[...], precision=hp, preferred_element_type=F32) + b1_ref[...]))
    h = jnp.sin(fr * (jnp.dot(h, w2_ref[...], precision=hp, preferred_element_type=F32) + b2_ref[...]))
    h = jnp.sin(fr * (jnp.dot(h, w3_ref[...], precision=hp, preferred_element_type=F32) + b3_ref[...]))
    o_ref[...] = h


def _filter_spec_kernel(f_ref, w4f_ref, w4b_ref, t_ref, ad_ref, bias_ref, alt_ref, tab_ref, wsrc_ref,
                        sp_ref, kq_ref, wdst_ref):
    wdst_ref[...] = wsrc_ref[...].astype(wdst_ref.dtype)
    Q = f_ref.shape[1]
    inv_n = 1.0 / (2 * RADIX * Q)
    ad = ad_ref[...]
    row = lax.broadcasted_iota(jnp.int32, (Q, ad.shape[1]), 0)

    def split(x):
        hi = x.astype(BF16)
        return hi, (x - hi.astype(F32)).astype(BF16)

    def dot_split(a, b):
        return ((_dot(a[1], b[1]) + _dot(a[1], b[0])) + _dot(a[0], b[1])) + _dot(a[0], b[0])

    w4f, w4b = split(w4f_ref[...]), split(w4b_ref[...])
    hf, hb = [], []
    for r in range(RADIX):
        f = split(f_ref[r][:, 0:w4f_ref.shape[0]])
        decay = jnp.exp(-t_ref[r] * ad) + MOD_SHIFT
        hf.append(dot_split(f, w4f) * decay)
        hb.append(dot_split(f, w4b) * decay)
    hb[0] = jnp.where(row == 0, 0.0, hb[0])
    norm = sum(_colsum(jnp.abs(h)) for h in hf + hb)
    inv = 1.0 / norm
    ev = [(hf[r] + hb[r]) * inv for r in range(RADIX)]
    od = [(hf[r] - hb[r]) * inv for r in range(RADIX)]
    evb = [e.astype(BF16) for e in ev]
    odb = [o.astype(BF16) for o in od]

    def cos_t(r, zb):
        return _dot(tab_ref[r], zb[r])

    def sin_t(r, zb):
        return _dot(tab_ref[RADIX + r], zb[r])

    a0, a2 = cos_t(0, evb), cos_t(2, evb)
    e, f_ = a0 + a2, a0 - a2
    g = cos_t(1, evb) + cos_t(3, evb)
    v = sin_t(1, evb) - sin_t(3, evb)
    b0, b2 = sin_t(0, odb), sin_t(2, odb)
    p, q = b0 + b2, b0 - b2
    u = sin_t(1, odb) + sin_t(3, odb)
    h = cos_t(1, odb) - cos_t(3, odb)
    first = row[:, :1] == 0
    w_end = jnp.where(first, inv_n, 2.0 * inv_n)
    w_dup = jnp.where(first, 0.0, 2.0 * inv_n)
    w_mid = 2.0 * inv_n
    bias = bias_ref[...]
    for s, (kr, ki, w) in enumerate([(e + g, p + u, w_end), (e - g, u - p, w_end),
                                     (f_ + v, h - q, w_mid), (f_ - v, q + h, w_dup)]):
        sp_ref[s] = (kr + bias) * w
        sp_ref[RADIX + s] = ki * w
    alt = alt_ref[...]
    (krq, _), (kr3, _) = _special_bins([_colsum(x * alt) for x in ev])
    (_, kiq), (_, ki3) = _special_bins([_colsum(x * alt) for x in od])
    for i, kval in enumerate((krq + bias, kiq, kr3 + bias, ki3)):
        kq_ref[i:i + 1, :] = kval * w_mid


def _hyena_filters(zfeat, t_col, absdelta, alt, tabs, w1, b1, w2, b2, w3, b3, freq, w4, hbias, tc, w_cast):
    L = zfeat.shape[0]
    Q = L // RADIX
    c2 = w4.shape[1] // N_DIR
    n_ch = c2 // HYENA_ORDER

    def pad2(a, r, c):
        return jnp.pad(a.astype(F32), ((0, r - a.shape[0]), (0, c - a.shape[1])))

    feat = pl.pallas_call(
        _filter_feat_kernel,
        out_shape=jax.ShapeDtypeStruct((L, LANES), F32),
        name="filter_feat",
        compiler_params=_params(32, 0),
    )(pad2(zfeat, L, LANES),
      pad2(w1, LANES, LANES), pad2(b1[None], 1, LANES),
      pad2(w2, LANES, LANES), pad2(b2[None], 1, LANES),
      pad2(w3, LANES, LANES), pad2(b3[None], 1, LANES), pad2(freq[None], 1, LANES))

    w4p = w4.astype(F32)
    fo = w4.shape[0]
    nt = c2 // tc
    nct = n_ch // tc
    slab = w_cast.shape[0] // nt
    assert slab * nt == w_cast.shape[0] and slab % BF16_ROWS == 0
    cast_spec = pl.BlockSpec((slab, w_cast.shape[1]), lambda j: (j, 0))
    sp, kq, w_bf16 = pl.pallas_call(
        _filter_spec_kernel,
        grid=(nt,),
        in_specs=[
            _resident((RADIX, Q, LANES), lambda j: (0, 0, 0)),
            pl.BlockSpec((fo, tc), lambda j: (0, j)),
            pl.BlockSpec((fo, tc), lambda j: (0, nt + j)),
            _resident((RADIX, Q, 1), lambda j: (0, 0, 0)),
            pl.BlockSpec((1, tc), lambda j: (0, j % nct)),
            pl.BlockSpec((1, tc), lambda j: (0, j)),
            _resident((Q, 1), lambda j: (0, 0)),
            _resident((2 * RADIX, Q, Q), lambda j: (0, 0, 0)),
            cast_spec,
        ],
        out_specs=[pl.BlockSpec((2 * RADIX, Q, tc), lambda j: (0, 0, j)),
                   pl.BlockSpec((RADIX, tc), lambda j: (0, j)), cast_spec],
        out_shape=[jax.ShapeDtypeStruct((2 * RADIX, Q, c2), F32),
                   jax.ShapeDtypeStruct((RADIX, c2), F32),
                   jax.ShapeDtypeStruct(w_cast.shape, BF16)],
        name="filter_spec",
        compiler_params=_params(58, 1),
    )(feat.reshape(RADIX, Q, LANES), w4p, w4p, t_col.reshape(RADIX, Q, 1), absdelta,
      hbias.reshape(1, c2).astype(F32), alt, tabs, w_cast)
    return (sp, kq), w_bf16


def _proj_kernel(x_ref, g_ref, w_ref, o_ref, h_ref, *, n_mix):
    j = pl.program_id(1)
    tn = o_ref.shape[1]
    has_gates = (j + 1) * tn > n_mix

    def with_gates(acc):
        col = j * tn + lax.broadcasted_iota(jnp.int32, acc.shape, 1)
        return jnp.where(col >= n_mix, _sigmoid(acc), acc)

    @pl.when(j == 0)
    def _():
        h = _rms(x_ref[...], g_ref[...]).astype(BF16)
        h_ref[...] = h
        acc = _dot(h, w_ref[...])
        o_ref[...] = (with_gates(acc) if tn > n_mix else acc).astype(BF16)

    @pl.when((j > 0) & jnp.logical_not(has_gates))
    def _():
        o_ref[...] = _dot(h_ref[...], w_ref[...]).astype(BF16)

    @pl.when((j > 0) & has_gates)
    def _():
        o_ref[...] = with_gates(_dot(h_ref[...], w_ref[...])).astype(BF16)


def _proj(x2, g, w, n_mix, tm, tn):
    M, D = x2.shape
    E = w.shape[1]
    return pl.pallas_call(
        functools.partial(_proj_kernel, n_mix=n_mix),
        grid=(M // tm, E // tn),
        in_specs=[
            pl.BlockSpec((tm, D), lambda i, j: (i, 0)),
            _resident((1, D), lambda i, j: (0, 0)),
            pl.BlockSpec((D, tn), lambda i, j: (0, j)),
        ],
        out_specs=pl.BlockSpec((tm, tn), lambda i, j: (i, j)),
        out_shape=jax.ShapeDtypeStruct((M, E), BF16),
        scratch_shapes=[pltpu.VMEM((tm, D), BF16)],
        name="proj",
        compiler_params=_params(56, 2),
    )(x2, g, w)


def _short_conv_rows(p_ref, il_ref, w_ref, b_ref, r0, rc):
    L = p_ref.shape[1]
    t0, t1 = RADIX * r0, RADIX * (r0 + rc)
    parts = [[] for _ in range(RADIX)]
    for c in range(il_ref.shape[0]):
        il_ref[c] = p_ref[0, t0:t1, c * LANES:(c + 1) * LANES].astype(F32)
        for r in range(RADIX):
            parts[r].append(il_ref[c, pl.ds(r, rc, stride=RADIX), :])
    p = [jnp.concatenate(x, axis=1) for x in parts]
    row = lax.broadcasted_iota(jnp.int32, p[0].shape, 0)
    if t0 == 0:
        before = 0.0
    else:
        before = p_ref[0, t0 - BF16_ROWS:t0, :].astype(F32)[BF16_ROWS - 1:BF16_ROWS, :]
    if t1 == L:
        after = 0.0
    else:
        after = p_ref[0, t1:t1 + BF16_ROWS, :].astype(F32)[0:1, :]
    last_prev = jnp.where(row == 0, before, pltpu.roll(p[RADIX - 1], 1, axis=0))
    first_next = jnp.where(row == rc - 1, after, pltpu.roll(p[0], rc - 1, axis=0))
    prev = [last_prev] + p[:-1]
    nxt = p[1:] + [first_next]
    w0, w1, w2, b = w_ref[0:1, :], w_ref[1:2, :], w_ref[2:3, :], b_ref[...]
    return [prev[r] * w0 + p[r] * w1 + nxt[r] * w2 + b for r in range(RADIX)]


def _hyena_kernel(v_ref, x1_ref, x2_ref, wv_ref, w1_ref, w2_ref, bv_ref, b1_ref, b2_ref,
                  sp0_ref, kq0_ref, sp1_ref, kq1_ref, alt_ref, tab_ref, *refs, rc, n_cast):
    o_ref = refs[n_cast]
    zb_ref, p_ref, il_ref = refs[2 * n_cast + 1:]
    for src, dst in zip(refs[:n_cast], refs[n_cast + 1:2 * n_cast + 1]):
        dst[...] = src[...].astype(dst.dtype)
    Q = tab_ref.shape[1]
    R = RADIX
    chunks = [(r0, slice(r0, r0 + rc)) for r0 in range(0, Q, rc)]

    def put_z(zs, rows):
        alt = alt_ref[rows, :]
        for r in range(R):
            zb_ref[r, rows, :] = zs[r].astype(BF16)
        return [_colsum(z * alt) for z in zs]

    def spectrum_product(sp_ref):
        zb = [zb_ref[r] for r in range(R)]
        for _, rows in chunks:
            a = [_dot(tab_ref[r, rows, :], zb[r]) for r in range(R)]
            b = [_dot(tab_ref[R + r, rows, :], zb[r]) for r in range(R)]
            e, f = a[0] + a[2], a[0] - a[2]
            g, h = a[1] + a[3], a[1] - a[3]
            p, q = b[0] + b[2], b[0] - b[2]
            u, v = b[1] + b[3], b[1] - b[3]
            xr = [e + g, e - g, f + v, f - v]
            xi = [p + u, u - p, h - q, q + h]
            pr, pi = zip(*[_cmul(xr[s], xi[s], sp_ref[s, rows, :], sp_ref[R + s, rows, :])
                           for s in range(R)])
            s01p, s01m, s23p, s23m = pr[0] + pr[1], pr[0] - pr[1], pr[2] + pr[3], pr[2] - pr[3]
            t01p, t01m, t23p, t23m = pi[0] + pi[1], pi[0] - pi[1], pi[2] + pi[3], pi[2] - pi[3]
            for r, val in enumerate([s01p + s23p, s01m + t23p, s01p - s23p, s01m - t23p,
                                     t01m - t23m, t01p + s23m, t01m + t23m, t01p - s23m]):
                p_ref[r, rows, :] = val.astype(BF16)

    def conv_rows(rows, t, kq_ref):
        (xrq, xiq), (xr3, xi3) = _special_bins(t)
        prq, piq = _cmul(xrq, xiq, kq_ref[0:1, :], kq_ref[1:2, :])
        pr3, pi3 = _cmul(xr3, xi3, kq_ref[2:3, :], kq_ref[3:4, :])
        sp = [prq + pr3, SQH * (prq + piq + pi3 - pr3), piq - pi3, SQH * (piq - prq + pr3 + pi3)]
        alt = alt_ref[rows, :]
        return [_dot(tab_ref[2 * R + r, rows, :], p_ref[r]) + _dot(tab_ref[3 * R + r, rows, :], p_ref[R + r])
                + alt * sp[r] for r in range(R)]

    def add(acc, new):
        return new if acc is None else [x + y for x, y in zip(acc, new)]

    t = None
    for r0, rows in chunks:
        t = add(t, put_z(_short_conv_rows(v_ref, il_ref, wv_ref, bv_ref, r0, rc), rows))
    spectrum_product(sp0_ref)
    t1 = None
    for r0, rows in chunks:
        y = conv_rows(rows, t, kq0_ref)
        gate = _short_conv_rows(x1_ref, il_ref, w1_ref, b1_ref, r0, rc)
        t1 = add(t1, put_z([gate[r] * y[r] for r in range(R)], rows))
    spectrum_product(sp1_ref)
    for r0, rows in chunks:
        y = conv_rows(rows, t1, kq1_ref)
        gate = _short_conv_rows(x2_ref, il_ref, w2_ref, b2_ref, r0, rc)
        out = [gate[r] * y[r] for r in range(R)]
        for c in range(il_ref.shape[0]):
            lanes = slice(c * LANES, (c + 1) * LANES)
            for r in range(R):
                il_ref[c, pl.ds(r, rc, stride=R), :] = out[r][:, lanes]
            o_ref[0, R * r0:R * (r0 + rc), lanes] = il_ref[c].astype(o_ref.dtype)


def _hyena(p3, conv_w, conv_b, spectra, alt, tabs, n_ch, ct, to_cast):
    B, L, _ = p3.shape
    Q = L // RADIX
    nct = n_ch // ct
    rc = _pick(Q, 512)
    sp, kq = spectra
    cb = conv_b.reshape(1, -1).astype(F32)
    cw = conv_w.astype(F32)

    def pspec(g):
        return pl.BlockSpec((1, L, ct), lambda j, b, g=g: (b, 0, g * nct + j))

    def rowspec(rows, g):
        return pl.BlockSpec((rows, ct), lambda j, b, g=g: (0, g * nct + j))

    def specspec(g):
        return pl.BlockSpec((2 * RADIX, Q, ct), lambda j, b, g=g: (0, 0, g * nct + j))

    steps = nct * B
    cast_specs = []
    for w in to_cast:
        slab = w.shape[0] // steps
        assert slab * steps == w.shape[0] and slab % BF16_ROWS == 0, w.shape
        cast_specs.append(pl.BlockSpec((slab, w.shape[1]), lambda j, b: (j * B + b, 0)))

    return pl.pallas_call(
        functools.partial(_hyena_kernel, rc=rc, n_cast=len(to_cast)),
        grid=(nct, B),
        in_specs=[
            pspec(0), pspec(1), pspec(2),
            rowspec(3, 0), rowspec(3, 1), rowspec(3, 2),
            rowspec(1, 0), rowspec(1, 1), rowspec(1, 2),
            specspec(0), rowspec(RADIX, 0), specspec(1), rowspec(RADIX, 1),
            _resident((Q, 1), lambda j, b: (0, 0)),
            _resident(tabs.shape, lambda j, b: (0, 0, 0)),
        ] + cast_specs,
        out_specs=[pl.BlockSpec((1, L, ct), lambda j, b: (b, 0, j))] + cast_specs,
        out_shape=[jax.ShapeDtypeStruct((B, L, n_ch), BF16)]
                  + [jax.ShapeDtypeStruct(w.shape, BF16) for w in to_cast],
        scratch_shapes=[pltpu.VMEM((RADIX, Q, ct), BF16),
                        pltpu.VMEM((2 * RADIX, Q, ct), BF16),
                        pltpu.VMEM((ct // LANES, RADIX * rc, LANES), F32)],
        name="hyena",
        compiler_params=_params(60, 2),
    )(p3, p3, p3, cw, cw, cw, cb, cb, cb, sp, kq, sp, kq, alt, tabs, *to_cast)


def _gelu(x):
    return 0.5 * x * (1.0 + lax.erf(x * math.sqrt(0.5)))


def _spatial_gating(u_ref, v_ref, g_ref, b_ref, ws_ref, bs_ref, o_ref, heads, chunk):
    u = _gelu(u_ref[...].astype(F32))
    v = _gelu(v_ref[...].astype(F32))
    mu = jnp.mean(v, axis=-1, keepdims=True)
    vc = v - mu
    var = jnp.mean(vc * vc, axis=-1, keepdims=True)
    vn = (vc * lax.rsqrt(var + EPS) * g_ref[...] + b_ref[...]).astype(BF16)
    tm, ds = vn.shape
    hd = ds // heads
    for n in range(tm // chunk):
        rows = slice(n * chunk, (n + 1) * chunk)
        for h in range(heads):
            cols = slice(h * hd, (h + 1) * hd)
            s = _dot(ws_ref[h], vn[rows, cols]) + bs_ref[:, h:h + 1]
            o_ref[rows, cols] = (u[rows, cols] * s).astype(o_ref.dtype)


def _merge_kernel(x_ref, yh_ref, u_ref, v_ref, gh0_ref, gh1_ref, gs0_ref, gs1_ref,
                  lng_ref, lnb_ref, ws_ref, bs_ref, wbh_ref, wbs_ref, wo_ref, o_ref, ys_ref, *, heads, chunk):
    gh = jnp.concatenate([gh0_ref[...], gh1_ref[...]], axis=1).astype(F32)
    ah = _dot(yh_ref[...], wbh_ref[...]) * gh
    _spatial_gating(u_ref, v_ref, lng_ref, lnb_ref, ws_ref, bs_ref, ys_ref, heads, chunk)
    gs = jnp.concatenate([gs0_ref[...], gs1_ref[...]], axis=1).astype(F32)
    a = ah + _dot(ys_ref[...], wbs_ref[...]) * gs
    o_ref[...] = x_ref[...] + _dot(a.astype(BF16), wo_ref[...])


def _merge(x2, yh, proj, sgu_col, gate_col, ln_g, ln_b, w_s, b_s, wbh, wbs, wo, tm):
    M, D = x2.shape
    ds = wbs.shape[0]
    heads, chunk, _ = w_s.shape
    gw = D // 2
    gb = gate_col // gw
    sb = sgu_col // ds
    assert gate_col % gw == 0 and sgu_col % ds == 0 and tm % chunk == 0
    full = lambda i: (0, 0)

    def gate(c):
        return pl.BlockSpec((tm, gw), lambda i, c=c: (i, gb + c))

    return pl.pallas_call(
        functools.partial(_merge_kernel, heads=heads, chunk=chunk),
        grid=(M // tm,),
        in_specs=[
            pl.BlockSpec((tm, D), lambda i: (i, 0)),
            pl.BlockSpec((tm, yh.shape[1]), lambda i: (i, 0)),
            pl.BlockSpec((tm, ds), lambda i: (i, sb)),
            pl.BlockSpec((tm, ds), lambda i: (i, sb + 1)),
            gate(0), gate(1), gate(2), gate(3),
            _resident((1, ds), full),
            _resident((1, ds), full),
            _resident((heads, chunk, chunk), lambda i: (0, 0, 0)),
            _resident((chunk, heads), full),
            _resident(wbh.shape, full),
            _resident(wbs.shape, full),
            _resident(wo.shape, full),
        ],
        out_specs=pl.BlockSpec((tm, D), lambda i: (i, 0)),
        out_shape=jax.ShapeDtypeStruct((M, D), F32),
        scratch_shapes=[pltpu.VMEM((tm, ds), BF16)],
        name="merge",
        compiler_params=_params(60, 1),
    )(x2, yh, proj, proj, proj, proj, proj, proj,
      ln_g.reshape(1, ds).astype(F32), ln_b.reshape(1, ds).astype(F32),
      w_s.astype(BF16), b_s.T.astype(F32), wbh, wbs, wo)


def _ffn_kernel(x_ref, g_ref, wg_ref, wu_ref, wo_ref, gf_ref, o_ref, h_ref, *, nf, final_norm):
    f = pl.program_id(1)

    def tile(h):
        gate = _dot(h, wg_ref[...])
        up = _dot(h, wu_ref[...])
        a = (gate * _sigmoid(gate) * up).astype(BF16)
        return _dot(a, wo_ref[...])

    @pl.when(f == 0)
    def _():
        x = x_ref[...]
        h = _rms(x, g_ref[...]).astype(BF16)
        h_ref[...] = h
        o_ref[...] = x + tile(h)

    last = nf - 1
    fuse_norm = final_norm and nf > 1

    @pl.when((f > 0) & (f < last) if fuse_norm else f > 0)
    def _():
        o_ref[...] += tile(h_ref[...])

    if fuse_norm:
        @pl.when(f == last)
        def _():
            o_ref[...] = _rms(o_ref[...] + tile(h_ref[...]), gf_ref[...])
    elif final_norm:
        @pl.when(f == last)
        def _():
            o_ref[...] = _rms(o_ref[...], gf_ref[...])


def _ffn(x2, g, w_in, w_out, g_final, final_norm, tm, tf):
    M, D = x2.shape
    FF = w_out.shape[0]
    nf = FF // tf
    full = lambda i, f: (0, 0)
    return pl.pallas_call(
        functools.partial(_ffn_kernel, nf=nf, final_norm=final_norm),
        grid=(M // tm, nf),
        in_specs=[
            pl.BlockSpec((tm, D), lambda i, f: (i, 0)),
            _resident((1, D), full),
            pl.BlockSpec((D, tf), lambda i, f: (0, f)),
            pl.BlockSpec((D, tf), lambda i, f: (0, nf + f)),
            pl.BlockSpec((tf, D), lambda i, f: (f, 0)),
            _resident((1, D), full),
        ],
        out_specs=pl.BlockSpec((tm, D), lambda i, f: (i, 0)),
        out_shape=jax.ShapeDtypeStruct((M, D), F32),
        scratch_shapes=[pltpu.VMEM((tm, D), BF16)],
        name="ffn",
        compiler_params=_params(60, 2),
    )(x2, g, w_in, w_in, w_out, g_final)


def _transform_tables(L):
    Q = L // RADIX
    k = np.arange(Q, dtype=np.int64)[:, None]
    m = np.arange(Q, dtype=np.int64)[None, :]

    def tab(fn, r):
        return fn(((k * (RADIX * m + r)) % (2 * L)) * (np.pi / L)).astype(np.float32)

    cs = [tab(np.cos, r) for r in range(RADIX)] + [tab(np.sin, r) for r in range(RADIX)]
    stack = np.stack(cs + [t.T for t in cs], axis=0)
    alt = (1.0 - 2.0 * (np.arange(Q) % 2)).astype(np.float32)[:, None]
    return jnp.asarray(stack).astype(BF16), jnp.asarray(alt)


def _position_features(L, emb):
    Q = L // RADIX
    row = np.arange(L)
    n = ((row % Q) * RADIX + row // Q).astype(np.float64)[:, None]
    t = n / (L - 1)
    bands = (emb - 1) // 2
    w = 2.0 * np.pi * n / L
    f = np.linspace(1e-4, bands - 1, bands)[None, :]
    z = np.concatenate([t, np.cos(f * w), -np.sin(f * w)], axis=-1)
    return jnp.asarray(z.astype(np.float32)), jnp.asarray(t.astype(np.float32))


def _pick(n, pref):
    t = min(n, pref)
    while n % t:
        t -= 1
    return t


def kernel(x, norm_mix_g, w_in, short_conv_w, short_conv_b, filt_w1, filt_b1, filt_w2, filt_b2, filt_w3, filt_b3, filt_freq, filt_w4, hyena_bias, sgu_ln_g, sgu_ln_b, sgu_w_s, sgu_b_s, w_branch_hyena, w_branch_sgu, w_out, norm_ffn_g, w_ffn_in, w_ffn_out, norm_final_g):
    B, L, D = x.shape
    depth = w_in.shape[0]
    n_ch = w_branch_hyena.shape[1]
    ds = w_branch_sgu.shape[1]
    assert n_ch == ds and 2 * n_ch == D and L % RADIX == 0 and depth >= 1
    n_mix = 3 * n_ch + 2 * ds
    M = B * L

    tabs, alt = _transform_tables(L)
    zfeat, t_col = _position_features(L, filt_w1.shape[1])
    max_decay = math.log(DECAY_TARGET) / FAST_DECAY
    min_decay = math.log(DECAY_TARGET) / SLOW_DECAY
    absdelta = jnp.asarray(np.abs(np.linspace(min_decay, max_decay, n_ch)).astype(np.float32)[None, :])

    ct = _pick(n_ch, 256)
    tm = _pick(M, 1024)
    tn = LANES * _pick(w_in.shape[2] // LANES, 2304 // LANES)

    x2 = x.reshape(M, D)
    for l in range(depth):
        spectra, w_in_bf16 = _hyena_filters(zfeat, t_col, absdelta, alt, tabs,
                                            filt_w1[l], filt_b1[l], filt_w2[l], filt_b2[l],
                                            filt_w3[l], filt_b3[l], filt_freq[l], filt_w4[l],
                                            hyena_bias[l], ct, w_in[l])
        proj = _proj(x2, norm_mix_g[l].reshape(1, D), w_in_bf16, n_mix, tm, tn)
        y_hy, wbh, wbs, wo, wfi, wfo = _hyena(
            proj.reshape(B, L, proj.shape[1]), short_conv_w[l], short_conv_b[l], spectra,
            alt, tabs, n_ch, ct,
            [w_branch_hyena[l], w_branch_sgu[l], w_out[l], w_ffn_in[l], w_ffn_out[l]])
        x2 = _merge(x2, y_hy.reshape(M, n_ch), proj, 3 * n_ch, n_mix,
                    sgu_ln_g[l], sgu_ln_b[l], sgu_w_s[l], sgu_b_s[l], wbh, wbs, wo, _pick(M, 512))
        last = l == depth - 1
        x2 = _ffn(x2, norm_ffn_g[l].reshape(1, D), wfi, wfo, norm_final_g.reshape(1, D), last,
                  _pick(M, 1024), _pick(w_ffn_out.shape[1], 512))
    return x2.reshape(B, L, D)
```

```python
import functools
import math

import numpy as np
import jax
import jax.numpy as jnp
from jax import lax
from jax.experimental import pallas as pl
from jax.experimental.pallas import tpu as pltpu

F32 = jnp.float32
BF16 = jnp.bfloat16

EPS = 1e-6
FAST_DECAY = 0.3
SLOW_DECAY = 1.5
DECAY_TARGET = 1e-2
MOD_SHIFT = 0.0
N_DIR = 2
HYENA_ORDER = 2
RADIX = 4
SQH = math.sqrt(0.5)
LANES = 128
BF16_ROWS = 16
MIB = 1024 * 1024


def _params(vmem_mib, n_grid):
    return pltpu.CompilerParams(
        dimension_semantics=("arbitrary",) * n_grid,
        vmem_limit_bytes=vmem_mib * MIB)


def _resident(block_shape, index_map):
    return pl.BlockSpec(block_shape, index_map, pipeline_mode=pl.Buffered(1))


def _rms(x, g):
    ms = jnp.mean(x * x, axis=-1, keepdims=True)
    return x * lax.rsqrt(ms + EPS) * g


def _dot(a, b):
    return jnp.dot(a, b, preferred_element_type=F32)


def _sigmoid(x):
    return 0.5 * jnp.tanh(0.5 * x) + 0.5


def _colsum(a):
    return jnp.sum(a, axis=0, keepdims=True)


def _cmul(xr, xi, kr, ki):
    return xr * kr - xi * ki, xr * ki + xi * kr


def _special_bins(t):
    d, s = SQH * (t[1] - t[3]), SQH * (t[1] + t[3])
    return (t[0] + d, t[2] + s), (t[0] - d, s - t[2])


def _filter_feat_kernel(z_ref, w1_ref, b1_ref, w2_ref, b2_ref, w3_ref, b3_ref, fr_ref, o_ref):
    hp = lax.Precision.HIGHEST
    fr = fr_ref[...]
    h = jnp.sin(fr * (jnp.dot(z_ref[...], w1_ref[...], precision=hp, preferred_element_type=F32) + b1_ref[...]))
    h = jnp.sin(fr * (jnp.dot(h, w2_ref[...], precision=hp, preferred_element_type=F32) + b2_ref[...]))
    h = jnp.sin(fr * (jnp.dot(h, w3_ref[...], precision=hp, preferred_element_type=F32) + b3_ref[...]))
    o_ref[...] = h


def _filter_spec_kernel(f_ref, w4f_ref, w4b_ref, t_ref, ad_ref, bias_ref, alt_ref, tab_ref, wsrc_ref,
                        sp_ref, kq_ref, wdst_ref):
    wdst_ref[...] = wsrc_ref[...].astype(wdst_ref.dtype)
    Q = f_ref.shape[1]
    inv_n = 1.0 / (2 * RADIX * Q)
    ad = ad_ref[...]
    row = lax.broadcasted_iota(jnp.int32, (Q, ad.shape[1]), 0)

    def split(x):
        hi = x.astype(BF16)
        return hi, (x - hi.astype(F32)).astype(BF16)

    def dot_split(a, b):
        return ((_dot(a[1], b[1]) + _dot(a[1], b[0])) + _dot(a[0], b[1])) + _dot(a[0], b[0])

    w4f, w4b = split(w4f_ref[...]), split(w4b_ref[...])
    hf, hb = [], []
    for r in range(RADIX):
        f = split(f_ref[r][:, 0:w4f_ref.shape[0]])
        decay = jnp.exp(-t_ref[r] * ad) + MOD_SHIFT
        hf.append(dot_split(f, w4f) * decay)
        hb.append(dot_split(f, w4b) * decay)
    hb[0] = jnp.where(row == 0, 0.0, hb[0])
    norm = sum(_colsum(jnp.abs(h)) for h in hf + hb)
    inv = 1.0 / norm
    ev = [(hf[r] + hb[r]) * inv for r in range(RADIX)]
    od = [(hf[r] - hb[r]) * inv for r in range(RADIX)]
    evb = [e.astype(BF16) for e in ev]
    odb = [o.astype(BF16) for o in od]

    def cos_t(r, zb):
        return _dot(tab_ref[r], zb[r])

    def sin_t(r, zb):
        return _dot(tab_ref[RADIX + r], zb[r])

    a0, a2 = cos_t(0, evb), cos_t(2, evb)
    e, f_ = a0 + a2, a0 - a2
    g = cos_t(1, evb) + cos_t(3, evb)
    v = sin_t(1, evb) - sin_t(3, evb)
    b0, b2 = sin_t(0, odb), sin_t(2, odb)
    p, q = b0 + b2, b0 - b2
    u = sin_t(1, odb) + sin_t(3, odb)
    h = cos_t(1, odb) - cos_t(3, odb)
    first = row[:, :1] == 0
    w_end = jnp.where(first, inv_n, 2.0 * inv_n)
    w_dup = jnp.where(first, 0.0, 2.0 * inv_n)
    w_mid = 2.0 * inv_n
    bias = bias_ref[...]
    for s, (kr, ki, w) in enumerate([(e + g, p + u, w_end), (e - g, u - p, w_end),
                                     (f_ + v, h - q, w_mid), (f_ - v, q + h, w_dup)]):
        sp_ref[s] = (kr + bias) * w
        sp_ref[RADIX + s] = ki * w
    alt = alt_ref[...]
    (krq, _), (kr3, _) = _special_bins([_colsum(x * alt) for x in ev])
    (_, kiq), (_, ki3) = _special_bins([_colsum(x * alt) for x in od])
    for i, kval in enumerate((krq + bias, kiq, kr3 + bias, ki3)):
        kq_ref[i:i + 1, :] = kval * w_mid


def _hyena_filters(zfeat, t_col, absdelta, alt, tabs, w1, b1, w2, b2, w3, b3, freq, w4, hbias, tc, w_cast):
    L = zfeat.shape[0]
    Q = L // RADIX
    c2 = w4.shape[1] // N_DIR
    n_ch = c2 // HYENA_ORDER

    def pad2(a, r, c):
        return jnp.pad(a.astype(F32), ((0, r - a.shape[0]), (0, c - a.shape[1])))

    feat = pl.pallas_call(
        _filter_feat_kernel,
        out_shape=jax.ShapeDtypeStruct((L, LANES), F32),
        name="filter_feat",
        compiler_params=_params(32, 0),
    )(pad2(zfeat, L, LANES),
      pad2(w1, LANES, LANES), pad2(b1[None], 1, LANES),
      pad2(w2, LANES, LANES), pad2(b2[None], 1, LANES),
      pad2(w3, LANES, LANES), pad2(b3[None], 1, LANES), pad2(freq[None], 1, LANES))

    w4p = w4.astype(F32)
    fo = w4.shape[0]
    nt = c2 // tc
    nct = n_ch // tc
    slab = w_cast.shape[0] // nt
    assert slab * nt == w_cast.shape[0] and slab % BF16_ROWS == 0
    cast_spec = pl.BlockSpec((slab, w_cast.shape[1]), lambda j: (j, 0))
    sp, kq, w_bf16 = pl.pallas_call(
        _filter_spec_kernel,
        grid=(nt,),
        in_specs=[
            _resident((RADIX, Q, LANES), lambda j: (0, 0, 0)),
            pl.BlockSpec((fo, tc), lambda j: (0, j)),
            pl.BlockSpec((fo, tc), lambda j: (0, nt + j)),
            _resident((RADIX, Q, 1), lambda j: (0, 0, 0)),
            pl.BlockSpec((1, tc), lambda j: (0, j % nct)),
            pl.BlockSpec((1, tc), lambda j: (0, j)),
            _resident((Q, 1), lambda j: (0, 0)),
            _resident((2 * RADIX, Q, Q), lambda j: (0, 0, 0)),
            cast_spec,
        ],
        out_specs=[pl.BlockSpec((2 * RADIX, Q, tc), lambda j: (0, 0, j)),
                   pl.BlockSpec((RADIX, tc), lambda j: (0, j)), cast_spec],
        out_shape=[jax.ShapeDtypeStruct((2 * RADIX, Q, c2), F32),
                   jax.ShapeDtypeStruct((RADIX, c2), F32),
                   jax.ShapeDtypeStruct(w_cast.shape, BF16)],
        name="filter_spec",
        compiler_params=_params(58, 1),
    )(feat.reshape(RADIX, Q, LANES), w4p, w4p, t_col.reshape(RADIX, Q, 1), absdelta,
      hbias.reshape(1, c2).astype(F32), alt, tabs, w_cast)
    return (sp, kq), w_bf16


def _proj_kernel(x_ref, g_ref, w_ref, o_ref, h_ref, *, n_mix):
    j = pl.program_id(1)
    tn = o_ref.shape[1]
    has_gates = (j + 1) * tn > n_mix

    def with_gates(acc):
        col = j * tn + lax.broadcasted_iota(jnp.int32, acc.shape, 1)
        return jnp.where(col >= n_mix, _sigmoid(acc), acc)

    @pl.when(j == 0)
    def _():
        h = _rms(x_ref[...], g_ref[...]).astype(BF16)
        h_ref[...] = h
        acc = _dot(h, w_ref[...])
        o_ref[...] = (with_gates(acc) if tn > n_mix else acc).astype(BF16)

    @pl.when((j > 0) & jnp.logical_not(has_gates))
    def _():
        o_ref[...] = _dot(h_ref[...], w_ref[...]).astype(BF16)

    @pl.when((j > 0) & has_gates)
    def _():
        o_ref[...] = with_gates(_dot(h_ref[...], w_ref[...])).astype(BF16)


def _proj(x2, g, w, n_mix, tm, tn):
    M, D = x2.shape
    E = w.shape[1]
    return pl.pallas_call(
        functools.partial(_proj_kernel, n_mix=n_mix),
        grid=(M // tm, E // tn),
        in_specs=[
            pl.BlockSpec((tm, D), lambda i, j: (i, 0)),
            _resident((1, D), lambda i, j: (0, 0)),
            pl.BlockSpec((D, tn), lambda i, j: (0, j)),
        ],
        out_specs=pl.BlockSpec((tm, tn), lambda i, j: (i, j)),
        out_shape=jax.ShapeDtypeStruct((M, E), BF16),
        scratch_shapes=[pltpu.VMEM((tm, D), BF16)],
        name="proj",
        compiler_params=_params(56, 2),
    )(x2, g, w)


def _short_conv_rows(p_ref, il_ref, w_ref, b_ref, r0, rc):
    L = p_ref.shape[1]
    t0, t1 = RADIX * r0, RADIX * (r0 + rc)
    parts = [[] for _ in range(RADIX)]
    for c in range(il_ref.shape[0]):
        il_ref[c] = p_ref[0, t0:t1, c * LANES:(c + 1) * LANES].astype(F32)
        for r in range(RADIX):
            parts[r].append(il_ref[c, pl.ds(r, rc, stride=RADIX), :])
    p = [jnp.concatenate(x, axis=1) for x in parts]
    row = lax.broadcasted_iota(jnp.int32, p[0].shape, 0)
    if t0 == 0:
        before = 0.0
    else:
        before = p_ref[0, t0 - BF16_ROWS:t0, :].astype(F32)[BF16_ROWS - 1:BF16_ROWS, :]
    if t1 == L:
        after = 0.0
    else:
        after = p_ref[0, t1:t1 + BF16_ROWS, :].astype(F32)[0:1, :]
    last_prev = jnp.where(row == 0, before, pltpu.roll(p[RADIX - 1], 1, axis=0))
    first_next = jnp.where(row == rc - 1, after, pltpu.roll(p[0], rc - 1, axis=0))
    prev = [last_prev] + p[:-1]
    nxt = p[1:] + [first_next]
    w0, w1, w2, b = w_ref[0:1, :], w_ref[1:2, :], w_ref[2:3, :], b_ref[...]
    return [prev[r] * w0 + p[r] * w1 + nxt[r] * w2 + b for r in range(RADIX)]


def _hyena_kernel(v_ref, x1_ref, x2_ref, wv_ref, w1_ref, w2_ref, bv_ref, b1_ref, b2_ref,
                  sp0_ref, kq0_ref, sp1_ref, kq1_ref, alt_ref, tab_ref, *refs, rc, n_cast):
    o_ref = refs[n_cast]
    zb_ref, p_ref, il_ref = refs[2 * n_cast + 1:]
    for src, dst in zip(refs[:n_cast], refs[n_cast + 1:2 * n_cast + 1]):
        dst[...] = src[...].astype(dst.dtype)
    Q = tab_ref.shape[1]
    R = RADIX
    chunks = [(r0, slice(r0, r0 + rc)) for r0 in range(0, Q, rc)]

    def put_z(zs, rows):
        alt = alt_ref[rows, :]
        for r in range(R):
            zb_ref[r, rows, :] = zs[r].astype(BF16)
        return [_colsum(z * alt) for z in zs]

    def spectrum_product(sp_ref):
        zb = [zb_ref[r] for r in range(R)]
        for _, rows in chunks:
            a = [_dot(tab_ref[r, rows, :], zb[r]) for r in range(R)]
            b = [_dot(tab_ref[R + r, rows, :], zb[r]) for r in range(R)]
            e, f = a[0] + a[2], a[0] - a[2]
            g, h = a[1] + a[3], a[1] - a[3]
            p, q = b[0] + b[2], b[0] - b[2]
            u, v = b[1] + b[3], b[1] - b[3]
            xr = [e + g, e - g, f + v, f - v]
            xi = [p + u, u - p, h - q, q + h]
            pr, pi = zip(*[_cmul(xr[s], xi[s], sp_ref[s, rows, :], sp_ref[R + s, rows, :])
                           for s in range(R)])
            s01p, s01m, s23p, s23m = pr[0] + pr[1], pr[0] - pr[1], pr[2] + pr[3], pr[2] - pr[3]
            t01p, t01m, t23p, t23m = pi[0] + pi[1], pi[0] - pi[1], pi[2] + pi[3], pi[2] - pi[3]
            for r, val in enumerate([s01p + s23p, s01m + t23p, s01p - s23p, s01m - t23p,
                                     t01m - t23m, t01p + s23m, t01m + t23m, t01p - s23m]):
                p_ref[r, rows, :] = val.astype(BF16)

    def conv_rows(rows, t, kq_ref):
        (xrq, xiq), (xr3, xi3) = _special_bins(t)
        prq, piq = _cmul(xrq, xiq, kq_ref[0:1, :], kq_ref[1:2, :])
        pr3, pi3 = _cmul(xr3, xi3, kq_ref[2:3, :], kq_ref[3:4, :])
        sp = [prq + pr3, SQH * (prq + piq + pi3 - pr3), piq - pi3, SQH * (piq - prq + pr3 + pi3)]
        alt = alt_ref[rows, :]
        return [_dot(tab_ref[2 * R + r, rows, :], p_ref[r]) + _dot(tab_ref[3 * R + r, rows, :], p_ref[R + r])
                + alt * sp[r] for r in range(R)]

    def add(acc, new):
        return new if acc is None else [x + y for x, y in zip(acc, new)]

    t = None
    for r0, rows in chunks:
        t = add(t, put_z(_short_conv_rows(v_ref, il_ref, wv_ref, bv_ref, r0, rc), rows))
    spectrum_product(sp0_ref)
    t1 = None
    for r0, rows in chunks:
        y = conv_rows(rows, t, kq0_ref)
        gate = _short_conv_rows(x1_ref, il_ref, w1_ref, b1_ref, r0, rc)
        t1 = add(t1, put_z([gate[r] * y[r] for r in range(R)], rows))
    spectrum_product(sp1_ref)
    for r0, rows in chunks:
        y = conv_rows(rows, t1, kq1_ref)
        gate = _short_conv_rows(x2_ref, il_ref, w2_ref, b2_ref, r0, rc)
        out = [gate[r] * y[r] for r in range(R)]
        for c in range(il_ref.shape[0]):
            lanes = slice(c * LANES, (c + 1) * LANES)
            for r in range(R):
                il_ref[c, pl.ds(r, rc, stride=R), :] = out[r][:, lanes]
            o_ref[0, R * r0:R * (r0 + rc), lanes] = il_ref[c].astype(o_ref.dtype)


def _hyena(p3, conv_w, conv_b, spectra, alt, tabs, n_ch, ct, to_cast):
    B, L, _ = p3.shape
    Q = L // RADIX
    nct = n_ch // ct
    rc = _pick(Q, 512)
    sp, kq = spectra
    cb = conv_b.reshape(1, -1).astype(F32)
    cw = conv_w.astype(F32)

    def pspec(g):
        return pl.BlockSpec((1, L, ct), lambda j, b, g=g: (b, 0, g * nct + j))

    def rowspec(rows, g):
        return pl.BlockSpec((rows, ct), lambda j, b, g=g: (0, g * nct + j))

    def specspec(g):
        return pl.BlockSpec((2 * RADIX, Q, ct), lambda j, b, g=g: (0, 0, g * nct + j))

    steps = nct * B
    cast_specs = []
    for w in to_cast:
        slab = w.shape[0] // steps
        assert slab * steps == w.shape[0] and slab % BF16_ROWS == 0, w.shape
        cast_specs.append(pl.BlockSpec((slab, w.shape[1]), lambda j, b: (j * B + b, 0)))

    return pl.pallas_call(
        functools.partial(_hyena_kernel, rc=rc, n_cast=len(to_cast)),
        grid=(nct, B),
        in_specs=[
            pspec(0), pspec(1), pspec(2),
            rowspec(3, 0), rowspec(3, 1), rowspec(3, 2),
            rowspec(1, 0), rowspec(1, 1), rowspec(1, 2),
            specspec(0), rowspec(RADIX, 0), specspec(1), rowspec(RADIX, 1),
            _resident((Q, 1), lambda j, b: (0, 0)),
            _resident(tabs.shape, lambda j, b: (0, 0, 0)),
        ] + cast_specs,
        out_specs=[pl.BlockSpec((1, L, ct), lambda j, b: (b, 0, j))] + cast_specs,
        out_shape=[jax.ShapeDtypeStruct((B, L, n_ch), BF16)]
                  + [jax.ShapeDtypeStruct(w.shape, BF16) for w in to_cast],
        scratch_shapes=[pltpu.VMEM((RADIX, Q, ct), BF16),
                        pltpu.VMEM((2 * RADIX, Q, ct), BF16),
                        pltpu.VMEM((ct // LANES, RADIX * rc, LANES), F32)],
        name="hyena",
        compiler_params=_params(60, 2),
    )(p3, p3, p3, cw, cw, cw, cb, cb, cb, sp, kq, sp, kq, alt, tabs, *to_cast)


def _gelu(x):
    return 0.5 * x * (1.0 + lax.erf(x * math.sqrt(0.5)))


def _spatial_gating(u_ref, v_ref, g_ref, b_ref, ws_ref, bs_ref, o_ref, heads, chunk):
    u = _gelu(u_ref[...].astype(F32))
    v = _gelu(v_ref[...].astype(F32))
    mu = jnp.mean(v, axis=-1, keepdims=True)
    vc = v - mu
    var = jnp.mean(vc * vc, axis=-1, keepdims=True)
    vn = (vc * lax.rsqrt(var + EPS) * g_ref[...] + b_ref[...]).astype(BF16)
    tm, ds = vn.shape
    hd = ds // heads
    for n in range(tm // chunk):
        rows = slice(n * chunk, (n + 1) * chunk)
        for h in range(heads):
            cols = slice(h * hd, (h + 1) * hd)
            s = _dot(ws_ref[h], vn[rows, cols]) + bs_ref[:, h:h + 1]
            o_ref[rows, cols] = (u[rows, cols] * s).astype(o_ref.dtype)


def _merge_kernel(x_ref, yh_ref, u_ref, v_ref, gh0_ref, gh1_ref, gs0_ref, gs1_ref,
                  lng_ref, lnb_ref, ws_ref, bs_ref, wbh_ref, wbs_ref, wo_ref, o_ref, ys_ref, *, heads, chunk):
    gh = jnp.concatenate([gh0_ref[...], gh1_ref[...]], axis=1).astype(F32)
    ah = _dot(yh_ref[...], wbh_ref[...]) * gh
    _spatial_gating(u_ref, v_ref, lng_ref, lnb_ref, ws_ref, bs_ref, ys_ref, heads, chunk)
    gs = jnp.concatenate([gs0_ref[...], gs1_ref[...]], axis=1).astype(F32)
    a = ah + _dot(ys_ref[...], wbs_ref[...]) * gs
    o_ref[...] = x_ref[...] + _dot(a.astype(BF16), wo_ref[...])


def _merge(x2, yh, proj, sgu_col, gate_col, ln_g, ln_b, w_s, b_s, wbh, wbs, wo, tm):
    M, D = x2.shape
    ds = wbs.shape[0]
    heads, chunk, _ = w_s.shape
    gw = D // 2
    gb = gate_col // gw
    sb = sgu_col // ds
    assert gate_col % gw == 0 and sgu_col % ds == 0 and tm % chunk == 0
    full = lambda i: (0, 0)

    def gate(c):
        return pl.BlockSpec((tm, gw), lambda i, c=c: (i, gb + c))

    return pl.pallas_call(
        functools.partial(_merge_kernel, heads=heads, chunk=chunk),
        grid=(M // tm,),
        in_specs=[
            pl.BlockSpec((tm, D), lambda i: (i, 0)),
            pl.BlockSpec((tm, yh.shape[1]), lambda i: (i, 0)),
            pl.BlockSpec((tm, ds), lambda i: (i, sb)),
            pl.BlockSpec((tm, ds), lambda i: (i, sb + 1)),
            gate(0), gate(1), gate(2), gate(3),
            _resident((1, ds), full),
            _resident((1, ds), full),
            _resident((heads, chunk, chunk), lambda i: (0, 0, 0)),
            _resident((chunk, heads), full),
            _resident(wbh.shape, full),
            _resident(wbs.shape, full),
            _resident(wo.shape, full),
        ],
        out_specs=pl.BlockSpec((tm, D), lambda i: (i, 0)),
        out_shape=jax.ShapeDtypeStruct((M, D), F32),
        scratch_shapes=[pltpu.VMEM((tm, ds), BF16)],
        name="merge",
        compiler_params=_params(60, 1),
    )(x2, yh, proj, proj, proj, proj, proj, proj,
      ln_g.reshape(1, ds).astype(F32), ln_b.reshape(1, ds).astype(F32),
      w_s.astype(BF16), b_s.T.astype(F32), wbh, wbs, wo)


def _ffn_kernel(x_hbm, g_ref, wg_ref, wu_ref, wo_ref, gf_ref, o_ref, h_ref, x_ref, x_sem, *, nf, final_norm):
    i = pl.program_id(0)
    f = pl.program_id(1)
    tm = x_ref.shape[0]

    def x_copy(tile_index):
        rows = pl.ds(pl.multiple_of(tile_index * tm, tm), tm)
        return pltpu.make_async_copy(x_hbm.at[rows, :], x_ref, x_sem)

    @pl.when((f == 0) & (i == 0))
    def _():
        x_copy(0).start()

    @pl.when((f == 1) & (i + 1 < pl.num_programs(0)))
    def _():
        x_copy(i + 1).start()

    def tile(h):
        gate = _dot(h, wg_ref[...])
        up = _dot(h, wu_ref[...])
        a = (gate * _sigmoid(gate) * up).astype(BF16)
        return _dot(a, wo_ref[...])

    @pl.when(f == 0)
    def _():
        x_copy(i).wait()
        x = x_ref[...]
        h = _rms(x, g_ref[...]).astype(BF16)
        h_ref[...] = h
        o_ref[...] = x + tile(h)

    last = nf - 1
    fuse_norm = final_norm and nf > 1

    @pl.when((f > 0) & (f < last) if fuse_norm else f > 0)
    def _():
        o_ref[...] += tile(h_ref[...])

    if fuse_norm:
        @pl.when(f == last)
        def _():
            o_ref[...] = _rms(o_ref[...] + tile(h_ref[...]), gf_ref[...])
    elif final_norm:
        @pl.when(f == last)
        def _():
            o_ref[...] = _rms(o_ref[...], gf_ref[...])


def _ffn(x2, g, w_in, w_out, g_final, final_norm, tm, tf):
    M, D = x2.shape
    FF = w_out.shape[0]
    nf = FF // tf
    assert nf >= 2
    full = lambda i, f: (0, 0)
    return pl.pallas_call(
        functools.partial(_ffn_kernel, nf=nf, final_norm=final_norm),
        grid=(M // tm, nf),
        in_specs=[
            pl.BlockSpec(memory_space=pl.ANY),
            _resident((1, D), full),
            pl.BlockSpec((D, tf), lambda i, f: (0, f)),
            pl.BlockSpec((D, tf), lambda i, f: (0, nf + f)),
            pl.BlockSpec((tf, D), lambda i, f: (f, 0)),
            _resident((1, D), full),
        ],
        out_specs=pl.BlockSpec((tm, D), lambda i, f: (i, 0)),
        out_shape=jax.ShapeDtypeStruct((M, D), F32),
        scratch_shapes=[pltpu.VMEM((tm, D), BF16), pltpu.VMEM((tm, D), F32), pltpu.SemaphoreType.DMA(())],
        name="ffn",
        compiler_params=_params(60, 2),
    )(x2, g, w_in, w_in, w_out, g_final)


def _transform_tables(L):
    Q = L // RADIX
    k = np.arange(Q, dtype=np.int64)[:, None]
    m = np.arange(Q, dtype=np.int64)[None, :]

    def tab(fn, r):
        return fn(((k * (RADIX * m + r)) % (2 * L)) * (np.pi / L)).astype(np.float32)

    cs = [tab(np.cos, r) for r in range(RADIX)] + [tab(np.sin, r) for r in range(RADIX)]
    stack = np.stack(cs + [t.T for t in cs], axis=0)
    alt = (1.0 - 2.0 * (np.arange(Q) % 2)).astype(np.float32)[:, None]
    return jnp.asarray(stack).astype(BF16), jnp.asarray(alt)


def _position_features(L, emb):
    Q = L // RADIX
    row = np.arange(L)
    n = ((row % Q) * RADIX + row // Q).astype(np.float64)[:, None]
    t = n / (L - 1)
    bands = (emb - 1) // 2
    w = 2.0 * np.pi * n / L
    f = np.linspace(1e-4, bands - 1, bands)[None, :]
    z = np.concatenate([t, np.cos(f * w), -np.sin(f * w)], axis=-1)
    return jnp.asarray(z.astype(np.float32)), jnp.asarray(t.astype(np.float32))


def _pick(n, pref):
    t = min(n, pref)
    while n % t:
        t -= 1
    return t


def kernel(x, norm_mix_g, w_in, short_conv_w, short_conv_b, filt_w1, filt_b1, filt_w2, filt_b2, filt_w3, filt_b3, filt_freq, filt_w4, hyena_bias, sgu_ln_g, sgu_ln_b, sgu_w_s, sgu_b_s, w_branch_hyena, w_branch_sgu, w_out, norm_ffn_g, w_ffn_in, w_ffn_out, norm_final_g):
    B, L, D = x.shape
    depth = w_in.shape[0]
    n_ch = w_branch_hyena.shape[1]
    ds = w_branch_sgu.shape[1]
    assert n_ch == ds and 2 * n_ch == D and L % RADIX == 0 and depth >= 1
    n_mix = 3 * n_ch + 2 * ds
    M = B * L

    tabs, alt = _transform_tables(L)
    zfeat, t_col = _position_features(L, filt_w1.shape[1])
    max_decay = math.log(DECAY_TARGET) / FAST_DECAY
    min_decay = math.log(DECAY_TARGET) / SLOW_DECAY
    absdelta = jnp.asarray(np.abs(np.linspace(min_decay, max_decay, n_ch)).astype(np.float32)[None, :])

    ct = _pick(n_ch, 256)
    tm = _pick(M, 1024)
    tn = LANES * _pick(w_in.shape[2] // LANES, 2304 // LANES)

    x2 = x.reshape(M, D)
    for l in range(depth):
        spectra, w_in_bf16 = _hyena_filters(zfeat, t_col, absdelta, alt, tabs,
                                            filt_w1[l], filt_b1[l], filt_w2[l], filt_b2[l],
                                            filt_w3[l], filt_b3[l], filt_freq[l], filt_w4[l],
                                            hyena_bias[l], ct, w_in[l])
        proj = _proj(x2, norm_mix_g[l].reshape(1, D), w_in_bf16, n_mix, tm, tn)
        y_hy, wbh, wbs, wo, wfi, wfo = _hyena(
            proj.reshape(B, L, proj.shape[1]), short_conv_w[l], short_conv_b[l], spectra,
            alt, tabs, n_ch, ct,
            [w_branch_hyena[l], w_branch_sgu[l], w_out[l], w_ffn_in[l], w_ffn_out[l]])
        x2 = _merge(x2, y_hy.reshape(M, n_ch), proj, 3 * n_ch, n_mix,
                    sgu_ln_g[l], sgu_ln_b[l], sgu_w_s[l], sgu_b_s[l], wbh, wbs, wo, _pick(M, 512))
        last = l == depth - 1
        x2 = _ffn(x2, norm_ffn_g[l].reshape(1, D), wfi, wfo, norm_final_g.reshape(1, D), last,
                  _pick(M, 1024), _pick(w_ffn_out.shape[1], 512))
    return x2.reshape(B, L, D)
```

```python
import functools
import math

import numpy as np
import jax
import jax.numpy as jnp
from jax import lax
from jax.experimental import pallas as pl
from jax.experimental.pallas import tpu as pltpu

F32 = jnp.float32
BF16 = jnp.bfloat16

EPS = 1e-6
FAST_DECAY = 0.3
SLOW_DECAY = 1.5
DECAY_TARGET = 1e-2
MOD_SHIFT = 0.0
N_DIR = 2
HYENA_ORDER = 2
RADIX = 4
SQH = math.sqrt(0.5)
LANES = 128
BF16_ROWS = 16
MIB = 1024 * 1024


def _params(vmem_mib, n_grid):
    return pltpu.CompilerParams(
        dimension_semantics=("arbitrary",) * n_grid,
        vmem_limit_bytes=vmem_mib * MIB)


def _resident(block_shape, index_map):
    return pl.BlockSpec(block_shape, index_map, pipeline_mode=pl.Buffered(1))


def _rms(x, g):
    ms = jnp.mean(x * x, axis=-1, keepdims=True)
    return x * lax.rsqrt(ms + EPS) * g


def _dot(a, b):
    return jnp.dot(a, b, preferred_element_type=F32)


def _sigmoid(x):
    return 0.5 * jnp.tanh(0.5 * x) + 0.5


def _colsum(a):
    return jnp.sum(a, axis=0, keepdims=True)


def _cmul(xr, xi, kr, ki):
    return xr * kr - xi * ki, xr * ki + xi * kr


def _special_bins(t):
    d, s = SQH * (t[1] - t[3]), SQH * (t[1] + t[3])
    return (t[0] + d, t[2] + s), (t[0] - d, s - t[2])


def _filter_feat_kernel(z_ref, w1_ref, b1_ref, w2_ref, b2_ref, w3_ref, b3_ref, fr_ref, o_ref):
    hp = lax.Precision.HIGHEST
    fr = fr_ref[...]
    h = jnp.sin(fr * (jnp.dot(z_ref[...], w1_ref[...], precision=hp, preferred_element_type=F32) + b1_ref[...]))
    h = jnp.sin(fr * (jnp.dot(h, w2_ref[...], precision=hp, preferred_element_type=F32) + b2_ref[...]))
    h = jnp.sin(fr * (jnp.dot(h, w3_ref[...], precision=hp, preferred_element_type=F32) + b3_ref[...]))
    o_ref[...] = h


def _filter_spec_kernel(f_ref, w4f_ref, w4b_ref, t_ref, ad_ref, bias_ref, alt_ref, tab_ref, wsrc_ref,
                        sp_ref, kq_ref, wdst_ref):
    wdst_ref[...] = wsrc_ref[...].astype(wdst_ref.dtype)
    Q = f_ref.shape[1]
    inv_n = 1.0 / (2 * RADIX * Q)
    ad = ad_ref[...]
    row = lax.broadcasted_iota(jnp.int32, (Q, ad.shape[1]), 0)

    def split(x):
        hi = x.astype(BF16)
        return hi, (x - hi.astype(F32)).astype(BF16)

    def dot_split(a, b):
        return ((_dot(a[1], b[1]) + _dot(a[1], b[0])) + _dot(a[0], b[1])) + _dot(a[0], b[0])

    w4f, w4b = split(w4f_ref[...]), split(w4b_ref[...])
    hf, hb = [], []
    for r in range(RADIX):
        f = split(f_ref[r][:, 0:w4f_ref.shape[0]])
        decay = jnp.exp(-t_ref[r] * ad) + MOD_SHIFT
        hf.append(dot_split(f, w4f) * decay)
        hb.append(dot_split(f, w4b) * decay)
    hb[0] = jnp.where(row == 0, 0.0, hb[0])
    norm = sum(_colsum(jnp.abs(h)) for h in hf + hb)
    inv = 1.0 / norm
    ev = [(hf[r] + hb[r]) * inv for r in range(RADIX)]
    od = [(hf[r] - hb[r]) * inv for r in range(RADIX)]
    evb = [e.astype(BF16) for e in ev]
    odb = [o.astype(BF16) for o in od]

    def cos_t(r, zb):
        return _dot(tab_ref[r], zb[r])

    def sin_t(r, zb):
        return _dot(tab_ref[RADIX + r], zb[r])

    a0, a2 = cos_t(0, evb), cos_t(2, evb)
    e, f_ = a0 + a2, a0 - a2
    g = cos_t(1, evb) + cos_t(3, evb)
    v = sin_t(1, evb) - sin_t(3, evb)
    b0, b2 = sin_t(0, odb), sin_t(2, odb)
    p, q = b0 + b2, b0 - b2
    u = sin_t(1, odb) + sin_t(3, odb)
    h = cos_t(1, odb) - cos_t(3, odb)
    first = row[:, :1] == 0
    w_end = jnp.where(first, inv_n, 2.0 * inv_n)
    w_dup = jnp.where(first, 0.0, 2.0 * inv_n)
    w_mid = 2.0 * inv_n
    bias = bias_ref[...]
    for s, (kr, ki, w) in enumerate([(e + g, p + u, w_end), (e - g, u - p, w_end),
                                     (f_ + v, h - q, w_mid), (f_ - v, q + h, w_dup)]):
        sp_ref[s] = (kr + bias) * w
        sp_ref[RADIX + s] = ki * w
    alt = alt_ref[...]
    (krq, _), (kr3, _) = _special_bins([_colsum(x * alt) for x in ev])
    (_, kiq), (_, ki3) = _special_bins([_colsum(x * alt) for x in od])
    for i, kval in enumerate((krq + bias, kiq, kr3 + bias, ki3)):
        kq_ref[i:i + 1, :] = kval * w_mid


def _hyena_filters(zfeat, t_col, absdelta, alt, tabs, w1, b1, w2, b2, w3, b3, freq, w4, hbias, tc, w_cast):
    L = zfeat.shape[0]
    Q = L // RADIX
    c2 = w4.shape[1] // N_DIR
    n_ch = c2 // HYENA_ORDER

    def pad2(a, r, c):
        return jnp.pad(a.astype(F32), ((0, r - a.shape[0]), (0, c - a.shape[1])))

    feat = pl.pallas_call(
        _filter_feat_kernel,
        out_shape=jax.ShapeDtypeStruct((L, LANES), F32),
        name="filter_feat",
        compiler_params=_params(32, 0),
    )(pad2(zfeat, L, LANES),
      pad2(w1, LANES, LANES), pad2(b1[None], 1, LANES),
      pad2(w2, LANES, LANES), pad2(b2[None], 1, LANES),
      pad2(w3, LANES, LANES), pad2(b3[None], 1, LANES), pad2(freq[None], 1, LANES))

    w4p = w4.astype(F32)
    fo = w4.shape[0]
    nt = c2 // tc
    nct = n_ch // tc
    slab = w_cast.shape[0] // nt
    assert slab * nt == w_cast.shape[0] and slab % BF16_ROWS == 0
    cast_spec = pl.BlockSpec((slab, w_cast.shape[1]), lambda j: (j, 0))
    sp, kq, w_bf16 = pl.pallas_call(
        _filter_spec_kernel,
        grid=(nt,),
        in_specs=[
            _resident((RADIX, Q, LANES), lambda j: (0, 0, 0)),
            pl.BlockSpec((fo, tc), lambda j: (0, j)),
            pl.BlockSpec((fo, tc), lambda j: (0, nt + j)),
            _resident((RADIX, Q, 1), lambda j: (0, 0, 0)),
            pl.BlockSpec((1, tc), lambda j: (0, j % nct)),
            pl.BlockSpec((1, tc), lambda j: (0, j)),
            _resident((Q, 1), lambda j: (0, 0)),
            _resident((2 * RADIX, Q, Q), lambda j: (0, 0, 0)),
            cast_spec,
        ],
        out_specs=[pl.BlockSpec((2 * RADIX, Q, tc), lambda j: (0, 0, j)),
                   pl.BlockSpec((RADIX, tc), lambda j: (0, j)), cast_spec],
        out_shape=[jax.ShapeDtypeStruct((2 * RADIX, Q, c2), F32),
                   jax.ShapeDtypeStruct((RADIX, c2), F32),
                   jax.ShapeDtypeStruct(w_cast.shape, BF16)],
        name="filter_spec",
        compiler_params=_params(58, 1),
    )(feat.reshape(RADIX, Q, LANES), w4p, w4p, t_col.reshape(RADIX, Q, 1), absdelta,
      hbias.reshape(1, c2).astype(F32), alt, tabs, w_cast)
    return (sp, kq), w_bf16


def _proj_kernel(x_ref, g_ref, w_ref, o_ref, h_ref, *, n_mix):
    j = pl.program_id(1)
    tn = o_ref.shape[1]
    has_gates = (j + 1) * tn > n_mix

    def with_gates(acc):
        col = j * tn + lax.broadcasted_iota(jnp.int32, acc.shape, 1)
        return jnp.where(col >= n_mix, _sigmoid(acc), acc)

    @pl.when(j == 0)
    def _():
        h = _rms(x_ref[...], g_ref[...]).astype(BF16)
        h_ref[...] = h
        acc = _dot(h, w_ref[...])
        o_ref[...] = (with_gates(acc) if tn > n_mix else acc).astype(BF16)

    @pl.when((j > 0) & jnp.logical_not(has_gates))
    def _():
        o_ref[...] = _dot(h_ref[...], w_ref[...]).astype(BF16)

    @pl.when((j > 0) & has_gates)
    def _():
        o_ref[...] = with_gates(_dot(h_ref[...], w_ref[...])).astype(BF16)


def _proj(x2, g, w, n_mix, tm, tn):
    M, D = x2.shape
    E = w.shape[1]
    return pl.pallas_call(
        functools.partial(_proj_kernel, n_mix=n_mix),
        grid=(M // tm, E // tn),
        in_specs=[
            pl.BlockSpec((tm, D), lambda i, j: (i, 0)),
            _resident((1, D), lambda i, j: (0, 0)),
            pl.BlockSpec((D, tn), lambda i, j: (0, j)),
        ],
        out_specs=pl.BlockSpec((tm, tn), lambda i, j: (i, j)),
        out_shape=jax.ShapeDtypeStruct((M, E), BF16),
        scratch_shapes=[pltpu.VMEM((tm, D), BF16)],
        name="proj",
        compiler_params=_params(56, 2),
    )(x2, g, w)


def _short_conv_rows(p_ref, il_ref, w_ref, b_ref, r0, rc):
    L = p_ref.shape[1]
    t0, t1 = RADIX * r0, RADIX * (r0 + rc)
    parts = [[] for _ in range(RADIX)]
    for c in range(il_ref.shape[0]):
        il_ref[c] = p_ref[0, t0:t1, c * LANES:(c + 1) * LANES].astype(F32)
        for r in range(RADIX):
            parts[r].append(il_ref[c, pl.ds(r, rc, stride=RADIX), :])
    p = [jnp.concatenate(x, axis=1) for x in parts]
    row = lax.broadcasted_iota(jnp.int32, p[0].shape, 0)
    if t0 == 0:
        before = 0.0
    else:
        before = p_ref[0, t0 - BF16_ROWS:t0, :].astype(F32)[BF16_ROWS - 1:BF16_ROWS, :]
    if t1 == L:
        after = 0.0
    else:
        after = p_ref[0, t1:t1 + BF16_ROWS, :].astype(F32)[0:1, :]
    last_prev = jnp.where(row == 0, before, pltpu.roll(p[RADIX - 1], 1, axis=0))
    first_next = jnp.where(row == rc - 1, after, pltpu.roll(p[0], rc - 1, axis=0))
    prev = [last_prev] + p[:-1]
    nxt = p[1:] + [first_next]
    w0, w1, w2, b = w_ref[0:1, :], w_ref[1:2, :], w_ref[2:3, :], b_ref[...]
    return [prev[r] * w0 + p[r] * w1 + nxt[r] * w2 + b for r in range(RADIX)]


def _hyena_kernel(v_ref, x1_ref, x2_ref, wv_ref, w1_ref, w2_ref, bv_ref, b1_ref, b2_ref,
                  sp0_ref, kq0_ref, sp1_ref, kq1_ref, alt_ref, tab_ref, *refs, rc, n_cast):
    o_ref = refs[n_cast]
    zb_ref, p_ref, il_ref = refs[2 * n_cast + 1:]
    for src, dst in zip(refs[:n_cast], refs[n_cast + 1:2 * n_cast + 1]):
        dst[...] = src[...].astype(dst.dtype)
    Q = tab_ref.shape[1]
    R = RADIX
    chunks = [(r0, slice(r0, r0 + rc)) for r0 in range(0, Q, rc)]

    def put_z(zs, rows):
        alt = alt_ref[rows, :]
        for r in range(R):
            zb_ref[r, rows, :] = zs[r].astype(BF16)
        return [_colsum(z * alt) for z in zs]

    def spectrum_product(sp_ref):
        zb = [zb_ref[r] for r in range(R)]
        for _, rows in chunks:
            a = [_dot(tab_ref[r, rows, :], zb[r]) for r in range(R)]
            b = [_dot(tab_ref[R + r, rows, :], zb[r]) for r in range(R)]
            e, f = a[0] + a[2], a[0] - a[2]
            g, h = a[1] + a[3], a[1] - a[3]
            p, q = b[0] + b[2], b[0] - b[2]
            u, v = b[1] + b[3], b[1] - b[3]
            xr = [e + g, e - g, f + v, f - v]
            xi = [p + u, u - p, h - q, q + h]
            pr, pi = zip(*[_cmul(xr[s], xi[s], sp_ref[s, rows, :], sp_ref[R + s, rows, :])
                           for s in range(R)])
            s01p, s01m, s23p, s23m = pr[0] + pr[1], pr[0] - pr[1], pr[2] + pr[3], pr[2] - pr[3]
            t01p, t01m, t23p, t23m = pi[0] + pi[1], pi[0] - pi[1], pi[2] + pi[3], pi[2] - pi[3]
            for r, val in enumerate([s01p + s23p, s01m + t23p, s01p - s23p, s01m - t23p,
                                     t01m - t23m, t01p + s23m, t01m + t23m, t01p - s23m]):
                p_ref[r, rows, :] = val.astype(BF16)

    def conv_rows(rows, t, kq_ref):
        (xrq, xiq), (xr3, xi3) = _special_bins(t)
        prq, piq = _cmul(xrq, xiq, kq_ref[0:1, :], kq_ref[1:2, :])
        pr3, pi3 = _cmul(xr3, xi3, kq_ref[2:3, :], kq_ref[3:4, :])
        sp = [prq + pr3, SQH * (prq + piq + pi3 - pr3), piq - pi3, SQH * (piq - prq + pr3 + pi3)]
        alt = alt_ref[rows, :]
        return [_dot(tab_ref[2 * R + r, rows, :], p_ref[r]) + _dot(tab_ref[3 * R + r, rows, :], p_ref[R + r])
                + alt * sp[r] for r in range(R)]

    def add(acc, new):
        return new if acc is None else [x + y for x, y in zip(acc, new)]

    t = None
    for r0, rows in chunks:
        t = add(t, put_z(_short_conv_rows(v_ref, il_ref, wv_ref, bv_ref, r0, rc), rows))
    spectrum_product(sp0_ref)
    t1 = None
    for r0, rows in chunks:
        y = conv_rows(rows, t, kq0_ref)
        gate = _short_conv_rows(x1_ref, il_ref, w1_ref, b1_ref, r0, rc)
        t1 = add(t1, put_z([gate[r] * y[r] for r in range(R)], rows))
    spectrum_product(sp1_ref)
    for r0, rows in chunks:
        y = conv_rows(rows, t1, kq1_ref)
        gate = _short_conv_rows(x2_ref, il_ref, w2_ref, b2_ref, r0, rc)
        out = [gate[r] * y[r] for r in range(R)]
        for c in range(il_ref.shape[0]):
            lanes = slice(c * LANES, (c + 1) * LANES)
            for r in range(R):
                il_ref[c, pl.ds(r, rc, stride=R), :] = out[r][:, lanes]
            o_ref[0, R * r0:R * (r0 + rc), lanes] = il_ref[c].astype(o_ref.dtype)


def _hyena(p3, conv_w, conv_b, spectra, alt, tabs, n_ch, ct, to_cast):
    B, L, _ = p3.shape
    Q = L // RADIX
    nct = n_ch // ct
    rc = _pick(Q, 512)
    sp, kq = spectra
    cb = conv_b.reshape(1, -1).astype(F32)
    cw = conv_w.astype(F32)

    def pspec(g):
        return pl.BlockSpec((1, L, ct), lambda j, b, g=g: (b, 0, g * nct + j))

    def rowspec(rows, g):
        return pl.BlockSpec((rows, ct), lambda j, b, g=g: (0, g * nct + j))

    def specspec(g):
        return pl.BlockSpec((2 * RADIX, Q, ct), lambda j, b, g=g: (0, 0, g * nct + j))

    steps = nct * B
    cast_specs = []
    for w in to_cast:
        slab = w.shape[0] // steps
        assert slab * steps == w.shape[0] and slab % BF16_ROWS == 0, w.shape
        cast_specs.append(pl.BlockSpec((slab, w.shape[1]), lambda j, b: (j * B + b, 0)))

    return pl.pallas_call(
        functools.partial(_hyena_kernel, rc=rc, n_cast=len(to_cast)),
        grid=(nct, B),
        in_specs=[
            pspec(0), pspec(1), pspec(2),
            rowspec(3, 0), rowspec(3, 1), rowspec(3, 2),
            rowspec(1, 0), rowspec(1, 1), rowspec(1, 2),
            specspec(0), rowspec(RADIX, 0), specspec(1), rowspec(RADIX, 1),
            _resident((Q, 1), lambda j, b: (0, 0)),
            _resident(tabs.shape, lambda j, b: (0, 0, 0)),
        ] + cast_specs,
        out_specs=[pl.BlockSpec((1, L, ct), lambda j, b: (b, 0, j))] + cast_specs,
        out_shape=[jax.ShapeDtypeStruct((B, L, n_ch), BF16)]
                  + [jax.ShapeDtypeStruct(w.shape, BF16) for w in to_cast],
        scratch_shapes=[pltpu.VMEM((RADIX, Q, ct), BF16),
                        pltpu.VMEM((2 * RADIX, Q, ct), BF16),
                        pltpu.VMEM((ct // LANES, RADIX * rc, LANES), F32)],
        name="hyena",
        compiler_params=_params(60, 2),
    )(p3, p3, p3, cw, cw, cw, cb, cb, cb, sp, kq, sp, kq, alt, tabs, *to_cast)


def _gelu(x):
    return 0.5 * x * (1.0 + lax.erf(x * math.sqrt(0.5)))


def _spatial_gating(u_ref, v_ref, g_ref, b_ref, ws_ref, bs_ref, o_ref, heads, chunk):
    u = _gelu(u_ref[...].astype(F32))
    v = _gelu(v_ref[...].astype(F32))
    mu = jnp.mean(v, axis=-1, keepdims=True)
    vc = v - mu
    var = jnp.mean(vc * vc, axis=-1, keepdims=True)
    vn = (vc * lax.rsqrt(var + EPS) * g_ref[...] + b_ref[...]).astype(BF16)
    tm, ds = vn.shape
    hd = ds // heads
    n_chunks = tm // chunk
    group = 2 if n_chunks % 2 == 0 else 1
    for n in range(0, n_chunks, group):
        rows = [slice((n + k) * chunk, (n + k + 1) * chunk) for k in range(group)]
        for h in range(heads):
            cols = slice(h * hd, (h + 1) * hd)
            rhs = jnp.concatenate([vn[r, cols] for r in rows], axis=1)
            s = _dot(ws_ref[h], rhs) + bs_ref[:, h:h + 1]
            for k, r in enumerate(rows):
                o_ref[r, cols] = (u[r, cols] * s[:, k * hd:(k + 1) * hd]).astype(o_ref.dtype)


def _merge_kernel(x_ref, yh_ref, u_ref, v_ref, gh0_ref, gh1_ref, gs0_ref, gs1_ref,
                  lng_ref, lnb_ref, ws_ref, bs_ref, wbh_ref, wbs_ref, wo_ref, o_ref, ys_ref, *, heads, chunk):
    gh = jnp.concatenate([gh0_ref[...], gh1_ref[...]], axis=1).astype(F32)
    ah = _dot(yh_ref[...], wbh_ref[...]) * gh
    _spatial_gating(u_ref, v_ref, lng_ref, lnb_ref, ws_ref, bs_ref, ys_ref, heads, chunk)
    gs = jnp.concatenate([gs0_ref[...], gs1_ref[...]], axis=1).astype(F32)
    a = ah + _dot(ys_ref[...], wbs_ref[...]) * gs
    o_ref[...] = x_ref[...] + _dot(a.astype(BF16), wo_ref[...])


def _merge(x2, yh, proj, sgu_col, gate_col, ln_g, ln_b, w_s, b_s, wbh, wbs, wo, tm):
    M, D = x2.shape
    ds = wbs.shape[0]
    heads, chunk, _ = w_s.shape
    gw = D // 2
    gb = gate_col // gw
    sb = sgu_col // ds
    assert gate_col % gw == 0 and sgu_col % ds == 0 and tm % chunk == 0
    full = lambda i: (0, 0)

    def gate(c):
        return pl.BlockSpec((tm, gw), lambda i, c=c: (i, gb + c))

    return pl.pallas_call(
        functools.partial(_merge_kernel, heads=heads, chunk=chunk),
        grid=(M // tm,),
        in_specs=[
            pl.BlockSpec((tm, D), lambda i: (i, 0)),
            pl.BlockSpec((tm, yh.shape[1]), lambda i: (i, 0)),
            pl.BlockSpec((tm, ds), lambda i: (i, sb)),
            pl.BlockSpec((tm, ds), lambda i: (i, sb + 1)),
            gate(0), gate(1), gate(2), gate(3),
            _resident((1, ds), full),
            _resident((1, ds), full),
            _resident((heads, chunk, chunk), lambda i: (0, 0, 0)),
            _resident((chunk, heads), full),
            _resident(wbh.shape, full),
            _resident(wbs.shape, full),
            _resident(wo.shape, full),
        ],
        out_specs=pl.BlockSpec((tm, D), lambda i: (i, 0)),
        out_shape=jax.ShapeDtypeStruct((M, D), F32),
        scratch_shapes=[pltpu.VMEM((tm, ds), BF16)],
        name="merge",
        compiler_params=_params(60, 1),
    )(x2, yh, proj, proj, proj, proj, proj, proj,
      ln_g.reshape(1, ds).astype(F32), ln_b.reshape(1, ds).astype(F32),
      w_s.astype(BF16), b_s.T.astype(F32), wbh, wbs, wo)


def _ffn_kernel(x_ref, g_ref, wg_ref, wu_ref, wo_ref, gf_ref, o_ref, h_ref, *, nf, final_norm):
    f = pl.program_id(1)

    def tile(h):
        gate = _dot(h, wg_ref[...])
        up = _dot(h, wu_ref[...])
        a = (gate * _sigmoid(gate) * up).astype(BF16)
        return _dot(a, wo_ref[...])

    @pl.when(f == 0)
    def _():
        x = x_ref[...]
        h = _rms(x, g_ref[...]).astype(BF16)
        h_ref[...] = h
        o_ref[...] = x + tile(h)

    last = nf - 1
    fuse_norm = final_norm and nf > 1

    @pl.when((f > 0) & (f < last) if fuse_norm else f > 0)
    def _():
        o_ref[...] += tile(h_ref[...])

    if fuse_norm:
        @pl.when(f == last)
        def _():
            o_ref[...] = _rms(o_ref[...] + tile(h_ref[...]), gf_ref[...])
    elif final_norm:
        @pl.when(f == last)
        def _():
            o_ref[...] = _rms(o_ref[...], gf_ref[...])


def _ffn(x2, g, w_in, w_out, g_final, final_norm, tm, tf):
    M, D = x2.shape
    FF = w_out.shape[0]
    nf = FF // tf
    full = lambda i, f: (0, 0)
    return pl.pallas_call(
        functools.partial(_ffn_kernel, nf=nf, final_norm=final_norm),
        grid=(M // tm, nf),
        in_specs=[
            pl.BlockSpec((tm, D), lambda i, f: (i, 0)),
            _resident((1, D), full),
            pl.BlockSpec((D, tf), lambda i, f: (0, f)),
            pl.BlockSpec((D, tf), lambda i, f: (0, nf + f)),
            pl.BlockSpec((tf, D), lambda i, f: (f, 0)),
            _resident((1, D), full),
        ],
        out_specs=pl.BlockSpec((tm, D), lambda i, f: (i, 0)),
        out_shape=jax.ShapeDtypeStruct((M, D), F32),
        scratch_shapes=[pltpu.VMEM((tm, D), BF16)],
        name="ffn",
        compiler_params=_params(60, 2),
    )(x2, g, w_in, w_in, w_out, g_final)


def _transform_tables(L):
    Q = L // RADIX
    k = np.arange(Q, dtype=np.int64)[:, None]
    m = np.arange(Q, dtype=np.int64)[None, :]

    def tab(fn, r):
        return fn(((k * (RADIX * m + r)) % (2 * L)) * (np.pi / L)).astype(np.float32)

    cs = [tab(np.cos, r) for r in range(RADIX)] + [tab(np.sin, r) for r in range(RADIX)]
    stack = np.stack(cs + [t.T for t in cs], axis=0)
    alt = (1.0 - 2.0 * (np.arange(Q) % 2)).astype(np.float32)[:, None]
    return jnp.asarray(stack).astype(BF16), jnp.asarray(alt)


def _position_features(L, emb):
    Q = L // RADIX
    row = np.arange(L)
    n = ((row % Q) * RADIX + row // Q).astype(np.float64)[:, None]
    t = n / (L - 1)
    bands = (emb - 1) // 2
    w = 2.0 * np.pi * n / L
    f = np.linspace(1e-4, bands - 1, bands)[None, :]
    z = np.concatenate([t, np.cos(f * w), -np.sin(f * w)], axis=-1)
    return jnp.asarray(z.astype(np.float32)), jnp.asarray(t.astype(np.float32))


def _pick(n, pref):
    t = min(n, pref)
    while n % t:
        t -= 1
    return t


def kernel(x, norm_mix_g, w_in, short_conv_w, short_conv_b, filt_w1, filt_b1, filt_w2, filt_b2, filt_w3, filt_b3, filt_freq, filt_w4, hyena_bias, sgu_ln_g, sgu_ln_b, sgu_w_s, sgu_b_s, w_branch_hyena, w_branch_sgu, w_out, norm_ffn_g, w_ffn_in, w_ffn_out, norm_final_g):
    B, L, D = x.shape
    depth = w_in.shape[0]
    n_ch = w_branch_hyena.shape[1]
    ds = w_branch_sgu.shape[1]
    assert n_ch == ds and 2 * n_ch == D and L % RADIX == 0 and depth >= 1
    n_mix = 3 * n_ch + 2 * ds
    M = B * L

    tabs, alt = _transform_tables(L)
    zfeat, t_col = _position_features(L, filt_w1.shape[1])
    max_decay = math.log(DECAY_TARGET) / FAST_DECAY
    min_decay = math.log(DECAY_TARGET) / SLOW_DECAY
    absdelta = jnp.asarray(np.abs(np.linspace(min_decay, max_decay, n_ch)).astype(np.float32)[None, :])

    ct = _pick(n_ch, 256)
    tm = _pick(M, 1024)
    tn = LANES * _pick(w_in.shape[2] // LANES, 2304 // LANES)

    x2 = x.reshape(M, D)
    for l in range(depth):
        spectra, w_in_bf16 = _hyena_filters(zfeat, t_col, absdelta, alt, tabs,
                                            filt_w1[l], filt_b1[l], filt_w2[l], filt_b2[l],
                                            filt_w3[l], filt_b3[l], filt_freq[l], filt_w4[l],
                                            hyena_bias[l], ct, w_in[l])
        proj = _proj(x2, norm_mix_g[l].reshape(1, D), w_in_bf16, n_mix, tm, tn)
        y_hy, wbh, wbs, wo, wfi, wfo = _hyena(
            proj.reshape(B, L, proj.shape[1]), short_conv_w[l], short_conv_b[l], spectra,
            alt, tabs, n_ch, ct,
            [w_branch_hyena[l], w_branch_sgu[l], w_out[l], w_ffn_in[l], w_ffn_out[l]])
        x2 = _merge(x2, y_hy.reshape(M, n_ch), proj, 3 * n_ch, n_mix,
                    sgu_ln_g[l], sgu_ln_b[l], sgu_w_s[l], sgu_b_s[l], wbh, wbs, wo, _pick(M, 512))
        last = l == depth - 1
        x2 = _ffn(x2, norm_ffn_g[l].reshape(1, D), wfi, wfo, norm_final_g.reshape(1, D), last,
                  _pick(M, 1024), _pick(w_ffn_out.shape[1], 512))
    return x2.reshape(B, L, D)
```
